```python
import jax, jax.numpy as jnp
from jax import lax
import numpy as np

D_MODEL = 1024
BATCH = 8
SEQ = 2048
DEPTH = 2

GRID_W = 64
CTX_LEN = 256
D_MIX = D_MODEL
HEAD_DIM = 64
ATT_W = D_MIX // 2
N_HEADS = ATT_W // HEAD_DIM
N_KV_HEADS = 2
GQA_GROUP = N_HEADS // N_KV_HEADS
KV_W = N_KV_HEADS * HEAD_DIM
CONV_W = D_MIX // 4
FOUR_W = D_MIX - ATT_W - CONV_W
FOUR_HEADS = 4
FOUR_HEAD_DIM = FOUR_W // FOUR_HEADS
CONV_K = 31
WINDOW = 128
BLOCK = 128
ROPE_BASE = 10000.0
EPS = 1e-6
NEG_INF = -1e30
SPLITS = (ATT_W, KV_W, KV_W, ATT_W, CONV_W, CONV_W, CONV_W, FOUR_W, FOUR_W)
IN_W = ATT_W + 2 * KV_W + ATT_W + 3 * CONV_W + 2 * FOUR_W

kernel_name = 'hybrid_conv_fourier_swa_diffusion_block'


def rms_norm(x, g):
    xf = x.astype(jnp.float32)
    y = xf * lax.rsqrt(jnp.mean(xf * xf, axis=-1, keepdims=True) + EPS)
    return (y * g.astype(jnp.float32)).astype(x.dtype)


def layer_norm(x, g, b):
    xf = x.astype(jnp.float32)
    mu = jnp.mean(xf, axis=-1, keepdims=True)
    var = jnp.mean(jnp.square(xf - mu), axis=-1, keepdims=True)
    y = (xf - mu) * lax.rsqrt(var + EPS)
    return (y * g.astype(jnp.float32) + b.astype(jnp.float32)).astype(x.dtype)


def split_cols(p):
    outs = []
    off = 0
    for w in SPLITS:
        outs.append(p[..., off:off + w])
        off += w
    return outs


def rope_axis(x, pos):
    half = x.shape[-1] // 2
    freqs = ROPE_BASE ** (-jnp.arange(half, dtype=jnp.float32) / half)
    ang = pos[:, None] * freqs[None, :]
    cos = jnp.cos(ang)[:, None, :].astype(x.dtype)
    sin = jnp.sin(ang)[:, None, :].astype(x.dtype)
    x1, x2 = x[..., :half], x[..., half:]
    return jnp.concatenate([x1 * cos - x2 * sin, x1 * sin + x2 * cos], axis=-1)


def rope_2d(x, row, col):
    h = x.shape[-1] // 2
    return jnp.concatenate([rope_axis(x[..., :h], row), rope_axis(x[..., h:], col)], axis=-1)


def windowed_attention(q, k, v, kc, vc, sink):
    bn, s = q.shape[0], q.shape[1]
    nb = s // BLOCK
    n_ctx = kc.shape[1]
    scale = HEAD_DIM ** -0.5
    qb = q.reshape(bn, nb, BLOCK, N_KV_HEADS, GQA_GROUP, HEAD_DIM)
    pad = ((0, 0), (BLOCK, BLOCK), (0, 0), (0, 0))
    kp = jnp.pad(k, pad).reshape(bn, nb + 2, BLOCK, N_KV_HEADS, HEAD_DIM)
    vp = jnp.pad(v, pad).reshape(bn, nb + 2, BLOCK, N_KV_HEADS, HEAD_DIM)
    kw = jnp.concatenate([kp[:, :-2], kp[:, 1:-1], kp[:, 2:]], axis=2)
    vw = jnp.concatenate([vp[:, :-2], vp[:, 1:-1], vp[:, 2:]], axis=2)
    s_loc = jnp.einsum('bnqhgd,bnkhd->bnhgqk', qb, kw).astype(jnp.float32) * scale
    q_rel = jnp.arange(BLOCK) + BLOCK
    k_rel = jnp.arange(3 * BLOCK)
    band = jnp.abs(q_rel[:, None] - k_rel[None, :]) <= WINDOW
    k_abs = jnp.arange(nb)[:, None] * BLOCK - BLOCK + k_rel[None, :]
    valid = (k_abs >= 0) & (k_abs < s)
    mask = band[None, :, :] & valid[:, None, :]
    s_loc = jnp.where(mask[None, :, None, None], s_loc, NEG_INF)
    s_ctx = jnp.einsum('bnqhgd,blhd->bnhgql', qb, kc).astype(jnp.float32) * scale
    s_sink = jnp.broadcast_to(
        sink.astype(jnp.float32).reshape(1, 1, N_KV_HEADS, GQA_GROUP, 1, 1),
        s_loc.shape[:-1] + (1,))
    p = jax.nn.softmax(jnp.concatenate([s_loc, s_ctx, s_sink], axis=-1), axis=-1)
    p_loc = p[..., :3 * BLOCK].astype(v.dtype)
    p_ctx = p[..., 3 * BLOCK:3 * BLOCK + n_ctx].astype(v.dtype)
    o = (jnp.einsum('bnhgqk,bnkhd->bnqhgd', p_loc, vw)
         + jnp.einsum('bnhgql,blhd->bnqhgd', p_ctx, vc))
    return o.reshape(bn, s, ATT_W)


def context_attention(q, k, v, sink):
    bn, n_ctx = q.shape[0], q.shape[1]
    scale = HEAD_DIM ** -0.5
    qh = q.reshape(bn, n_ctx, N_KV_HEADS, GQA_GROUP, HEAD_DIM)
    sc = jnp.einsum('blhgd,bmhd->bhglm', qh, k).astype(jnp.float32) * scale
    s_sink = jnp.broadcast_to(
        sink.astype(jnp.float32).reshape(1, N_KV_HEADS, GQA_GROUP, 1, 1), sc.shape[:-1] + (1,))
    p = jax.nn.softmax(jnp.concatenate([sc, s_sink], axis=-1), axis=-1)
    o = jnp.einsum('bhglm,bmhd->blhgd', p[..., :n_ctx].astype(v.dtype), v)
    return o.reshape(bn, n_ctx, ATT_W)


def conformer_conv(a, b_glu, conv_w, conv_b, ln_g, ln_b):
    u = a * jax.nn.sigmoid(b_glu)
    y = lax.conv_general_dilated(
        u, conv_w[:, None, :], window_strides=(1,),
        padding=[(CONV_K // 2, CONV_K // 2)],
        dimension_numbers=('NWC', 'WIO', 'NWC'),
        feature_group_count=CONV_W) + conv_b
    return jax.nn.silu(layer_norm(y, ln_g, ln_b))


def fourier_mix(u, w_four, b_four):
    bn, n = u.shape[0], u.shape[1]
    uh = u.reshape(bn, n, FOUR_HEADS, FOUR_HEAD_DIM).astype(jnp.float32)
    f = jnp.fft.fftn(uh, axes=(1, 3), norm='ortho').real
    f = f.reshape(bn, n, FOUR_W).astype(u.dtype)
    return f @ w_four + b_four


def mixer_out(attn, parts, conv_w, conv_b, ln_g, ln_b, w_four, b_four, w_out):
    _, _, _, g_att, c_a, c_b, g_conv, f_u, g_four = parts
    y_conv = conformer_conv(c_a, c_b, conv_w, conv_b, ln_g, ln_b)
    y_four = fourier_mix(f_u, w_four, b_four)
    y = jnp.concatenate([attn * jax.nn.silu(g_att),
                         y_conv * jax.nn.silu(g_conv),
                         y_four * jax.nn.silu(g_four)], axis=-1)
    return y @ w_out


def setup_inputs(seed: int = 0) -> dict:
    key = jax.random.key(seed)
    ks = jax.random.split(key, 20)
    f32 = jnp.float32
    nrm = lambda k, shp, s: jax.random.normal(k, shp, f32) * s
    return {
        'x': nrm(ks[0], (BATCH, SEQ, D_MODEL), 1.0),
        'c': nrm(ks[1], (BATCH, D_MODEL), 1.0),
        'ctx': nrm(ks[2], (BATCH, CTX_LEN, D_MODEL), 1.0),
        'c_ctx': nrm(ks[3], (D_MODEL,), 1.0),
        'w_ada': nrm(ks[4], (DEPTH, D_MODEL, 3 * D_MODEL), 0.5 * D_MODEL ** -0.5),
        'b_ada': nrm(ks[5], (DEPTH, 3 * D_MODEL), 0.02),
        'norm_g': 1.0 + nrm(ks[6], (DEPTH, D_MODEL), 0.02),
        'w_in': nrm(ks[7], (DEPTH, D_MODEL, IN_W), D_MODEL ** -0.5),
        'attn_sink': nrm(ks[8], (DEPTH, N_HEADS), 0.5),
        'conv_w': nrm(ks[9], (DEPTH, CONV_K, CONV_W), CONV_K ** -0.5),
        'conv_b': nrm(ks[10], (DEPTH, CONV_W), 0.02),
        'conv_ln_g': 1.0 + nrm(ks[11], (DEPTH, CONV_W), 0.02),
        'conv_ln_b': nrm(ks[12], (DEPTH, CONV_W), 0.02),
        'w_four': nrm(ks[13], (DEPTH, FOUR_W, FOUR_W), FOUR_W ** -0.5),
        'b_four': nrm(ks[14], (DEPTH, FOUR_W), 0.02),
        'w_out': nrm(ks[15], (DEPTH, D_MIX, D_MODEL), D_MIX ** -0.5),
        'final_g': 1.0 + nrm(ks[16], (D_MODEL,), 0.02),
    }


def reference(x, c, ctx, c_ctx, w_ada, b_ada, norm_g, w_in, attn_sink, conv_w, conv_b,
              conv_ln_g, conv_ln_b, w_four, b_four, w_out, final_g):
    bn, s, _ = x.shape
    n_ctx = ctx.shape[1]
    ROWS = s // GRID_W
    row_pos = jnp.repeat(jnp.arange(ROWS, dtype=jnp.float32), GRID_W)
    col_pos = jnp.tile(jnp.arange(GRID_W, dtype=jnp.float32), ROWS)
    sc = jax.nn.silu(c)
    scc = jax.nn.silu(c_ctx)
    h_ctx = ctx
    for l in range(DEPTH):
        shift, scale, gate = jnp.split(sc @ w_ada[l] + b_ada[l], 3, axis=-1)
        shift_c, scale_c, gate_c = jnp.split(scc @ w_ada[l] + b_ada[l], 3, axis=-1)
        hc = rms_norm(h_ctx, norm_g[l]) * (1.0 + scale_c) + shift_c
        if l < DEPTH - 1:
            pc = split_cols(hc @ w_in[l])
            kc_raw, vc_raw = pc[1], pc[2]
        else:
            kv = hc @ w_in[l][:, ATT_W:ATT_W + 2 * KV_W]
            kc_raw, vc_raw = kv[..., :KV_W], kv[..., KV_W:]
        kc = kc_raw.reshape(bn, n_ctx, N_KV_HEADS, HEAD_DIM)
        vc = vc_raw.reshape(bn, n_ctx, N_KV_HEADS, HEAD_DIM)

        hx = rms_norm(x, norm_g[l]) * (1.0 + scale[:, None, :]) + shift[:, None, :]
        px = split_cols(hx @ w_in[l])
        q = rope_2d(px[0].reshape(bn, s, N_HEADS, HEAD_DIM), row_pos, col_pos)
        k = rope_2d(px[1].reshape(bn, s, N_KV_HEADS, HEAD_DIM), row_pos, col_pos)
        v = px[2].reshape(bn, s, N_KV_HEADS, HEAD_DIM)
        attn = windowed_attention(q, k, v, kc, vc, attn_sink[l])
        y = mixer_out(attn, px, conv_w[l], conv_b[l], conv_ln_g[l], conv_ln_b[l],
                      w_four[l], b_four[l], w_out[l])

        if l < DEPTH - 1:
            qc = pc[0].reshape(bn, n_ctx, N_HEADS, HEAD_DIM)
            attn_c = context_attention(qc, kc, vc, attn_sink[l])
            yc = mixer_out(attn_c, pc, conv_w[l], conv_b[l], conv_ln_g[l], conv_ln_b[l],
                           w_four[l], b_four[l], w_out[l])
            h_ctx = h_ctx + gate_c * yc
        x = x + gate[:, None, :] * y
    return rms_norm(x, final_g)
```

```python
import functools

import numpy as np
import jax
import jax.numpy as jnp
from jax import lax
from jax.experimental import pallas as pl
from jax.experimental.pallas import tpu as pltpu

F32 = jnp.float32
BF16 = jnp.bfloat16

D_MODEL = 1024
DEPTH = 2
GRID_W = 64
HEAD_DIM = 64
ATT_W = 512
N_HEADS = 8
N_KV_HEADS = 2
GQA_GROUP = N_HEADS // N_KV_HEADS
KV_W = N_KV_HEADS * HEAD_DIM
CONV_W = 256
FOUR_W = 256
FOUR_HEADS = 4
FOUR_HEAD_DIM = FOUR_W // FOUR_HEADS
CONV_K = 31
CONV_HALO = 16
WINDOW = 128
BLOCK = 128
ROPE_BASE = 10000.0
EPS = 1e-6
NEG_INF = -1e30
IN_W = 2 * ATT_W + 2 * KV_W + 3 * CONV_W + 2 * FOUR_W
OFF_Q = 0
OFF_KV = ATT_W
OFF_GATT = OFF_KV + 2 * KV_W
OFF_CONV = OFF_GATT + ATT_W
OFF_GCONV = OFF_CONV + 2 * CONV_W
OFF_FOUR = OFF_GCONV + CONV_W
OFF_GFOUR = OFF_FOUR + FOUR_W
LANES = 128
MOD_ROWS = 16


def _dot(a, b):
    return jnp.dot(a, b, preferred_element_type=F32)


def _dot_nt(a, b):
    return lax.dot_general(a, b, (((1,), (1,)), ((), ())), preferred_element_type=F32)


def _silu(x):
    return x * jax.nn.sigmoid(x)


def _ada_kernel(cc_ref, w_ref, b_ref, o_ref):
    a = _silu(cc_ref[...])
    w = w_ref[0]
    a_hi = a.astype(BF16)
    a_lo = (a - a_hi.astype(F32)).astype(BF16)
    w_hi = w.astype(BF16)
    w_lo = (w - w_hi.astype(F32)).astype(BF16)
    acc = _dot(a_hi, w_hi) + _dot(a_lo, w_hi) + _dot(a_hi, w_lo)
    o_ref[0] = acc + b_ref[0]


def _modulation(cc, w_ada, b_ada):
    tn = 768
    return pl.pallas_call(
        _ada_kernel,
        grid=(DEPTH, 3 * D_MODEL // tn),
        in_specs=[
            pl.BlockSpec((MOD_ROWS, D_MODEL), lambda l, j: (0, 0)),
            pl.BlockSpec((1, D_MODEL, tn), lambda l, j: (l, 0, j)),
            pl.BlockSpec((1, 1, tn), lambda l, j: (l, 0, j)),
        ],
        out_specs=pl.BlockSpec((1, MOD_ROWS, tn), lambda l, j: (l, 0, j)),
        out_shape=jax.ShapeDtypeStruct((DEPTH, MOD_ROWS, 3 * D_MODEL), F32),
        name="ada_modulation",
    )(cc, w_ada, b_ada.reshape(DEPTH, 1, 3 * D_MODEL))


def _norm_modulate(x_ref, sh_ref, sc_ref, g_ref):
    x = x_ref[0]
    ms = jnp.mean(x * x, axis=-1, keepdims=True)
    h = x * lax.rsqrt(ms + EPS) * g_ref[...]
    h = h * (1.0 + sc_ref[0]) + sh_ref[0]
    return h.astype(BF16)


def _rope_block(blk, rope_ref, base):
    cos = rope_ref[:, base:base + LANES]
    sin_lo = rope_ref[:, base + LANES:base + 2 * LANES]
    sin_hi = rope_ref[:, base + 2 * LANES:base + 3 * LANES]
    return (blk * cos + pltpu.roll(blk, LANES - 16, 1) * sin_lo + pltpu.roll(blk, 16, 1) * sin_hi)


def _in_kernel(x_ref, sh_ref, sc_ref, g_ref, w_ref, rope_ref, dft_ref,
               q_ref, kv_ref, gatt_ref, uc_ref, gcv_ref, ab_ref, gf_ref):
    hb = _norm_modulate(x_ref, sh_ref, sc_ref, g_ref)

    def seg(a, b):
        return _dot(hb, w_ref[:, a:b])

    qp = seg(OFF_Q, OFF_Q + ATT_W)
    for c in range(ATT_W // LANES):
        blk = _rope_block(qp[:, c * LANES:(c + 1) * LANES], rope_ref, 0)
        q_ref[0, :, c * LANES:(c + 1) * LANES] = blk.astype(BF16)
    kvp = seg(OFF_KV, OFF_KV + 2 * KV_W)
    kv_ref[0, :, 0:KV_W] = _rope_block(kvp[:, 0:KV_W], rope_ref, 3 * LANES).astype(BF16)
    kv_ref[0, :, KV_W:2 * KV_W] = kvp[:, KV_W:2 * KV_W].astype(BF16)
    gatt_ref[0] = _silu(seg(OFF_GATT, OFF_GATT + ATT_W)).astype(BF16)
    cab = seg(OFF_CONV, OFF_CONV + 2 * CONV_W)
    uc_ref[0] = (cab[:, :CONV_W] * jax.nn.sigmoid(cab[:, CONV_W:])).astype(BF16)
    gcv_ref[0] = _silu(seg(OFF_GCONV, OFF_GCONV + CONV_W)).astype(BF16)
    fu = seg(OFF_FOUR, OFF_FOUR + FOUR_W).astype(BF16)
    ab_ref[0] = _dot(fu, dft_ref[...]).astype(BF16)
    gf_ref[0] = _silu(seg(OFF_GFOUR, OFF_GFOUR + FOUR_W)).astype(BF16)


def _mod_specs(mod_row):
    row = (lambda b, i: mod_row) if mod_row is not None else (lambda b, i: b)
    return [pl.BlockSpec((1, 1, D_MODEL), lambda b, i, j=j: (row(b, i), 0, j)) for j in (0, 1)]


def _in_proj(x, mod3, norm_g, w_bf, rope_tab, dft_c, *, tm, mod_row):
    bn, s, _ = x.shape
    widths = (ATT_W, 2 * KV_W, ATT_W, CONV_W, CONV_W, 2 * FOUR_W, FOUR_W)
    return pl.pallas_call(
        _in_kernel,
        grid=(bn, s // tm),
        in_specs=[pl.BlockSpec((1, tm, D_MODEL), lambda b, i: (b, i, 0))]
        + _mod_specs(mod_row)
        + [
            pl.BlockSpec((1, D_MODEL), lambda b, i: (0, 0)),
            pl.BlockSpec((D_MODEL, IN_W), lambda b, i: (0, 0)),
            pl.BlockSpec((tm, 6 * LANES), lambda b, i: (i, 0)),
            pl.BlockSpec((FOUR_W, 2 * FOUR_W), lambda b, i: (0, 0)),
        ],
        out_specs=[pl.BlockSpec((1, tm, w), lambda b, i: (b, i, 0)) for w in widths],
        out_shape=[jax.ShapeDtypeStruct((bn, s, w), BF16) for w in widths],
        name="in_proj",
    )(x, mod3, mod3, norm_g.reshape(1, D_MODEL), w_bf, rope_tab, dft_c)


def _kv_kernel(x_ref, sh_ref, sc_ref, g_ref, w_ref, kv_ref):
    hb = _norm_modulate(x_ref, sh_ref, sc_ref, g_ref)
    kv_ref[0] = _dot(hb, w_ref[...]).astype(BF16)


def _kv_proj(x, mod3, norm_g, w_kv_bf, *, tm, mod_row):
    bn, s, _ = x.shape
    return pl.pallas_call(
        _kv_kernel,
        grid=(bn, s // tm),
        in_specs=[pl.BlockSpec((1, tm, D_MODEL), lambda b, i: (b, i, 0))]
        + _mod_specs(mod_row)
        + [
            pl.BlockSpec((1, D_MODEL), lambda b, i: (0, 0)),
            pl.BlockSpec((D_MODEL, 2 * KV_W), lambda b, i: (0, 0)),
        ],
        out_specs=pl.BlockSpec((1, tm, 2 * KV_W), lambda b, i: (b, i, 0)),
        out_shape=jax.ShapeDtypeStruct((bn, s, 2 * KV_W), BF16),
        name="ctx_kv_proj",
    )(x, mod3, mod3, norm_g.reshape(1, D_MODEL), w_kv_bf)


def _attn_heads(sink_ref, q_ref, g_ref, o_ref, keys):
    for hd in range(N_HEADS):
        kh = hd // GQA_GROUP
        q = q_ref[0, :, hd * HEAD_DIM:(hd + 1) * HEAD_DIM]
        sink = sink_ref[hd]
        scores = []
        for k, _, mask in keys[kh]:
            sc = _dot_nt(q, k)
            if mask is not None:
                sc = jnp.where(mask, sc, NEG_INF)
            scores.append(sc)
        m = jnp.maximum(scores[0].max(axis=-1, keepdims=True), sink)
        for sc in scores[1:]:
            m = jnp.maximum(m, sc.max(axis=-1, keepdims=True))
        denom = jnp.exp(sink - m)
        acc = None
        for sc, (_, v, _) in zip(scores, keys[kh]):
            p = jnp.exp(sc - m)
            denom = denom + p.sum(axis=-1, keepdims=True)
            pv = _dot(p.astype(BF16), v)
            acc = pv if acc is None else acc + pv
        o = acc / denom
        g = g_ref[0, :, hd * HEAD_DIM:(hd + 1) * HEAD_DIM].astype(F32)
        o_ref[0, :, hd * HEAD_DIM:(hd + 1) * HEAD_DIM] = (o * g).astype(BF16)


def _local_attn_kernel(sink_ref, q_ref, kv_ref, kvc_ref, g_ref, o_ref, *, seq):
    n = pl.program_id(1)
    start = pl.multiple_of(jnp.clip((n - 1) * BLOCK, 0, seq - 3 * BLOCK), BLOCK)
    qpos = n * BLOCK + lax.broadcasted_iota(jnp.int32, (BLOCK, 3 * BLOCK), 0)
    kpos = start + lax.broadcasted_iota(jnp.int32, (BLOCK, 3 * BLOCK), 1)
    band = jnp.abs(qpos - kpos) <= WINDOW
    keys = []
    for kh in range(N_KV_HEADS):
        ko, vo = kh * HEAD_DIM, KV_W + kh * HEAD_DIM
        keys.append([
            (kv_ref[0, pl.ds(start, 3 * BLOCK), ko:ko + HEAD_DIM],
             kv_ref[0, pl.ds(start, 3 * BLOCK), vo:vo + HEAD_DIM], band),
            (kvc_ref[0, :, ko:ko + HEAD_DIM], kvc_ref[0, :, vo:vo + HEAD_DIM], None),
        ])
    _attn_heads(sink_ref, q_ref, g_ref, o_ref, keys)


def _local_attention(sink, q, kv, kvc, gatt):
    bn, s, _ = q.shape
    n_ctx = kvc.shape[1]
    return pl.pallas_call(
        functools.partial(_local_attn_kernel, seq=s),
        grid=(bn, s // BLOCK),
        in_specs=[
            pl.BlockSpec(memory_space=pltpu.SMEM),
            pl.BlockSpec((1, BLOCK, ATT_W), lambda b, n: (b, n, 0)),
            pl.BlockSpec((1, s, 2 * KV_W), lambda b, n: (b, 0, 0)),
            pl.BlockSpec((1, n_ctx, 2 * KV_W), lambda b, n: (b, 0, 0)),
            pl.BlockSpec((1, BLOCK, ATT_W), lambda b, n: (b, n, 0)),
        ],
        out_specs=pl.BlockSpec((1, BLOCK, ATT_W), lambda b, n: (b, n, 0)),
        out_shape=jax.ShapeDtypeStruct((bn, s, ATT_W), BF16),
        name="local_attention",
    )(sink, q, kv, kvc, gatt)


def _ctx_attn_kernel(sink_ref, q_ref, kvc_ref, g_ref, o_ref):
    keys = []
    for kh in range(N_KV_HEADS):
        ko, vo = kh * HEAD_DIM, KV_W + kh * HEAD_DIM
        keys.append([(kvc_ref[0, :, ko:ko + HEAD_DIM], kvc_ref[0, :, vo:vo + HEAD_DIM], None)])
    _attn_heads(sink_ref, q_ref, g_ref, o_ref, keys)


def _ctx_attention(sink, q, kvc, gatt):
    bn, n_ctx, _ = q.shape
    return pl.pallas_call(
        _ctx_attn_kernel,
        grid=(bn,),
        in_specs=[
            pl.BlockSpec(memory_space=pltpu.SMEM),
            pl.BlockSpec((1, n_ctx, ATT_W), lambda b: (b, 0, 0)),
            pl.BlockSpec((1, n_ctx, 2 * KV_W), lambda b: (b, 0, 0)),
            pl.BlockSpec((1, n_ctx, ATT_W), lambda b: (b, 0, 0)),
        ],
        out_specs=pl.BlockSpec((1, n_ctx, ATT_W), lambda b: (b, 0, 0)),
        out_shape=jax.ShapeDtypeStruct((bn, n_ctx, ATT_W), BF16),
        name="ctx_attention",
    )(sink, q, kvc, gatt)


def _conv_kernel(u_ref, g_ref, w_ref, b_ref, lg_ref, lb_ref, o_ref, pad_ref, *, seq, chunk):
    zeros = jnp.zeros((CONV_HALO, CONV_W), F32)
    pad_ref[0:CONV_HALO, :] = zeros
    pad_ref[CONV_HALO + seq:2 * CONV_HALO + seq, :] = zeros
    pad_ref[CONV_HALO:CONV_HALO + seq, :] = u_ref[0].astype(F32)

    for i in range(seq // chunk):
        t0 = i * chunk
        acc = jnp.zeros((chunk, CONV_W), F32)
        for j in range(CONV_K):
            off = CONV_HALO - CONV_K // 2 + j
            acc = acc + pad_ref[t0 + off:t0 + off + chunk, :] * w_ref[j:j + 1, :]
        y = acc + b_ref[...]
        mu = jnp.mean(y, axis=-1, keepdims=True)
        yc = y - mu
        var = jnp.mean(yc * yc, axis=-1, keepdims=True)
        yn = yc * lax.rsqrt(var + EPS) * lg_ref[...] + lb_ref[...]
        g = g_ref[0, t0:t0 + chunk, :].astype(F32)
        o_ref[0, t0:t0 + chunk, :] = (_silu(yn) * g).astype(BF16)


def _conv_branch(uc, gcv, conv_w, conv_b, ln_g, ln_b):
    bn, s, _ = uc.shape
    row = lambda a: a.reshape(1, CONV_W)
    vec_spec = pl.BlockSpec((1, CONV_W), lambda b: (0, 0))
    seq_spec = pl.BlockSpec((1, s, CONV_W), lambda b: (b, 0, 0))
    return pl.pallas_call(
        functools.partial(_conv_kernel, seq=s, chunk=128),
        grid=(bn,),
        in_specs=[seq_spec, seq_spec, pl.BlockSpec((CONV_K, CONV_W), lambda b: (0, 0)),
                  vec_spec, vec_spec, vec_spec],
        out_specs=seq_spec,
        out_shape=jax.ShapeDtypeStruct((bn, s, CONV_W), BF16),
        scratch_shapes=[pltpu.VMEM((s + 2 * CONV_HALO, CONV_W), F32)],
        name="conv_branch",
    )(uc, gcv, conv_w, row(conv_b), row(ln_g), row(ln_b))


def _four_kernel(ab_ref, g_ref, c_ref, s_ref, cw_ref, sw_ref, wf_ref, bf_ref, o_ref,
                 rc_ref, rs_ref, *, half, chunk):
    top = ab_ref[0, 0:half, :].astype(F32)
    bot = ab_ref[0, half:2 * half, :].astype(F32)
    plus = top + bot
    minus = top - bot
    a_m, b_m = minus[:, :FOUR_W], minus[:, FOUR_W:]
    cw, sw = cw_ref[...], sw_ref[...]
    rc_ref[:, 0:FOUR_W] = plus[:, :FOUR_W].astype(BF16)
    rc_ref[:, FOUR_W:] = (cw * a_m - sw * b_m).astype(BF16)
    rs_ref[:, 0:FOUR_W] = plus[:, FOUR_W:].astype(BF16)
    rs_ref[:, FOUR_W:] = (sw * a_m + cw * b_m).astype(BF16)

    def body(i, carry):
        r0 = pl.multiple_of(i * chunk, chunk)
        e = _dot(c_ref[pl.ds(r0, chunk), :], rc_ref[...]) - _dot(s_ref[pl.ds(r0, chunk), :], rs_ref[...])
        eb = e.astype(BF16)
        for p in range(2):
            f = _dot(eb[:, p * FOUR_W:(p + 1) * FOUR_W], wf_ref[...]) + bf_ref[...]
            g = g_ref[0, pl.ds(r0, chunk), p * FOUR_W:(p + 1) * FOUR_W].astype(F32)
            o_ref[0, pl.ds(r0, chunk), p * FOUR_W:(p + 1) * FOUR_W] = (f * g).astype(BF16)
        return carry

    lax.fori_loop(0, half // chunk, body, 0)


def _fourier_branch(ab, gf, tabs, w_four_bf, b_four):
    bn, s, _ = ab.shape
    half = s // 2
    c_half, s_half, cw, sw = tabs
    gf2 = gf.reshape(bn, half, 2 * FOUR_W)
    const = lambda shape: pl.BlockSpec(shape, lambda b: (0,) * len(shape))
    out = pl.pallas_call(
        functools.partial(_four_kernel, half=half, chunk=min(256, half)),
        grid=(bn,),
        in_specs=[
            pl.BlockSpec((1, s, 2 * FOUR_W), lambda b: (b, 0, 0)),
            pl.BlockSpec((1, half, 2 * FOUR_W), lambda b: (b, 0, 0)),
            const((half, half)), const((half, half)),
            const((half, FOUR_W)), const((half, FOUR_W)),
            const((FOUR_W, FOUR_W)), const((1, FOUR_W)),
        ],
        out_specs=pl.BlockSpec((1, half, 2 * FOUR_W), lambda b: (b, 0, 0)),
        out_shape=jax.ShapeDtypeStruct((bn, half, 2 * FOUR_W), BF16),
        scratch_shapes=[pltpu.VMEM((half, 2 * FOUR_W), BF16), pltpu.VMEM((half, 2 * FOUR_W), BF16)],
        name="fourier_branch",
    )(ab, gf2, c_half, s_half, cw, sw, w_four_bf, b_four.reshape(1, FOUR_W))
    return out.reshape(bn, s, FOUR_W)


def _out_kernel(x_ref, att_ref, cv_ref, fo_ref, w_ref, gate_ref, fg_ref, o_ref, *, final):
    y = (_dot(att_ref[0], w_ref[0:ATT_W, :])
         + _dot(cv_ref[0], w_ref[ATT_W:ATT_W + CONV_W, :])
         + _dot(fo_ref[0], w_ref[ATT_W + CONV_W:, :]))
    xn = x_ref[0] + gate_ref[0] * y
    if final:
        ms = jnp.mean(xn * xn, axis=-1, keepdims=True)
        xn = xn * lax.rsqrt(ms + EPS) * fg_ref[...]
    o_ref[0] = xn


def _out_proj(x, att, cv, fo, w_out_bf, mod3, final_g, *, tm, mod_row, final):
    bn, s, _ = x.shape
    row = (lambda b, i: mod_row) if mod_row is not None else (lambda b, i: b)
    tile = lambda w: pl.BlockSpec((1, tm, w), lambda b, i: (b, i, 0))
    return pl.pallas_call(
        functools.partial(_out_kernel, final=final),
        grid=(bn, s // tm),
        in_specs=[
            tile(D_MODEL), tile(ATT_W), tile(CONV_W), tile(FOUR_W),
            pl.BlockSpec((D_MODEL, D_MODEL), lambda b, i: (0, 0)),
            pl.BlockSpec((1, 1, D_MODEL), lambda b, i: (row(b, i), 0, 2)),
            pl.BlockSpec((1, D_MODEL), lambda b, i: (0, 0)),
        ],
        out_specs=tile(D_MODEL),
        out_shape=jax.ShapeDtypeStruct((bn, s, D_MODEL), F32),
        name="out_proj",
    )(x, att, cv, fo, w_out_bf, mod3, final_g.reshape(1, D_MODEL))


def _rope_tables(s):
    rows = s // GRID_W
    row_pos = jnp.repeat(jnp.arange(rows, dtype=F32), GRID_W)
    col_pos = jnp.tile(jnp.arange(GRID_W, dtype=F32), rows)
    half = HEAD_DIM // 4
    freqs = ROPE_BASE ** (-jnp.arange(half, dtype=F32) / half)
    zero = jnp.zeros((s, half), F32)

    def per_head(fn_lo, fn_hi):
        parts = []
        for pos in (row_pos, col_pos):
            ang = pos[:, None] * freqs[None, :]
            parts += [fn_lo(ang), fn_hi(ang)]
        return jnp.tile(jnp.concatenate(parts, axis=-1), (1, LANES // HEAD_DIM))

    cos = per_head(jnp.cos, jnp.cos)
    sin_lo = per_head(lambda a: -jnp.sin(a), lambda a: zero)
    sin_hi = per_head(lambda a: zero, jnp.sin)
    scale = HEAD_DIM ** -0.5
    return jnp.concatenate([cos * scale, sin_lo * scale, sin_hi * scale, cos, sin_lo, sin_hi], axis=-1)


def _identity_rope_tables(s):
    one = jnp.ones((s, LANES), F32)
    zero = jnp.zeros((s, LANES), F32)
    return jnp.concatenate([one * HEAD_DIM ** -0.5, zero, zero, one, zero, zero], axis=-1)


def _channel_dft(n_pos):
    c = np.arange(FOUR_HEAD_DIM)
    ang = 2.0 * np.pi * ((c[:, None] * c[None, :]) % FOUR_HEAD_DIM) / FOUR_HEAD_DIM
    eye = np.eye(FOUR_HEADS)
    scale = 1.0 / np.sqrt(float(n_pos * FOUR_HEAD_DIM))
    tab = np.concatenate([np.kron(eye, np.cos(ang)), np.kron(eye, np.sin(ang))], axis=1) * scale
    return jnp.asarray(tab.astype(np.float32))


def _position_dft(n_pos):
    half = n_pos // 2
    k = np.arange(half)
    ang = 2.0 * np.pi * ((k[:, None] * k[None, :]) % half) / half
    beta = np.pi * k / half
    ones = np.ones((1, FOUR_W))
    f32 = lambda a: jnp.asarray(a.astype(np.float32))
    return (f32(np.cos(ang)), f32(np.sin(ang)),
            f32(np.cos(beta)[:, None] * ones), f32(np.sin(beta)[:, None] * ones))


def _mixer(h, mod3, mod_row, norm_g, w_in_bf, rope_tab, dft_c, pos_tabs, tm, attention,
           conv_w, conv_b, ln_g, ln_b, w_four_bf, b_four):
    q, kv, gatt, uc, gcv, ab, gf = _in_proj(h, mod3, norm_g, w_in_bf, rope_tab, dft_c, tm=tm, mod_row=mod_row)
    att = attention(q, kv, gatt)
    cv = _conv_branch(uc, gcv, conv_w, conv_b, ln_g, ln_b)
    c_half, s_half, cw, sw = pos_tabs
    fo = _fourier_branch(ab, gf, (c_half.astype(BF16), s_half.astype(BF16), cw, sw), w_four_bf, b_four)
    return kv, att, cv, fo


def kernel(x, c, ctx, c_ctx, w_ada, b_ada, norm_g, w_in, attn_sink, conv_w, conv_b,
           conv_ln_g, conv_ln_b, w_four, b_four, w_out, final_g):
    bn, s, _ = x.shape
    n_ctx = ctx.shape[1]
    assert bn < MOD_ROWS
    ctx_row = bn

    cc = jnp.zeros((MOD_ROWS, D_MODEL), F32).at[:bn].set(c).at[ctx_row].set(c_ctx)
    mod = _modulation(cc, w_ada, b_ada)

    w_in_bf = w_in.astype(BF16)
    w_out_bf = w_out.astype(BF16)
    w_four_bf = w_four.astype(BF16)
    rope_x = _rope_tables(s)
    rope_c = _identity_rope_tables(n_ctx)
    dft_x = _channel_dft(s).astype(BF16)
    dft_ctx = _channel_dft(n_ctx).astype(BF16)
    pos_x = _position_dft(s)
    pos_ctx = _position_dft(n_ctx)

    h_ctx = ctx
    for l in range(DEPTH):
        mod3 = mod[l].reshape(MOD_ROWS, 1, 3 * D_MODEL)
        branch = (conv_w[l], conv_b[l], conv_ln_g[l], conv_ln_b[l], w_four_bf[l], b_four[l])
        sink = attn_sink[l]
        if l < DEPTH - 1:
            kvc, att_c, cv_c, fo_c = _mixer(
                h_ctx, mod3, ctx_row, norm_g[l], w_in_bf[l], rope_c, dft_ctx, pos_ctx, n_ctx,
                lambda q, kv, g: _ctx_attention(sink, q, kv, g), *branch)
        else:
            kvc = _kv_proj(h_ctx, mod3, norm_g[l], w_in_bf[l][:, OFF_KV:OFF_KV + 2 * KV_W],
                           tm=n_ctx, mod_row=ctx_row)
        _, att, cv, fo = _mixer(
            x, mod3, None, norm_g[l], w_in_bf[l], rope_x, dft_x, pos_x, 512,
            lambda q, kv, g: _local_attention(sink, q, kv, kvc, g), *branch)
        if l < DEPTH - 1:
            h_ctx = _out_proj(h_ctx, att_c, cv_c, fo_c, w_out_bf[l], mod3, final_g,
                              tm=n_ctx, mod_row=ctx_row, final=False)
        x = _out_proj(x, att, cv, fo, w_out_bf[l], mod3, final_g,
                      tm=512, mod_row=None, final=(l == DEPTH - 1))
    return x
```

```python
import functools

import numpy as np
import jax
import jax.numpy as jnp
from jax import lax
from jax.experimental import pallas as pl
from jax.experimental.pallas import tpu as pltpu

F32 = jnp.float32
BF16 = jnp.bfloat16

D_MODEL = 1024
DEPTH = 2
GRID_W = 64
HEAD_DIM = 64
ATT_W = 512
N_HEADS = 8
N_KV_HEADS = 2
GQA_GROUP = N_HEADS // N_KV_HEADS
KV_W = N_KV_HEADS * HEAD_DIM
CONV_W = 256
FOUR_W = 256
FOUR_HEADS = 4
FOUR_HEAD_DIM = FOUR_W // FOUR_HEADS
CONV_K = 31
SUBLANES = 8
CONV_HALO = 16
WINDOW = 128
BLOCK = 128
ROPE_BASE = 10000.0
EPS = 1e-6
NEG_INF = -1e30
IN_W = 2 * ATT_W + 2 * KV_W + 3 * CONV_W + 2 * FOUR_W
OFF_Q = 0
OFF_KV = ATT_W
OFF_GATT = OFF_KV + 2 * KV_W
OFF_CONV = OFF_GATT + ATT_W
OFF_GCONV = OFF_CONV + 2 * CONV_W
OFF_FOUR = OFF_GCONV + CONV_W
OFF_GFOUR = OFF_FOUR + FOUR_W
LANES = 128
KVX_W = 8 * LANES
MOD_ROWS = 16


def _dot(a, b):
    return jnp.dot(a, b, preferred_element_type=F32)


def _dot_nt(a, b):
    return lax.dot_general(a, b, (((1,), (1,)), ((), ())), preferred_element_type=F32)


def _silu(x):
    return x * jax.nn.sigmoid(x)


def _ada_kernel(cc_ref, w_ref, b_ref, o_ref):
    a = _silu(cc_ref[...])
    w = w_ref[0]
    a_hi = a.astype(BF16)
    a_lo = (a - a_hi.astype(F32)).astype(BF16)
    w_hi = w.astype(BF16)
    w_lo = (w - w_hi.astype(F32)).astype(BF16)
    acc = _dot(a_hi, w_hi) + _dot(a_lo, w_hi) + _dot(a_hi, w_lo)
    o_ref[0] = acc + b_ref[0]


def _modulation(cc, w_ada, b_ada):
    tn = 768
    return pl.pallas_call(
        _ada_kernel,
        grid=(DEPTH, 3 * D_MODEL // tn),
        in_specs=[
            pl.BlockSpec((MOD_ROWS, D_MODEL), lambda l, j: (0, 0)),
            pl.BlockSpec((1, D_MODEL, tn), lambda l, j: (l, 0, j)),
            pl.BlockSpec((1, 1, tn), lambda l, j: (l, 0, j)),
        ],
        out_specs=pl.BlockSpec((1, MOD_ROWS, tn), lambda l, j: (l, 0, j)),
        out_shape=jax.ShapeDtypeStruct((DEPTH, MOD_ROWS, 3 * D_MODEL), F32),
        name="ada_modulation",
    )(cc, w_ada, b_ada.reshape(DEPTH, 1, 3 * D_MODEL))


def _norm_modulate(x_ref, sh_ref, sc_ref, g_ref):
    x = x_ref[0]
    ms = jnp.mean(x * x, axis=-1, keepdims=True)
    h = x * lax.rsqrt(ms + EPS) * g_ref[...]
    h = h * (1.0 + sc_ref[0]) + sh_ref[0]
    return h.astype(BF16)


def _rope_block(blk, rope_ref, base):
    cos = rope_ref[:, base:base + LANES]
    sin_lo = rope_ref[:, base + LANES:base + 2 * LANES]
    sin_hi = rope_ref[:, base + 2 * LANES:base + 3 * LANES]
    return (blk * cos + pltpu.roll(blk, LANES - 16, 1) * sin_lo + pltpu.roll(blk, 16, 1) * sin_hi)


def _store_padded_kv(kv_ref, kblk, vblk):
    lo = lax.broadcasted_iota(jnp.int32, kblk.shape, 1) < HEAD_DIM
    for t, blk in enumerate((kblk, vblk)):
        swapped = pltpu.roll(blk, HEAD_DIM, 1)
        blocks = (jnp.where(lo, blk, 0.0), jnp.where(lo, 0.0, swapped),
                  jnp.where(lo, swapped, 0.0), jnp.where(lo, 0.0, blk))
        for i, b in enumerate(blocks):
            c = (4 * t + i) * LANES
            kv_ref[0, :, c:c + LANES] = b.astype(BF16)


def _in_kernel(x_ref, sh_ref, sc_ref, g_ref, w_ref, rope_ref, dft_ref,
               q_ref, kv_ref, gatt_ref, uc_ref, gcv_ref, ab_ref, gf_ref):
    hb = _norm_modulate(x_ref, sh_ref, sc_ref, g_ref)

    def seg(a, b):
        return _dot(hb, w_ref[:, a:b])

    qp = seg(OFF_Q, OFF_Q + ATT_W)
    for c in range(ATT_W // LANES):
        blk = _rope_block(qp[:, c * LANES:(c + 1) * LANES], rope_ref, 0)
        q_ref[0, :, c * LANES:(c + 1) * LANES] = blk.astype(BF16)
    kvp = seg(OFF_KV, OFF_KV + 2 * KV_W)
    _store_padded_kv(kv_ref, _rope_block(kvp[:, 0:KV_W], rope_ref, 3 * LANES), kvp[:, KV_W:2 * KV_W])
    gatt_ref[0] = _silu(seg(OFF_GATT, OFF_GATT + ATT_W)).astype(BF16)
    cab = seg(OFF_CONV, OFF_CONV + 2 * CONV_W)
    uc_ref[0] = (cab[:, :CONV_W] * jax.nn.sigmoid(cab[:, CONV_W:])).astype(BF16)
    gcv_ref[0] = _silu(seg(OFF_GCONV, OFF_GCONV + CONV_W)).astype(BF16)
    fu = seg(OFF_FOUR, OFF_FOUR + FOUR_W).astype(BF16)
    ab_ref[0] = _dot(fu, dft_ref[...]).astype(BF16)
    gf_ref[0] = _silu(seg(OFF_GFOUR, OFF_GFOUR + FOUR_W)).astype(BF16)


def _mod_specs(mod_row):
    row = (lambda b, i: mod_row) if mod_row is not None else (lambda b, i: b)
    return [pl.BlockSpec((1, 1, D_MODEL), lambda b, i, j=j: (row(b, i), 0, j)) for j in (0, 1)]


def _in_proj(x, mod3, norm_g, w_bf, rope_tab, dft_c, *, tm, mod_row):
    bn, s, _ = x.shape
    widths = (ATT_W, KVX_W, ATT_W, CONV_W, CONV_W, 2 * FOUR_W, FOUR_W)
    return pl.pallas_call(
        _in_kernel,
        grid=(bn, s // tm),
        in_specs=[pl.BlockSpec((1, tm, D_MODEL), lambda b, i: (b, i, 0))]
        + _mod_specs(mod_row)
        + [
            pl.BlockSpec((1, D_MODEL), lambda b, i: (0, 0)),
            pl.BlockSpec((D_MODEL, IN_W), lambda b, i: (0, 0)),
            pl.BlockSpec((tm, 6 * LANES), lambda b, i: (i, 0)),
            pl.BlockSpec((FOUR_W, 2 * FOUR_W), lambda b, i: (0, 0)),
        ],
        out_specs=[pl.BlockSpec((1, tm, w), lambda b, i: (b, i, 0)) for w in widths],
        out_shape=[jax.ShapeDtypeStruct((bn, s, w), BF16) for w in widths],
        name="in_proj",
    )(x, mod3, mod3, norm_g.reshape(1, D_MODEL), w_bf, rope_tab, dft_c)


def _kv_kernel(x_ref, sh_ref, sc_ref, g_ref, w_ref, kv_ref):
    hb = _norm_modulate(x_ref, sh_ref, sc_ref, g_ref)
    kvp = _dot(hb, w_ref[...])
    _store_padded_kv(kv_ref, kvp[:, 0:KV_W], kvp[:, KV_W:2 * KV_W])


def _kv_proj(x, mod3, norm_g, w_kv_bf, *, tm, mod_row):
    bn, s, _ = x.shape
    return pl.pallas_call(
        _kv_kernel,
        grid=(bn, s // tm),
        in_specs=[pl.BlockSpec((1, tm, D_MODEL), lambda b, i: (b, i, 0))]
        + _mod_specs(mod_row)
        + [
            pl.BlockSpec((1, D_MODEL), lambda b, i: (0, 0)),
            pl.BlockSpec((D_MODEL, 2 * KV_W), lambda b, i: (0, 0)),
        ],
        out_specs=pl.BlockSpec((1, tm, KVX_W), lambda b, i: (b, i, 0)),
        out_shape=jax.ShapeDtypeStruct((bn, s, KVX_W), BF16),
        name="ctx_kv_proj",
    )(x, mod3, mod3, norm_g.reshape(1, D_MODEL), w_kv_bf)


def _pair_attention(q2, sink_cols, keysets):
    out = None
    for e, kset in enumerate(keysets):
        scores = []
        for k, _, bias in kset:
            sc = _dot_nt(q2, k)
            scores.append(sc if bias is None else sc + bias)
        m = sink_cols[e]
        for sc in scores:
            m = jnp.maximum(m, sc.max(axis=-1, keepdims=True))
        r = None
        for sc, (_, v, _) in zip(scores, kset):
            p = jnp.exp(sc - m).astype(BF16)
            pv = _dot(p, jnp.concatenate([v, jnp.ones_like(v)], axis=1))
            r = pv if r is None else r + pv
        o = r[:, :LANES] / (r[:, LANES:] + jnp.exp(sink_cols[e] - m))
        out = o if out is None else out + o
    return out


def _group_attention(sink_ref, q_ref, g_ref, o_ref, rows, tq, kvh, keysets):
    c0 = 2 * kvh * LANES
    q2 = jnp.concatenate([q_ref[0, rows, c0:c0 + LANES], q_ref[0, rows, c0 + LANES:c0 + 2 * LANES]], axis=0)
    sink_cols = [
        jnp.concatenate([jnp.full((tq, 1), sink_ref[GQA_GROUP * kvh + e], F32),
                         jnp.full((tq, 1), sink_ref[GQA_GROUP * kvh + 2 + e], F32)], axis=0)
        for e in range(2)]
    o = _pair_attention(q2, sink_cols, keysets)
    for hf in range(2):
        cols = slice(c0 + hf * LANES, c0 + (hf + 1) * LANES)
        g = g_ref[0, rows, cols].astype(F32)
        o_ref[0, rows, cols] = (o[hf * tq:(hf + 1) * tq] * g).astype(BF16)


def _kv_cols(kvh, e):
    k0 = (2 * kvh + e) * LANES
    return slice(k0, k0 + LANES), slice(4 * LANES + k0, 5 * LANES + k0)


def _local_attn_kernel(sink_ref, q_ref, kv_ref, kvc_ref, bias_ref, g_ref, o_ref, *, seq, sub):
    nblk = seq // BLOCK
    for sb in range(sub):
        n = pl.program_id(1) * sub + sb
        start = pl.multiple_of(jnp.clip((n - 1) * BLOCK, 0, seq - 3 * BLOCK), BLOCK)
        win = pl.ds(start, 3 * BLOCK)
        bias = bias_ref[jnp.where(n == 0, 0, jnp.where(n == nblk - 1, 2, 1))]
        bias2 = jnp.concatenate([bias, bias], axis=0)
        rows = slice(sb * BLOCK, (sb + 1) * BLOCK)
        for kvh in range(N_KV_HEADS):
            keysets = []
            for e in range(2):
                kc, vc = _kv_cols(kvh, e)
                keysets.append([(kv_ref[0, win, kc], kv_ref[0, win, vc], bias2),
                                (kvc_ref[0, :, kc], kvc_ref[0, :, vc], None)])
            _group_attention(sink_ref, q_ref, g_ref, o_ref, rows, BLOCK, kvh, keysets)


def _band_bias():
    i = np.arange(BLOCK)[:, None]
    k = np.arange(3 * BLOCK)[None, :]
    tabs = [np.where(np.abs(BLOCK * qb + i - k) <= WINDOW, 0.0, NEG_INF) for qb in range(3)]
    return jnp.asarray(np.stack(tabs).astype(np.float32))


def _local_attention(sink, q, kv, kvc, gatt):
    bn, s, _ = q.shape
    n_ctx = kvc.shape[1]
    sub = 2
    tq = sub * BLOCK
    assert s % tq == 0 and s >= 3 * BLOCK
    return pl.pallas_call(
        functools.partial(_local_attn_kernel, seq=s, sub=sub),
        grid=(bn, s // tq),
        in_specs=[
            pl.BlockSpec(memory_space=pltpu.SMEM),
            pl.BlockSpec((1, tq, ATT_W), lambda b, n: (b, n, 0)),
            pl.BlockSpec((1, s, KVX_W), lambda b, n: (b, 0, 0)),
            pl.BlockSpec((1, n_ctx, KVX_W), lambda b, n: (b, 0, 0)),
            pl.BlockSpec((3, BLOCK, 3 * BLOCK), lambda b, n: (0, 0, 0)),
            pl.BlockSpec((1, tq, ATT_W), lambda b, n: (b, n, 0)),
        ],
        out_specs=pl.BlockSpec((1, tq, ATT_W), lambda b, n: (b, n, 0)),
        out_shape=jax.ShapeDtypeStruct((bn, s, ATT_W), BF16),
        name="local_attention",
    )(sink, q, kv, kvc, _band_bias(), gatt)


def _ctx_attn_kernel(sink_ref, q_ref, kvc_ref, g_ref, o_ref, *, n_ctx):
    for kvh in range(N_KV_HEADS):
        keysets = []
        for e in range(2):
            kc, vc = _kv_cols(kvh, e)
            keysets.append([(kvc_ref[0, :, kc], kvc_ref[0, :, vc], None)])
        _group_attention(sink_ref, q_ref, g_ref, o_ref, slice(0, n_ctx), n_ctx, kvh, keysets)


def _ctx_attention(sink, q, kvc, gatt):
    bn, n_ctx, _ = q.shape
    return pl.pallas_call(
        functools.partial(_ctx_attn_kernel, n_ctx=n_ctx),
        grid=(bn,),
        in_specs=[
            pl.BlockSpec(memory_space=pltpu.SMEM),
            pl.BlockSpec((1, n_ctx, ATT_W), lambda b: (b, 0, 0)),
            pl.BlockSpec((1, n_ctx, KVX_W), lambda b: (b, 0, 0)),
            pl.BlockSpec((1, n_ctx, ATT_W), lambda b: (b, 0, 0)),
        ],
        out_specs=pl.BlockSpec((1, n_ctx, ATT_W), lambda b: (b, 0, 0)),
        out_shape=jax.ShapeDtypeStruct((bn, n_ctx, ATT_W), BF16),
        name="ctx_attention",
    )(sink, q, kvc, gatt)


def _conv_kernel(u_ref, g_ref, w_ref, b_ref, lg_ref, lb_ref, o_ref, pad_ref, *, seq, chunk):
    zeros = jnp.zeros((CONV_HALO, CONV_W), F32)
    pad_ref[0:CONV_HALO, :] = zeros
    pad_ref[CONV_HALO + seq:2 * CONV_HALO + seq, :] = zeros
    pad_ref[CONV_HALO:CONV_HALO + seq, :] = u_ref[0].astype(F32)
    lead = CONV_HALO - CONV_K // 2

    def body(i, carry):
        t0 = pl.multiple_of(i * chunk, chunk)
        tiles = []
        for lt in range(CONV_W // LANES):
            lanes = slice(lt * LANES, (lt + 1) * LANES)
            acc = None
            for r in range(SUBLANES):
                z = None
                for a in range((CONV_K - r + SUBLANES - 1) // SUBLANES):
                    j = SUBLANES * a + r
                    term = pad_ref[pl.ds(t0 + SUBLANES * a, chunk + SUBLANES), lanes] * w_ref[j:j + 1, lanes]
                    z = term if z is None else z + term
                zs = z[r + lead:r + lead + chunk]
                acc = zs if acc is None else acc + zs
            tiles.append(acc)
        y = jnp.concatenate(tiles, axis=1) + b_ref[...]
        mu = jnp.mean(y, axis=-1, keepdims=True)
        yc = y - mu
        var = jnp.mean(yc * yc, axis=-1, keepdims=True)
        yn = yc * lax.rsqrt(var + EPS) * lg_ref[...] + lb_ref[...]
        g = g_ref[0, pl.ds(t0, chunk), :].astype(F32)
        o_ref[0, pl.ds(t0, chunk), :] = (_silu(yn) * g).astype(BF16)
        return carry

    lax.fori_loop(0, seq // chunk, body, 0)


def _conv_branch(uc, gcv, conv_w, conv_b, ln_g, ln_b):
    bn, s, _ = uc.shape
    assert CONV_HALO - CONV_K // 2 + SUBLANES - 1 <= SUBLANES and CONV_K // 2 <= CONV_HALO
    row = lambda a: a.reshape(1, CONV_W)
    vec_spec = pl.BlockSpec((1, CONV_W), lambda b: (0, 0))
    seq_spec = pl.BlockSpec((1, s, CONV_W), lambda b: (b, 0, 0))
    return pl.pallas_call(
        functools.partial(_conv_kernel, seq=s, chunk=128),
        grid=(bn,),
        in_specs=[seq_spec, seq_spec, pl.BlockSpec((CONV_K, CONV_W), lambda b: (0, 0)),
                  vec_spec, vec_spec, vec_spec],
        out_specs=seq_spec,
        out_shape=jax.ShapeDtypeStruct((bn, s, CONV_W), BF16),
        scratch_shapes=[pltpu.VMEM((s + 2 * CONV_HALO, CONV_W), F32)],
        name="conv_branch",
    )(uc, gcv, conv_w, row(conv_b), row(ln_g), row(ln_b))


def _four_kernel(ab_ref, g_ref, c_ref, s_ref, cw_ref, sw_ref, wf_ref, bf_ref, o_ref,
                 rc_ref, rs_ref, il_ref, *, half, chunk):
    top = ab_ref[0, 0:half, :].astype(F32)
    bot = ab_ref[0, half:2 * half, :].astype(F32)
    plus = top + bot
    minus = top - bot
    a_m, b_m = minus[:, :FOUR_W], minus[:, FOUR_W:]
    cw, sw = cw_ref[...], sw_ref[...]
    rc_ref[:, 0:FOUR_W] = plus[:, :FOUR_W].astype(BF16)
    rc_ref[:, FOUR_W:] = (cw * a_m - sw * b_m).astype(BF16)
    rs_ref[:, 0:FOUR_W] = plus[:, FOUR_W:].astype(BF16)
    rs_ref[:, FOUR_W:] = (sw * a_m + cw * b_m).astype(BF16)

    def body(i, carry):
        r0 = pl.multiple_of(i * chunk, chunk)
        e = _dot(c_ref[pl.ds(r0, chunk), :], rc_ref[...]) - _dot(s_ref[pl.ds(r0, chunk), :], rs_ref[...])
        eb = e.astype(BF16)
        for p in range(2):
            f = _dot(eb[:, p * FOUR_W:(p + 1) * FOUR_W], wf_ref[...])
            for lt in range(FOUR_W // LANES):
                il_ref[lt, pl.ds(p, chunk, stride=2), :] = f[:, lt * LANES:(lt + 1) * LANES]
        out_rows = pl.ds(pl.multiple_of(2 * r0, 2 * chunk), 2 * chunk)
        g = g_ref[0, out_rows, :].astype(F32)
        f_all = jnp.concatenate([il_ref[lt] for lt in range(FOUR_W // LANES)], axis=1)
        o_ref[0, out_rows, :] = ((f_all + bf_ref[...]) * g).astype(BF16)
        return carry

    lax.fori_loop(0, half // chunk, body, 0)


def _fourier_branch(ab, gf, tabs, w_four_bf, b_four):
    bn, s, _ = ab.shape
    half = s // 2
    chunk = min(256, half)
    c_half, s_half, cw, sw = tabs
    const = lambda shape: pl.BlockSpec(shape, lambda b: (0,) * len(shape))
    return pl.pallas_call(
        functools.partial(_four_kernel, half=half, chunk=chunk),
        grid=(bn,),
        in_specs=[
            pl.BlockSpec((1, s, 2 * FOUR_W), lambda b: (b, 0, 0)),
            pl.BlockSpec((1, s, FOUR_W), lambda b: (b, 0, 0)),
            const((half, half)), const((half, half)),
            const((half, FOUR_W)), const((half, FOUR_W)),
            const((FOUR_W, FOUR_W)), const((1, FOUR_W)),
        ],
        out_specs=pl.BlockSpec((1, s, FOUR_W), lambda b: (b, 0, 0)),
        out_shape=jax.ShapeDtypeStruct((bn, s, FOUR_W), BF16),
        scratch_shapes=[pltpu.VMEM((half, 2 * FOUR_W), BF16), pltpu.VMEM((half, 2 * FOUR_W), BF16),
                        pltpu.VMEM((FOUR_W // LANES, 2 * chunk, LANES), F32)],
        name="fourier_branch",
    )(ab, gf, c_half, s_half, cw, sw, w_four_bf, b_four.reshape(1, FOUR_W))


def _out_kernel(x_ref, att_ref, cv_ref, fo_ref, w_ref, gate_ref, fg_ref, o_ref, *, final):
    y = (_dot(att_ref[0], w_ref[0:ATT_W, :])
         + _dot(cv_ref[0], w_ref[ATT_W:ATT_W + CONV_W, :])
         + _dot(fo_ref[0], w_ref[ATT_W + CONV_W:, :]))
    xn = x_ref[0] + gate_ref[0] * y
    if final:
        ms = jnp.mean(xn * xn, axis=-1, keepdims=True)
        xn = xn * lax.rsqrt(ms + EPS) * fg_ref[...]
    o_ref[0] = xn


def _out_proj(x, att, cv, fo, w_out_bf, mod3, final_g, *, tm, mod_row, final):
    bn, s, _ = x.shape
    row = (lambda b, i: mod_row) if mod_row is not None else (lambda b, i: b)
    tile = lambda w: pl.BlockSpec((1, tm, w), lambda b, i: (b, i, 0))
    return pl.pallas_call(
        functools.partial(_out_kernel, final=final),
        grid=(bn, s // tm),
        in_specs=[
            tile(D_MODEL), tile(ATT_W), tile(CONV_W), tile(FOUR_W),
            pl.BlockSpec((D_MODEL, D_MODEL), lambda b, i: (0, 0)),
            pl.BlockSpec((1, 1, D_MODEL), lambda b, i: (row(b, i), 0, 2)),
            pl.BlockSpec((1, D_MODEL), lambda b, i: (0, 0)),
        ],
        out_specs=tile(D_MODEL),
        out_shape=jax.ShapeDtypeStruct((bn, s, D_MODEL), F32),
        name="out_proj",
    )(x, att, cv, fo, w_out_bf, mod3, final_g.reshape(1, D_MODEL))


def _rope_tables(s):
    rows = s // GRID_W
    row_pos = jnp.repeat(jnp.arange(rows, dtype=F32), GRID_W)
    col_pos = jnp.tile(jnp.arange(GRID_W, dtype=F32), rows)
    half = HEAD_DIM // 4
    freqs = ROPE_BASE ** (-jnp.arange(half, dtype=F32) / half)
    zero = jnp.zeros((s, half), F32)

    def per_head(fn_lo, fn_hi):
        parts = []
        for pos in (row_pos, col_pos):
            ang = pos[:, None] * freqs[None, :]
            parts += [fn_lo(ang), fn_hi(ang)]
        return jnp.tile(jnp.concatenate(parts, axis=-1), (1, LANES // HEAD_DIM))

    cos = per_head(jnp.cos, jnp.cos)
    sin_lo = per_head(lambda a: -jnp.sin(a), lambda a: zero)
    sin_hi = per_head(lambda a: zero, jnp.sin)
    scale = HEAD_DIM ** -0.5
    return jnp.concatenate([cos * scale, sin_lo * scale, sin_hi * scale, cos, sin_lo, sin_hi], axis=-1)


def _identity_rope_tables(s):
    one = jnp.ones((s, LANES), F32)
    zero = jnp.zeros((s, LANES), F32)
    return jnp.concatenate([one * HEAD_DIM ** -0.5, zero, zero, one, zero, zero], axis=-1)


def _channel_dft(n_pos):
    c = np.arange(FOUR_HEAD_DIM)
    ang = 2.0 * np.pi * ((c[:, None] * c[None, :]) % FOUR_HEAD_DIM) / FOUR_HEAD_DIM
    eye = np.eye(FOUR_HEADS)
    scale = 1.0 / np.sqrt(float(n_pos * FOUR_HEAD_DIM))
    tab = np.concatenate([np.kron(eye, np.cos(ang)), np.kron(eye, np.sin(ang))], axis=1) * scale
    return jnp.asarray(tab.astype(np.float32))


def _position_dft(n_pos):
    half = n_pos // 2
    k = np.arange(half)
    ang = 2.0 * np.pi * ((k[:, None] * k[None, :]) % half) / half
    beta = np.pi * k / half
    ones = np.ones((1, FOUR_W))
    f32 = lambda a: jnp.asarray(a.astype(np.float32))
    return (f32(np.cos(ang)), f32(np.sin(ang)),
            f32(np.cos(beta)[:, None] * ones), f32(np.sin(beta)[:, None] * ones))


def _mixer(h, mod3, mod_row, norm_g, w_in_bf, rope_tab, dft_c, pos_tabs, tm, attention,
           conv_w, conv_b, ln_g, ln_b, w_four_bf, b_four):
    q, kv, gatt, uc, gcv, ab, gf = _in_proj(h, mod3, norm_g, w_in_bf, rope_tab, dft_c, tm=tm, mod_row=mod_row)
    att = attention(q, kv, gatt)
    cv = _conv_branch(uc, gcv, conv_w, conv_b, ln_g, ln_b)
    c_half, s_half, cw, sw = pos_tabs
    fo = _fourier_branch(ab, gf, (c_half.astype(BF16), s_half.astype(BF16), cw, sw), w_four_bf, b_four)
    return kv, att, cv, fo


def kernel(x, c, ctx, c_ctx, w_ada, b_ada, norm_g, w_in, attn_sink, conv_w, conv_b,
           conv_ln_g, conv_ln_b, w_four, b_four, w_out, final_g):
    bn, s, _ = x.shape
    n_ctx = ctx.shape[1]
    assert bn < MOD_ROWS
    ctx_row = bn

    cc = jnp.zeros((MOD_ROWS, D_MODEL), F32).at[:bn].set(c).at[ctx_row].set(c_ctx)
    mod = _modulation(cc, w_ada, b_ada)

    w_in_bf = w_in.astype(BF16)
    w_out_bf = w_out.astype(BF16)
    w_four_bf = w_four.astype(BF16)
    rope_x = _rope_tables(s)
    rope_c = _identity_rope_tables(n_ctx)
    dft_x = _channel_dft(s).astype(BF16)
    dft_ctx = _channel_dft(n_ctx).astype(BF16)
    pos_x = _position_dft(s)
    pos_ctx = _position_dft(n_ctx)

    h_ctx = ctx
    for l in range(DEPTH):
        mod3 = mod[l].reshape(MOD_ROWS, 1, 3 * D_MODEL)
        branch = (conv_w[l], conv_b[l], conv_ln_g[l], conv_ln_b[l], w_four_bf[l], b_four[l])
        sink = attn_sink[l]
        if l < DEPTH - 1:
            kvc, att_c, cv_c, fo_c = _mixer(
                h_ctx, mod3, ctx_row, norm_g[l], w_in_bf[l], rope_c, dft_ctx, pos_ctx, n_ctx,
                lambda q, kv, g: _ctx_attention(sink, q, kv, g), *branch)
        else:
            kvc = _kv_proj(h_ctx, mod3, norm_g[l], w_in_bf[l][:, OFF_KV:OFF_KV + 2 * KV_W],
                           tm=n_ctx, mod_row=ctx_row)
        _, att, cv, fo = _mixer(
            x, mod3, None, norm_g[l], w_in_bf[l], rope_x, dft_x, pos_x, 512,
            lambda q, kv, g: _local_attention(sink, q, kv, kvc, g), *branch)
        if l < DEPTH - 1:
            h_ctx = _out_proj(h_ctx, att_c, cv_c, fo_c, w_out_bf[l], mod3, final_g,
                              tm=n_ctx, mod_row=ctx_row, final=False)
        x = _out_proj(x, att, cv, fo, w_out_bf[l], mod3, final_g,
                      tm=512, mod_row=None, final=(l == DEPTH - 1))
    return x
```

```python
import functools

import numpy as np
import jax
import jax.numpy as jnp
from jax import lax
from jax.experimental import pallas as pl
from jax.experimental.pallas import tpu as pltpu

F32 = jnp.float32
BF16 = jnp.bfloat16

D_MODEL = 1024
DEPTH = 2
GRID_W = 64
HEAD_DIM = 64
ATT_W = 512
N_HEADS = 8
N_KV_HEADS = 2
GQA_GROUP = N_HEADS // N_KV_HEADS
KV_W = N_KV_HEADS * HEAD_DIM
CONV_W = 256
FOUR_W = 256
FOUR_HEADS = 4
FOUR_HEAD_DIM = FOUR_W // FOUR_HEADS
CONV_K = 31
SUBLANES = 8
CONV_HALO = 16
WINDOW = 128
BLOCK = 128
ROPE_BASE = 10000.0
EPS = 1e-6
NEG_INF = -1e30
IN_W = 2 * ATT_W + 2 * KV_W + 3 * CONV_W + 2 * FOUR_W
OFF_Q = 0
OFF_KV = ATT_W
OFF_GATT = OFF_KV + 2 * KV_W
OFF_CONV = OFF_GATT + ATT_W
OFF_GCONV = OFF_CONV + 2 * CONV_W
OFF_FOUR = OFF_GCONV + CONV_W
OFF_GFOUR = OFF_FOUR + FOUR_W
LANES = 128
KVX_W = 8 * LANES
MOD_ROWS = 16
SEQUENTIAL = pltpu.CompilerParams(dimension_semantics=("arbitrary", "arbitrary"))


def _dot(a, b):
    return jnp.dot(a, b, preferred_element_type=F32)


def _dot_nt(a, b):
    return lax.dot_general(a, b, (((1,), (1,)), ((), ())), preferred_element_type=F32)


def _silu(x):
    return x * jax.nn.sigmoid(x)


def _ada_kernel(cc_ref, w_ref, b_ref, o_ref):
    a = _silu(cc_ref[...])
    w = w_ref[0]
    a_hi = a.astype(BF16)
    a_lo = (a - a_hi.astype(F32)).astype(BF16)
    w_hi = w.astype(BF16)
    w_lo = (w - w_hi.astype(F32)).astype(BF16)
    acc = _dot(a_hi, w_hi) + _dot(a_lo, w_hi) + _dot(a_hi, w_lo)
    o_ref[0] = acc + b_ref[0]


def _modulation(cc, w_ada, b_ada):
    tn = 768
    return pl.pallas_call(
        _ada_kernel,
        grid=(DEPTH, 3 * D_MODEL // tn),
        in_specs=[
            pl.BlockSpec((MOD_ROWS, D_MODEL), lambda l, j: (0, 0)),
            pl.BlockSpec((1, D_MODEL, tn), lambda l, j: (l, 0, j)),
            pl.BlockSpec((1, 1, tn), lambda l, j: (l, 0, j)),
        ],
        out_specs=pl.BlockSpec((1, MOD_ROWS, tn), lambda l, j: (l, 0, j)),
        out_shape=jax.ShapeDtypeStruct((DEPTH, MOD_ROWS, 3 * D_MODEL), F32),
        name="ada_modulation",
    )(cc, w_ada, b_ada.reshape(DEPTH, 1, 3 * D_MODEL))


def _norm_modulate(x_ref, sh_ref, sc_ref, g_ref):
    x = x_ref[0]
    ms = jnp.mean(x * x, axis=-1, keepdims=True)
    gain = g_ref[...] * (1.0 + sc_ref[0])
    h = x * lax.rsqrt(ms + EPS) * gain + sh_ref[0]
    return h.astype(BF16)


def _rope_block(blk, rope_ref, base):
    cos = rope_ref[:, base:base + LANES]
    sin_lo = rope_ref[:, base + LANES:base + 2 * LANES]
    sin_hi = rope_ref[:, base + 2 * LANES:base + 3 * LANES]
    return (blk * cos + pltpu.roll(blk, LANES - 16, 1) * sin_lo + pltpu.roll(blk, 16, 1) * sin_hi)


def _store_padded_kv(kv_ref, kblk, vblk):
    lo = lax.broadcasted_iota(jnp.int32, kblk.shape, 1) < HEAD_DIM
    for t, blk in enumerate((kblk, vblk)):
        swapped = pltpu.roll(blk, HEAD_DIM, 1)
        blocks = (jnp.where(lo, blk, 0.0), jnp.where(lo, 0.0, swapped),
                  jnp.where(lo, swapped, 0.0), jnp.where(lo, 0.0, blk))
        for i, b in enumerate(blocks):
            c = (4 * t + i) * LANES
            kv_ref[0, :, c:c + LANES] = b.astype(BF16)


def _in_kernel(x_ref, sh_ref, sc_ref, g_ref, w_ref, rope_ref, dft_ref,
               q_ref, kv_ref, gatt_ref, uc_ref, gcv_ref, ab_ref, gf_ref):
    hb = _norm_modulate(x_ref, sh_ref, sc_ref, g_ref)
    p = _dot(hb, w_ref[...])

    def seg(a, b):
        return p[:, a:b]

    qp = seg(OFF_Q, OFF_Q + ATT_W)
    for c in range(ATT_W // LANES):
        blk = _rope_block(qp[:, c * LANES:(c + 1) * LANES], rope_ref, 0)
        q_ref[0, :, c * LANES:(c + 1) * LANES] = blk.astype(BF16)
    kvp = seg(OFF_KV, OFF_KV + 2 * KV_W)
    _store_padded_kv(kv_ref, _rope_block(kvp[:, 0:KV_W], rope_ref, 3 * LANES), kvp[:, KV_W:2 * KV_W])
    gatt_ref[0] = _silu(seg(OFF_GATT, OFF_GATT + ATT_W)).astype(BF16)
    cab = seg(OFF_CONV, OFF_CONV + 2 * CONV_W)
    uc_ref[0] = (cab[:, :CONV_W] * jax.nn.sigmoid(cab[:, CONV_W:])).astype(BF16)
    gcv_ref[0] = _silu(seg(OFF_GCONV, OFF_GCONV + CONV_W)).astype(BF16)
    fu = seg(OFF_FOUR, OFF_FOUR + FOUR_W).astype(BF16)
    ab_ref[0] = _dot(fu, dft_ref[...]).astype(BF16)
    gf_ref[0] = _silu(seg(OFF_GFOUR, OFF_GFOUR + FOUR_W)).astype(BF16)


def _mod_spec(mod_row, j):
    row = (lambda b: mod_row) if mod_row is not None else (lambda b: b)
    return pl.BlockSpec((1, 1, D_MODEL), lambda b, i: (row(b), 0, j))


def _in_proj(x, mod3, norm_g, w_bf, rope_tab, dft_c, *, tm, mod_row):
    bn, s, _ = x.shape
    widths = (ATT_W, KVX_W, ATT_W, CONV_W, CONV_W, 2 * FOUR_W, FOUR_W)
    return pl.pallas_call(
        _in_kernel,
        grid=(bn, s // tm),
        in_specs=[
            pl.BlockSpec((1, tm, D_MODEL), lambda b, i: (b, i, 0)),
            _mod_spec(mod_row, 0), _mod_spec(mod_row, 1),
            pl.BlockSpec((1, D_MODEL), lambda b, i: (0, 0)),
            pl.BlockSpec((D_MODEL, IN_W), lambda b, i: (0, 0)),
            pl.BlockSpec((tm, 6 * LANES), lambda b, i: (i, 0)),
            pl.BlockSpec((FOUR_W, 2 * FOUR_W), lambda b, i: (0, 0)),
        ],
        out_specs=[pl.BlockSpec((1, tm, w), lambda b, i: (b, i, 0)) for w in widths],
        out_shape=[jax.ShapeDtypeStruct((bn, s, w), BF16) for w in widths],
        name="in_proj",
    )(x, mod3, mod3, norm_g.reshape(1, D_MODEL), w_bf, rope_tab, dft_c)


def _kv_kernel(x_ref, sh_ref, sc_ref, g_ref, w_ref, kv_ref):
    hb = _norm_modulate(x_ref, sh_ref, sc_ref, g_ref)
    kvp = _dot(hb, w_ref[...])
    _store_padded_kv(kv_ref, kvp[:, 0:KV_W], kvp[:, KV_W:2 * KV_W])


def _kv_proj(x, mod3, norm_g, w_kv_bf, *, tm, mod_row):
    bn, s, _ = x.shape
    return pl.pallas_call(
        _kv_kernel,
        grid=(bn, s // tm),
        in_specs=[
            pl.BlockSpec((1, tm, D_MODEL), lambda b, i: (b, i, 0)),
            _mod_spec(mod_row, 0), _mod_spec(mod_row, 1),
            pl.BlockSpec((1, D_MODEL), lambda b, i: (0, 0)),
            pl.BlockSpec((D_MODEL, 2 * KV_W), lambda b, i: (0, 0)),
        ],
        out_specs=pl.BlockSpec((1, tm, KVX_W), lambda b, i: (b, i, 0)),
        out_shape=jax.ShapeDtypeStruct((bn, s, KVX_W), BF16),
        name="ctx_kv_proj",
    )(x, mod3, mod3, norm_g.reshape(1, D_MODEL), w_kv_bf)


def _fill_conv_pad(pad_ref, u_ref, seq):
    zeros = jnp.zeros((CONV_HALO, CONV_W), F32)
    pad_ref[0:CONV_HALO, :] = zeros
    pad_ref[CONV_HALO + seq:2 * CONV_HALO + seq, :] = zeros
    pad_ref[CONV_HALO:CONV_HALO + seq, :] = u_ref[0].astype(F32)


def _conv_chunk(pad_ref, w_ref, b_ref, lg_ref, lb_ref, t0, chunk):
    lead = CONV_HALO - CONV_K // 2
    tiles = []
    for lt in range(CONV_W // LANES):
        lanes = slice(lt * LANES, (lt + 1) * LANES)
        acc = None
        for r in range(SUBLANES):
            z = None
            for a in range((CONV_K - r + SUBLANES - 1) // SUBLANES):
                j = SUBLANES * a + r
                term = pad_ref[pl.ds(t0 + SUBLANES * a, chunk + SUBLANES), lanes] * w_ref[j:j + 1, lanes]
                z = term if z is None else z + term
            zs = z[r + lead:r + lead + chunk]
            acc = zs if acc is None else acc + zs
        tiles.append(acc)
    y = jnp.concatenate(tiles, axis=1) + b_ref[...]
    mu = jnp.mean(y, axis=-1, keepdims=True)
    yc = y - mu
    var = jnp.mean(yc * yc, axis=-1, keepdims=True)
    return _silu(yc * lax.rsqrt(var + EPS) * lg_ref[...] + lb_ref[...])


def _conv_rows(pad_ref, conv_refs, gcv_ref, cv_ref, t0, n_rows):
    for j in range(n_rows // BLOCK):
        y = _conv_chunk(pad_ref, *conv_refs, t0 + j * BLOCK, BLOCK)
        rows = slice(j * BLOCK, (j + 1) * BLOCK)
        cv_ref[0, rows, :] = (y * gcv_ref[0, rows, :].astype(F32)).astype(BF16)


def _conv_specs(index):
    vec = pl.BlockSpec((1, CONV_W), index)
    return [pl.BlockSpec((CONV_K, CONV_W), index), vec, vec, vec]


def _conv_args(conv_w, conv_b, ln_g, ln_b):
    row = lambda a: a.reshape(1, CONV_W)
    return conv_w, row(conv_b), row(ln_g), row(ln_b)


def _pair_attention(q2, sink_cols, keysets):
    out = None
    for e, kset in enumerate(keysets):
        scores = []
        for k, _, bias in kset:
            sc = _dot_nt(q2, k)
            scores.append(sc if bias is None else sc + bias)
        m = sink_cols[e]
        for sc in scores:
            m = jnp.maximum(m, sc.max(axis=-1, keepdims=True))
        r = None
        for sc, (_, v, _) in zip(scores, kset):
            p = jnp.exp(sc - m).astype(BF16)
            pv = _dot(p, jnp.concatenate([v, jnp.ones_like(v)], axis=1))
            r = pv if r is None else r + pv
        o = r[:, :LANES] / (r[:, LANES:] + jnp.exp(sink_cols[e] - m))
        out = o if out is None else out + o
    return out


def _group_attention(sink_ref, q_ref, g_ref, o_ref, rows, tq, kvh, keysets):
    c0 = 2 * kvh * LANES
    q2 = jnp.concatenate([q_ref[0, rows, c0:c0 + LANES], q_ref[0, rows, c0 + LANES:c0 + 2 * LANES]], axis=0)
    sink_cols = [
        jnp.concatenate([jnp.full((tq, 1), sink_ref[GQA_GROUP * kvh + e], F32),
                         jnp.full((tq, 1), sink_ref[GQA_GROUP * kvh + 2 + e], F32)], axis=0)
        for e in range(2)]
    o = _pair_attention(q2, sink_cols, keysets)
    for hf in range(2):
        cols = slice(c0 + hf * LANES, c0 + (hf + 1) * LANES)
        g = g_ref[0, rows, cols].astype(F32)
        o_ref[0, rows, cols] = (o[hf * tq:(hf + 1) * tq] * g).astype(BF16)


def _kv_cols(kvh, e):
    k0 = (2 * kvh + e) * LANES
    return slice(k0, k0 + LANES), slice(4 * LANES + k0, 5 * LANES + k0)


def _local_attn_kernel(sink_ref, q_ref, kv_ref, kvc_ref, bias_ref, g_ref, uc_ref, gcv_ref,
                       cw_ref, cb_ref, lg_ref, lb_ref, o_ref, cv_ref, pad_ref, *, seq, sub):
    @pl.when(pl.program_id(1) == 0)
    def _():
        _fill_conv_pad(pad_ref, uc_ref, seq)

    nblk = seq // BLOCK
    for sb in range(sub):
        n = pl.program_id(1) * sub + sb
        start = pl.multiple_of(jnp.clip((n - 1) * BLOCK, 0, seq - 3 * BLOCK), BLOCK)
        win = pl.ds(start, 3 * BLOCK)
        bias = bias_ref[jnp.where(n == 0, 0, jnp.where(n == nblk - 1, 2, 1))]
        bias2 = jnp.concatenate([bias, bias], axis=0)
        rows = slice(sb * BLOCK, (sb + 1) * BLOCK)
        for kvh in range(N_KV_HEADS):
            keysets = []
            for e in range(2):
                kc, vc = _kv_cols(kvh, e)
                keysets.append([(kv_ref[0, win, kc], kv_ref[0, win, vc], bias2),
                                (kvc_ref[0, :, kc], kvc_ref[0, :, vc], None)])
            _group_attention(sink_ref, q_ref, g_ref, o_ref, rows, BLOCK, kvh, keysets)
            if kvh == 0:
                y = _conv_chunk(pad_ref, cw_ref, cb_ref, lg_ref, lb_ref, pl.multiple_of(n * BLOCK, BLOCK), BLOCK)
                cv_ref[0, rows, :] = (y * gcv_ref[0, rows, :].astype(F32)).astype(BF16)


def _band_bias():
    i = np.arange(BLOCK)[:, None]
    k = np.arange(3 * BLOCK)[None, :]
    tabs = [np.where(np.abs(BLOCK * qb + i - k) <= WINDOW, 0.0, NEG_INF) for qb in range(3)]
    return jnp.asarray(np.stack(tabs).astype(np.float32))


def _local_attention(sink, q, kv, kvc, gatt, uc, gcv, conv_args):
    bn, s, _ = q.shape
    n_ctx = kvc.shape[1]
    sub = 2
    tq = sub * BLOCK
    assert s % tq == 0 and s >= 3 * BLOCK
    tile = lambda w: pl.BlockSpec((1, tq, w), lambda b, n: (b, n, 0))
    whole = lambda rows, w: pl.BlockSpec((1, rows, w), lambda b, n: (b, 0, 0))
    return pl.pallas_call(
        functools.partial(_local_attn_kernel, seq=s, sub=sub),
        grid=(bn, s // tq),
        in_specs=[
            pl.BlockSpec(memory_space=pltpu.SMEM),
            tile(ATT_W), whole(s, KVX_W), whole(n_ctx, KVX_W),
            pl.BlockSpec((3, BLOCK, 3 * BLOCK), lambda b, n: (0, 0, 0)),
            tile(ATT_W), whole(s, CONV_W), tile(CONV_W),
        ] + _conv_specs(lambda b, n: (0, 0)),
        out_specs=[tile(ATT_W), tile(CONV_W)],
        out_shape=[jax.ShapeDtypeStruct((bn, s, ATT_W), BF16), jax.ShapeDtypeStruct((bn, s, CONV_W), BF16)],
        scratch_shapes=[pltpu.VMEM((s + 2 * CONV_HALO, CONV_W), F32)],
        compiler_params=SEQUENTIAL,
        name="local_attention",
    )(sink, q, kv, kvc, _band_bias(), gatt, uc, gcv, *conv_args)


def _ctx_attn_kernel(sink_ref, q_ref, kvc_ref, g_ref, uc_ref, gcv_ref,
                     cw_ref, cb_ref, lg_ref, lb_ref, o_ref, cv_ref, pad_ref, *, n_ctx):
    _fill_conv_pad(pad_ref, uc_ref, n_ctx)
    for kvh in range(N_KV_HEADS):
        keysets = []
        for e in range(2):
            kc, vc = _kv_cols(kvh, e)
            keysets.append([(kvc_ref[0, :, kc], kvc_ref[0, :, vc], None)])
        _group_attention(sink_ref, q_ref, g_ref, o_ref, slice(0, n_ctx), n_ctx, kvh, keysets)
    _conv_rows(pad_ref, (cw_ref, cb_ref, lg_ref, lb_ref), gcv_ref, cv_ref, 0, n_ctx)


def _ctx_attention(sink, q, kvc, gatt, uc, gcv, conv_args):
    bn, n_ctx, _ = q.shape
    whole = lambda w: pl.BlockSpec((1, n_ctx, w), lambda b: (b, 0, 0))
    return pl.pallas_call(
        functools.partial(_ctx_attn_kernel, n_ctx=n_ctx),
        grid=(bn,),
        in_specs=[pl.BlockSpec(memory_space=pltpu.SMEM), whole(ATT_W), whole(KVX_W), whole(ATT_W),
                  whole(CONV_W), whole(CONV_W)] + _conv_specs(lambda b: (0, 0)),
        out_specs=[whole(ATT_W), whole(CONV_W)],
        out_shape=[jax.ShapeDtypeStruct((bn, n_ctx, ATT_W), BF16),
                   jax.ShapeDtypeStruct((bn, n_ctx, CONV_W), BF16)],
        scratch_shapes=[pltpu.VMEM((n_ctx + 2 * CONV_HALO, CONV_W), F32)],
        name="ctx_attention",
    )(sink, q, kvc, gatt, uc, gcv, *conv_args)


def _fourier_prepare(ab_ref, cw_ref, sw_ref, rc_ref, rs_ref, half):
    top = ab_ref[0, 0:half, :].astype(F32)
    bot = ab_ref[0, half:2 * half, :].astype(F32)
    plus = top + bot
    minus = top - bot
    a_m, b_m = minus[:, :FOUR_W], minus[:, FOUR_W:]
    cw, sw = cw_ref[...], sw_ref[...]
    rc_ref[:, 0:FOUR_W] = plus[:, :FOUR_W].astype(BF16)
    rc_ref[:, FOUR_W:] = (cw * a_m - sw * b_m).astype(BF16)
    rs_ref[:, 0:FOUR_W] = plus[:, FOUR_W:].astype(BF16)
    rs_ref[:, FOUR_W:] = (sw * a_m + cw * b_m).astype(BF16)


def _fourier_rows(c_ref, s_ref, rc_ref, rs_ref, wf_ref, bf_ref, il_ref, r0, chunk):
    e = _dot(c_ref[pl.ds(r0, chunk), :], rc_ref[...]) - _dot(s_ref[pl.ds(r0, chunk), :], rs_ref[...])
    eb = e.astype(BF16)
    for p in range(2):
        f = _dot(eb[:, p * FOUR_W:(p + 1) * FOUR_W], wf_ref[...])
        for lt in range(FOUR_W // LANES):
            il_ref[lt, pl.ds(p, chunk, stride=2), :] = f[:, lt * LANES:(lt + 1) * LANES]
    return jnp.concatenate([il_ref[lt] for lt in range(FOUR_W // LANES)], axis=1) + bf_ref[...]


def _out_kernel(x_ref, att_ref, cv_ref, ab_ref, gf_ref, c_ref, s_ref, cw_ref, sw_ref, wf_ref, bf_ref,
                w_ref, gate_ref, fg_ref, o_ref, rc_ref, rs_ref, il_ref, *, tm, half, final):
    @pl.when(pl.program_id(1) == 0)
    def _():
        _fourier_prepare(ab_ref, cw_ref, sw_ref, rc_ref, rs_ref, half)

    r0 = pl.multiple_of(pl.program_id(1) * (tm // 2), tm // 2)
    four = _fourier_rows(c_ref, s_ref, rc_ref, rs_ref, wf_ref, bf_ref, il_ref, r0, tm // 2)
    fo = (four * gf_ref[0].astype(F32)).astype(BF16)
    y = _dot(jnp.concatenate([att_ref[0], cv_ref[0], fo], axis=1), w_ref[...])
    xn = x_ref[0] + gate_ref[0] * y
    if final:
        ms = jnp.mean(xn * xn, axis=-1, keepdims=True)
        xn = xn * lax.rsqrt(ms + EPS) * fg_ref[...]
    o_ref[0] = xn


def _out_proj(x, att, cv, ab, gf, pos_tabs, w_four_bf, b_four, w_out_bf, mod3, final_g, *, tm, mod_row, final):
    bn, s, _ = x.shape
    half = s // 2
    c_half, s_half, cw, sw = pos_tabs
    tile = lambda w: pl.BlockSpec((1, tm, w), lambda b, i: (b, i, 0))
    const = lambda shape: pl.BlockSpec(shape, lambda b, i: (0,) * len(shape))
    return pl.pallas_call(
        functools.partial(_out_kernel, tm=tm, half=half, final=final),
        grid=(bn, s // tm),
        in_specs=[
            tile(D_MODEL), tile(ATT_W), tile(CONV_W),
            pl.BlockSpec((1, s, 2 * FOUR_W), lambda b, i: (b, 0, 0)),
            tile(FOUR_W),
            const((half, half)), const((half, half)), const((half, FOUR_W)), const((half, FOUR_W)),
            const((FOUR_W, FOUR_W)), const((1, FOUR_W)),
            const((D_MODEL, D_MODEL)),
            _mod_spec(mod_row, 2),
            const((1, D_MODEL)),
        ],
        out_specs=tile(D_MODEL),
        out_shape=jax.ShapeDtypeStruct((bn, s, D_MODEL), F32),
        scratch_shapes=[pltpu.VMEM((half, 2 * FOUR_W), BF16), pltpu.VMEM((half, 2 * FOUR_W), BF16),
                        pltpu.VMEM((FOUR_W // LANES, tm, LANES), F32)],
        compiler_params=SEQUENTIAL,
        name="out_proj",
    )(x, att, cv, ab, gf, c_half, s_half, cw, sw, w_four_bf, b_four.reshape(1, FOUR_W),
      w_out_bf, mod3, final_g.reshape(1, D_MODEL))


def _rope_tables(s):
    rows = s // GRID_W
    row_pos = jnp.repeat(jnp.arange(rows, dtype=F32), GRID_W)
    col_pos = jnp.tile(jnp.arange(GRID_W, dtype=F32), rows)
    half = HEAD_DIM // 4
    freqs = ROPE_BASE ** (-jnp.arange(half, dtype=F32) / half)
    zero = jnp.zeros((s, half), F32)

    def per_head(fn_lo, fn_hi):
        parts = []
        for pos in (row_pos, col_pos):
            ang = pos[:, None] * freqs[None, :]
            parts += [fn_lo(ang), fn_hi(ang)]
        return jnp.tile(jnp.concatenate(parts, axis=-1), (1, LANES // HEAD_DIM))

    cos = per_head(jnp.cos, jnp.cos)
    sin_lo = per_head(lambda a: -jnp.sin(a), lambda a: zero)
    sin_hi = per_head(lambda a: zero, jnp.sin)
    scale = HEAD_DIM ** -0.5
    return jnp.concatenate([cos * scale, sin_lo * scale, sin_hi * scale, cos, sin_lo, sin_hi], axis=-1)


def _identity_rope_tables(s):
    one = jnp.ones((s, LANES), F32)
    zero = jnp.zeros((s, LANES), F32)
    return jnp.concatenate([one * HEAD_DIM ** -0.5, zero, zero, one, zero, zero], axis=-1)


def _channel_dft(n_pos):
    c = np.arange(FOUR_HEAD_DIM)
    ang = 2.0 * np.pi * ((c[:, None] * c[None, :]) % FOUR_HEAD_DIM) / FOUR_HEAD_DIM
    eye = np.eye(FOUR_HEADS)
    scale = 1.0 / np.sqrt(float(n_pos * FOUR_HEAD_DIM))
    tab = np.concatenate([np.kron(eye, np.cos(ang)), np.kron(eye, np.sin(ang))], axis=1) * scale
    return jnp.asarray(tab.astype(np.float32))


def _position_dft(n_pos):
    half = n_pos // 2
    k = np.arange(half)
    ang = 2.0 * np.pi * ((k[:, None] * k[None, :]) % half) / half
    beta = np.pi * k / half
    ones = np.ones((1, FOUR_W))
    f32 = lambda a: jnp.asarray(a.astype(np.float32))
    return (f32(np.cos(ang)).astype(BF16), f32(np.sin(ang)).astype(BF16),
            f32(np.cos(beta)[:, None] * ones), f32(np.sin(beta)[:, None] * ones))


def kernel(x, c, ctx, c_ctx, w_ada, b_ada, norm_g, w_in, attn_sink, conv_w, conv_b,
           conv_ln_g, conv_ln_b, w_four, b_four, w_out, final_g):
    bn, s, _ = x.shape
    n_ctx = ctx.shape[1]
    assert bn < MOD_ROWS
    ctx_row = bn

    cc = jnp.zeros((MOD_ROWS, D_MODEL), F32).at[:bn].set(c).at[ctx_row].set(c_ctx)
    mod = _modulation(cc, w_ada, b_ada)

    w_in_bf = w_in.astype(BF16)
    w_out_bf = w_out.astype(BF16)
    w_four_bf = w_four.astype(BF16)
    rope_x = _rope_tables(s)
    rope_c = _identity_rope_tables(n_ctx)
    dft_x = _channel_dft(s).astype(BF16)
    dft_ctx = _channel_dft(n_ctx).astype(BF16)
    pos_x = _position_dft(s)
    pos_ctx = _position_dft(n_ctx)

    h_ctx = ctx
    for l in range(DEPTH):
        mod3 = mod[l].reshape(MOD_ROWS, 1, 3 * D_MODEL)
        conv_args = _conv_args(conv_w[l], conv_b[l], conv_ln_g[l], conv_ln_b[l])
        sink = attn_sink[l]
        last = l == DEPTH - 1
        if not last:
            qc, kvc, gatt_c, uc_c, gcv_c, ab_c, gf_c = _in_proj(
                h_ctx, mod3, norm_g[l], w_in_bf[l], rope_c, dft_ctx, tm=n_ctx, mod_row=ctx_row)
            att_c, cv_c = _ctx_attention(sink, qc, kvc, gatt_c, uc_c, gcv_c, conv_args)
        else:
            kvc = _kv_proj(h_ctx, mod3, norm_g[l], w_in_bf[l][:, OFF_KV:OFF_KV + 2 * KV_W],
                           tm=n_ctx, mod_row=ctx_row)
        q, kv, gatt, uc, gcv, ab, gf = _in_proj(
            x, mod3, norm_g[l], w_in_bf[l], rope_x, dft_x, tm=512, mod_row=None)
        att, cv = _local_attention(sink, q, kv, kvc, gatt, uc, gcv, conv_args)
        if not last:
            h_ctx = _out_proj(h_ctx, att_c, cv_c, ab_c, gf_c, pos_ctx, w_four_bf[l], b_four[l], w_out_bf[l],
                              mod3, final_g, tm=n_ctx, mod_row=ctx_row, final=False)
        x = _out_proj(x, att, cv, ab, gf, pos_x, w_four_bf[l], b_four[l], w_out_bf[l],
                      mod3, final_g, tm=512, mod_row=None, final=last)
    return x
```

```python
import functools

import numpy as np
import jax
import jax.numpy as jnp
from jax import lax
from jax.experimental import pallas as pl
from jax.experimental.pallas import tpu as pltpu

F32 = jnp.float32
BF16 = jnp.bfloat16

D_MODEL = 1024
DEPTH = 2
GRID_W = 64
HEAD_DIM = 64
ATT_W = 512
N_HEADS = 8
N_KV_HEADS = 2
GQA_GROUP = N_HEADS // N_KV_HEADS
KV_W = N_KV_HEADS * HEAD_DIM
CONV_W = 256
FOUR_W = 256
FOUR_HEADS = 4
FOUR_HEAD_DIM = FOUR_W // FOUR_HEADS
CONV_K = 31
SUBLANES = 8
CONV_HALO = 16
WINDOW = 128
BLOCK = 128
ROPE_BASE = 10000.0
EPS = 1e-6
NEG_INF = -1e30
LOG2E = 1.4426950408889634
Q_SCALE = HEAD_DIM ** -0.5 * LOG2E
IN_W = 2 * ATT_W + 2 * KV_W + 3 * CONV_W + 2 * FOUR_W
OFF_Q = 0
OFF_KV = ATT_W
OFF_GATT = OFF_KV + 2 * KV_W
OFF_CONV = OFF_GATT + ATT_W
OFF_GCONV = OFF_CONV + 2 * CONV_W
OFF_FOUR = OFF_GCONV + CONV_W
OFF_GFOUR = OFF_FOUR + FOUR_W
LANES = 128
KVX_W = 4 * LANES
MOD_ROWS = 16
SEQUENTIAL = pltpu.CompilerParams(dimension_semantics=("arbitrary", "arbitrary"))


def _dot(a, b):
    return jnp.dot(a, b, preferred_element_type=F32)


def _dot_nt(a, b):
    return lax.dot_general(a, b, (((1,), (1,)), ((), ())), preferred_element_type=F32)


def _silu(x):
    return x * jax.nn.sigmoid(x)


def _ada_kernel(cc_ref, w_ref, b_ref, o_ref):
    a = _silu(cc_ref[...])
    w = w_ref[0]
    a_hi = a.astype(BF16)
    a_lo = (a - a_hi.astype(F32)).astype(BF16)
    w_hi = w.astype(BF16)
    w_lo = (w - w_hi.astype(F32)).astype(BF16)
    acc = _dot(a_hi, w_hi) + _dot(a_lo, w_hi) + _dot(a_hi, w_lo)
    o_ref[0] = acc + b_ref[0]


def _modulation(cc, w_ada, b_ada):
    tn = 768
    return pl.pallas_call(
        _ada_kernel,
        grid=(DEPTH, 3 * D_MODEL // tn),
        in_specs=[
            pl.BlockSpec((MOD_ROWS, D_MODEL), lambda l, j: (0, 0)),
            pl.BlockSpec((1, D_MODEL, tn), lambda l, j: (l, 0, j)),
            pl.BlockSpec((1, 1, tn), lambda l, j: (l, 0, j)),
        ],
        out_specs=pl.BlockSpec((1, MOD_ROWS, tn), lambda l, j: (l, 0, j)),
        out_shape=jax.ShapeDtypeStruct((DEPTH, MOD_ROWS, 3 * D_MODEL), F32),
        name="ada_modulation",
    )(cc, w_ada, b_ada.reshape(DEPTH, 1, 3 * D_MODEL))


def _norm_modulate(x_ref, sh_ref, sc_ref, g_ref):
    x = x_ref[0]
    ms = jnp.mean(x * x, axis=-1, keepdims=True)
    gain = g_ref[...] * (1.0 + sc_ref[0])
    h = x * lax.rsqrt(ms + EPS) * gain + sh_ref[0]
    return h.astype(BF16)


def _rope_block(blk, rope_ref, base):
    cos = rope_ref[:, base:base + LANES]
    sin_lo = rope_ref[:, base + LANES:base + 2 * LANES]
    sin_hi = rope_ref[:, base + 2 * LANES:base + 3 * LANES]
    return (blk * cos + pltpu.roll(blk, LANES - 16, 1) * sin_lo + pltpu.roll(blk, 16, 1) * sin_hi)


def _store_padded_kv(kt_ref, v_ref, kblk, vblk):
    tm = kblk.shape[0]
    kt = kblk.T.astype(BF16)
    zeros = jnp.zeros((HEAD_DIM, tm), BF16)
    k_blocks = ((kt[:HEAD_DIM], zeros), (zeros, kt[:HEAD_DIM]), (kt[HEAD_DIM:], zeros), (zeros, kt[HEAD_DIM:]))
    for i, halves in enumerate(k_blocks):
        kt_ref[0, i * LANES:(i + 1) * LANES, :] = jnp.concatenate(halves, axis=0)
    lo = lax.broadcasted_iota(jnp.int32, vblk.shape, 1) < HEAD_DIM
    swapped = pltpu.roll(vblk, HEAD_DIM, 1)
    v_blocks = (jnp.where(lo, vblk, 0.0), jnp.where(lo, 0.0, swapped),
                jnp.where(lo, swapped, 0.0), jnp.where(lo, 0.0, vblk))
    for i, b in enumerate(v_blocks):
        v_ref[0, :, i * LANES:(i + 1) * LANES] = b.astype(BF16)


def _in_kernel(x_ref, sh_ref, sc_ref, g_ref, w_ref, rope_ref, dft_ref,
               q_ref, kt_ref, v_ref, gatt_ref, uc_ref, gcv_ref, ab_ref, gf_ref):
    hb = _norm_modulate(x_ref, sh_ref, sc_ref, g_ref)
    p = _dot(hb, w_ref[...])

    def seg(a, b):
        return p[:, a:b]

    qp = seg(OFF_Q, OFF_Q + ATT_W)
    for c in range(ATT_W // LANES):
        blk = _rope_block(qp[:, c * LANES:(c + 1) * LANES], rope_ref, 0)
        q_ref[0, :, c * LANES:(c + 1) * LANES] = blk.astype(BF16)
    kvp = seg(OFF_KV, OFF_KV + 2 * KV_W)
    _store_padded_kv(kt_ref, v_ref, _rope_block(kvp[:, 0:KV_W], rope_ref, 3 * LANES), kvp[:, KV_W:2 * KV_W])
    gatt_ref[0] = _silu(seg(OFF_GATT, OFF_GATT + ATT_W)).astype(BF16)
    cab = seg(OFF_CONV, OFF_CONV + 2 * CONV_W)
    uc_ref[0] = (cab[:, :CONV_W] * jax.nn.sigmoid(cab[:, CONV_W:])).astype(BF16)
    gcv_ref[0] = _silu(seg(OFF_GCONV, OFF_GCONV + CONV_W)).astype(BF16)
    fu = seg(OFF_FOUR, OFF_FOUR + FOUR_W).astype(BF16)
    ab_ref[0] = _dot(fu, dft_ref[...]).astype(BF16)
    gf_ref[0] = _silu(seg(OFF_GFOUR, OFF_GFOUR + FOUR_W)).astype(BF16)


def _mod_spec(mod_row, j):
    row = (lambda b: mod_row) if mod_row is not None else (lambda b: b)
    return pl.BlockSpec((1, 1, D_MODEL), lambda b, i: (row(b), 0, j))


def _in_proj(x, mod3, norm_g, w_bf, rope_tab, dft_c, *, tm, mod_row):
    bn, s, _ = x.shape
    widths = (ATT_W, None, KVX_W, ATT_W, CONV_W, CONV_W, 2 * FOUR_W, FOUR_W)
    out_specs = [pl.BlockSpec((1, tm, w), lambda b, i: (b, i, 0)) if w else
                 pl.BlockSpec((1, KVX_W, tm), lambda b, i: (b, 0, i)) for w in widths]
    out_shape = [jax.ShapeDtypeStruct((bn, s, w) if w else (bn, KVX_W, s), BF16) for w in widths]
    return pl.pallas_call(
        _in_kernel,
        grid=(bn, s // tm),
        in_specs=[
            pl.BlockSpec((1, tm, D_MODEL), lambda b, i: (b, i, 0)),
            _mod_spec(mod_row, 0), _mod_spec(mod_row, 1),
            pl.BlockSpec((1, D_MODEL), lambda b, i: (0, 0)),
            pl.BlockSpec((D_MODEL, IN_W), lambda b, i: (0, 0)),
            pl.BlockSpec((tm, 6 * LANES), lambda b, i: (i, 0)),
            pl.BlockSpec((FOUR_W, 2 * FOUR_W), lambda b, i: (0, 0)),
        ],
        out_specs=out_specs,
        out_shape=out_shape,
        name="in_proj",
    )(x, mod3, mod3, norm_g.reshape(1, D_MODEL), w_bf, rope_tab, dft_c)


def _kv_kernel(x_ref, sh_ref, sc_ref, g_ref, w_ref, kt_ref, v_ref):
    hb = _norm_modulate(x_ref, sh_ref, sc_ref, g_ref)
    kvp = _dot(hb, w_ref[...])
    _store_padded_kv(kt_ref, v_ref, kvp[:, 0:KV_W], kvp[:, KV_W:2 * KV_W])


def _kv_proj(x, mod3, norm_g, w_kv_bf, *, tm, mod_row):
    bn, s, _ = x.shape
    return pl.pallas_call(
        _kv_kernel,
        grid=(bn, s // tm),
        in_specs=[
            pl.BlockSpec((1, tm, D_MODEL), lambda b, i: (b, i, 0)),
            _mod_spec(mod_row, 0), _mod_spec(mod_row, 1),
            pl.BlockSpec((1, D_MODEL), lambda b, i: (0, 0)),
            pl.BlockSpec((D_MODEL, 2 * KV_W), lambda b, i: (0, 0)),
        ],
        out_specs=[pl.BlockSpec((1, KVX_W, tm), lambda b, i: (b, 0, i)),
                   pl.BlockSpec((1, tm, KVX_W), lambda b, i: (b, i, 0))],
        out_shape=[jax.ShapeDtypeStruct((bn, KVX_W, s), BF16), jax.ShapeDtypeStruct((bn, s, KVX_W), BF16)],
        name="ctx_kv_proj",
    )(x, mod3, mod3, norm_g.reshape(1, D_MODEL), w_kv_bf)


def _fill_conv_pad(pad_ref, u_ref, seq):
    zeros = jnp.zeros((CONV_HALO, CONV_W), F32)
    pad_ref[0:CONV_HALO, :] = zeros
    pad_ref[CONV_HALO + seq:2 * CONV_HALO + seq, :] = zeros
    pad_ref[CONV_HALO:CONV_HALO + seq, :] = u_ref[0].astype(F32)


def _zero_after(token):
    if token is None:
        return jnp.zeros((1, LANES), F32)
    bits = pltpu.bitcast(jnp.broadcast_to(token[:SUBLANES], (SUBLANES, LANES)), jnp.uint32)
    zero = lax.shift_right_logical(lax.shift_right_logical(bits, jnp.uint32(16)), jnp.uint32(16))
    return pltpu.bitcast(zero, F32)[:1]


def _conv_tile(pad_ref, w_ref, t0, lt, zero):
    lead = CONV_HALO - CONV_K // 2
    lanes = slice(lt * LANES, (lt + 1) * LANES)
    acc = None
    for r in range(SUBLANES):
        z = None
        for a in range((CONV_K - r + SUBLANES - 1) // SUBLANES):
            j = SUBLANES * a + r
            term = pad_ref[pl.ds(t0 + SUBLANES * a, BLOCK + SUBLANES), lanes] * (w_ref[j:j + 1, lanes] + zero)
            z = term if z is None else z + term
        zs = z[r + lead:r + lead + BLOCK]
        acc = zs if acc is None else acc + zs
    return acc


def _conv_finish(tiles, b_ref, lg_ref, lb_ref):
    y = jnp.concatenate(tiles, axis=1) + b_ref[...]
    mu = jnp.mean(y, axis=-1, keepdims=True)
    yc = y - mu
    var = jnp.mean(yc * yc, axis=-1, keepdims=True)
    return _silu(yc * lax.rsqrt(var + EPS) * lg_ref[...] + lb_ref[...])


def _conv_pieces(pad_ref, conv_refs, gcv_ref, cv_ref, t0, n_rows):
    w_ref, b_ref, lg_ref, lb_ref = conv_refs
    pieces = []
    for j in range(n_rows // BLOCK):
        tiles = []
        rows = slice(j * BLOCK, (j + 1) * BLOCK)
        for lt in range(CONV_W // LANES):
            pieces.append(lambda token, j=j, lt=lt, tiles=tiles: tiles.append(
                _conv_tile(pad_ref, w_ref, t0 + j * BLOCK, lt, _zero_after(token))))

        def finish(token, tiles=tiles, rows=rows):
            y = _conv_finish(tiles, b_ref, lg_ref, lb_ref)
            cv_ref[0, rows, :] = (y * gcv_ref[0, rows, :].astype(F32)).astype(BF16)
        pieces.append(finish)
    return pieces


def _conv_specs(index):
    vec = pl.BlockSpec((1, CONV_W), index)
    return [pl.BlockSpec((CONV_K, CONV_W), index), vec, vec, vec]


def _conv_args(conv_w, conv_b, ln_g, ln_b):
    row = lambda a: a.reshape(1, CONV_W)
    return conv_w, row(conv_b), row(ln_g), row(ln_b)


class _Chain:
    def __init__(self, q2, sink_col, kset):
        self.q2, self.sink_col, self.kset = q2, sink_col, kset

    def scores(self):
        self.s = []
        for kt, _, bias in self.kset:
            sc = _dot(self.q2, kt)
            self.s.append(sc if bias is None else sc + bias)

    def softmax(self):
        m = self.sink_col
        for sc in self.s:
            m = jnp.maximum(m, sc.max(axis=-1, keepdims=True))
        self.p = [jnp.exp2(sc - m).astype(BF16) for sc in self.s]
        self.extra = jnp.exp2(self.sink_col - m)
        self.row_max = m
        self.s = None

    def values(self):
        r = None
        for p, (_, v, _) in zip(self.p, self.kset):
            pv = _dot(p, jnp.concatenate([v, jnp.ones_like(v)], axis=1))
            r = pv if r is None else r + pv
        self.p = None
        return r[:, :LANES] / (r[:, LANES:] + self.extra)


def _run_chains(chains, finish, fillers, ahead):
    n = len(chains)
    fillers = list(fillers)
    for i in range(-ahead, n):
        if 0 <= i + ahead < n:
            chains[i + ahead].scores()
        if 0 <= i < n:
            finish(i, chains[i].values())
        if 0 <= i + 1 < n:
            chains[i + 1].softmax()
            if fillers:
                fillers.pop(0)(chains[i + 1].row_max)
    for f in fillers:
        f(None)


def _kv_block(kvh, e):
    k0 = (2 * kvh + e) * LANES
    return slice(k0, k0 + LANES)


def _group_chains(sink_ref, q_ref, rows, tq, kvh, ksets):
    c0 = 2 * kvh * LANES
    q2 = jnp.concatenate([q_ref[0, rows, c0:c0 + LANES], q_ref[0, rows, c0 + LANES:c0 + 2 * LANES]], axis=0)
    chains = []
    for e in range(2):
        sink_col = jnp.concatenate([jnp.full((tq, 1), sink_ref[GQA_GROUP * kvh + e] * LOG2E, F32),
                                    jnp.full((tq, 1), sink_ref[GQA_GROUP * kvh + 2 + e] * LOG2E, F32)], axis=0)
        chains.append(_Chain(q2, sink_col, ksets[e]))
    return chains


def _make_finish(groups, g_ref, o_ref):
    partial = {}

    def finish(i, out):
        if i % 2 == 0:
            partial[i // 2] = out
            return
        rows, tq, kvh = groups[i // 2]
        o = partial.pop(i // 2) + out
        c0 = 2 * kvh * LANES
        for hf in range(2):
            cols = slice(c0 + hf * LANES, c0 + (hf + 1) * LANES)
            g = g_ref[0, rows, cols].astype(F32)
            o_ref[0, rows, cols] = (o[hf * tq:(hf + 1) * tq] * g).astype(BF16)
    return finish


def _local_attn_kernel(sink_ref, q_ref, kt_ref, v_ref, ktc_ref, vc_ref, bias_ref, g_ref, uc_ref, gcv_ref,
                       cw_ref, cb_ref, lg_ref, lb_ref, o_ref, cv_ref, pad_ref, *, seq, sub):
    @pl.when(pl.program_id(1) == 0)
    def _():
        _fill_conv_pad(pad_ref, uc_ref, seq)

    nblk = seq // BLOCK
    chains, groups = [], []
    for sb in range(sub):
        n = pl.program_id(1) * sub + sb
        start = pl.multiple_of(jnp.clip((n - 1) * BLOCK, 0, seq - 3 * BLOCK), BLOCK)
        win = pl.ds(start, 3 * BLOCK)
        bias = bias_ref[jnp.where(n == 0, 0, jnp.where(n == nblk - 1, 2, 1))]
        bias2 = jnp.concatenate([bias, bias], axis=0)
        rows = slice(sb * BLOCK, (sb + 1) * BLOCK)
        for kvh in range(N_KV_HEADS):
            ksets = []
            for e in range(2):
                blk = _kv_block(kvh, e)
                ksets.append([(kt_ref[0, blk, win], v_ref[0, win, blk], bias2),
                              (ktc_ref[0, blk, :], vc_ref[0, :, blk], None)])
            chains += _group_chains(sink_ref, q_ref, rows, BLOCK, kvh, ksets)
            groups.append((rows, BLOCK, kvh))
    t0 = pl.multiple_of(pl.program_id(1) * (sub * BLOCK), sub * BLOCK)
    fillers = _conv_pieces(pad_ref, (cw_ref, cb_ref, lg_ref, lb_ref), gcv_ref, cv_ref, t0, sub * BLOCK)
    _run_chains(chains, _make_finish(groups, g_ref, o_ref), fillers, ahead=4)


def _band_bias():
    i = np.arange(BLOCK)[:, None]
    k = np.arange(3 * BLOCK)[None, :]
    tabs = [np.where(np.abs(BLOCK * qb + i - k) <= WINDOW, 0.0, NEG_INF) for qb in range(3)]
    return jnp.asarray(np.stack(tabs).astype(np.float32))


def _local_attention(sink, q, kv, kvc, gatt, uc, gcv, conv_args):
    bn, s, _ = q.shape
    n_ctx = kvc[1].shape[1]
    sub = 4
    tq = sub * BLOCK
    assert s % tq == 0 and s >= 3 * BLOCK
    tile = lambda w: pl.BlockSpec((1, tq, w), lambda b, n: (b, n, 0))
    whole = lambda rows, w: pl.BlockSpec((1, rows, w), lambda b, n: (b, 0, 0))
    kv_specs = lambda rows: [whole(KVX_W, rows), whole(rows, KVX_W)]
    return pl.pallas_call(
        functools.partial(_local_attn_kernel, seq=s, sub=sub),
        grid=(bn, s // tq),
        in_specs=[
            pl.BlockSpec(memory_space=pltpu.SMEM),
            tile(ATT_W), *kv_specs(s), *kv_specs(n_ctx),
            pl.BlockSpec((3, BLOCK, 3 * BLOCK), lambda b, n: (0, 0, 0)),
            tile(ATT_W), whole(s, CONV_W), tile(CONV_W),
        ] + _conv_specs(lambda b, n: (0, 0)),
        out_specs=[tile(ATT_W), tile(CONV_W)],
        out_shape=[jax.ShapeDtypeStruct((bn, s, ATT_W), BF16), jax.ShapeDtypeStruct((bn, s, CONV_W), BF16)],
        scratch_shapes=[pltpu.VMEM((s + 2 * CONV_HALO, CONV_W), F32)],
        compiler_params=SEQUENTIAL,
        name="local_attention",
    )(sink, q, *kv, *kvc, _band_bias(), gatt, uc, gcv, *conv_args)


def _ctx_attn_kernel(sink_ref, q_ref, ktc_ref, vc_ref, g_ref, uc_ref, gcv_ref,
                     cw_ref, cb_ref, lg_ref, lb_ref, o_ref, cv_ref, pad_ref, *, n_ctx):
    _fill_conv_pad(pad_ref, uc_ref, n_ctx)
    rows = slice(0, n_ctx)
    chains, groups = [], []
    for kvh in range(N_KV_HEADS):
        ksets = []
        for e in range(2):
            blk = _kv_block(kvh, e)
            ksets.append([(ktc_ref[0, blk, :], vc_ref[0, :, blk], None)])
        chains += _group_chains(sink_ref, q_ref, rows, n_ctx, kvh, ksets)
        groups.append((rows, n_ctx, kvh))
    fillers = _conv_pieces(pad_ref, (cw_ref, cb_ref, lg_ref, lb_ref), gcv_ref, cv_ref, 0, n_ctx)
    _run_chains(chains, _make_finish(groups, g_ref, o_ref), fillers, ahead=2)


def _ctx_attention(sink, q, kvc, gatt, uc, gcv, conv_args):
    bn, n_ctx, _ = q.shape
    whole = lambda w: pl.BlockSpec((1, n_ctx, w), lambda b: (b, 0, 0))
    return pl.pallas_call(
        functools.partial(_ctx_attn_kernel, n_ctx=n_ctx),
        grid=(bn,),
        in_specs=[pl.BlockSpec(memory_space=pltpu.SMEM), whole(ATT_W),
                  pl.BlockSpec((1, KVX_W, n_ctx), lambda b: (b, 0, 0)), whole(KVX_W), whole(ATT_W),
                  whole(CONV_W), whole(CONV_W)] + _conv_specs(lambda b: (0, 0)),
        out_specs=[whole(ATT_W), whole(CONV_W)],
        out_shape=[jax.ShapeDtypeStruct((bn, n_ctx, ATT_W), BF16),
                   jax.ShapeDtypeStruct((bn, n_ctx, CONV_W), BF16)],
        scratch_shapes=[pltpu.VMEM((n_ctx + 2 * CONV_HALO, CONV_W), F32)],
        name="ctx_attention",
    )(sink, q, *kvc, gatt, uc, gcv, *conv_args)


def _fourier_prepare(ab_ref, cw_ref, sw_ref, rc_ref, rs_ref, half):
    top = ab_ref[0, 0:half, :].astype(F32)
    bot = ab_ref[0, half:2 * half, :].astype(F32)
    plus = top + bot
    minus = top - bot
    a_m, b_m = minus[:, :FOUR_W], minus[:, FOUR_W:]
    cw, sw = cw_ref[...], sw_ref[...]
    rc_ref[:, 0:FOUR_W] = plus[:, :FOUR_W].astype(BF16)
    rc_ref[:, FOUR_W:] = (cw * a_m - sw * b_m).astype(BF16)
    rs_ref[:, 0:FOUR_W] = plus[:, FOUR_W:].astype(BF16)
    rs_ref[:, FOUR_W:] = (sw * a_m + cw * b_m).astype(BF16)


def _fourier_rows(c_ref, s_ref, rc_ref, rs_ref, wf_ref, bf_ref, il_ref, r0, chunk):
    e = _dot(c_ref[pl.ds(r0, chunk), :], rc_ref[...]) - _dot(s_ref[pl.ds(r0, chunk), :], rs_ref[...])
    eb = e.astype(BF16)
    for p in range(2):
        f = _dot(eb[:, p * FOUR_W:(p + 1) * FOUR_W], wf_ref[...])
        for lt in range(FOUR_W // LANES):
            il_ref[lt, pl.ds(p, chunk, stride=2), :] = f[:, lt * LANES:(lt + 1) * LANES]
    return jnp.concatenate([il_ref[lt] for lt in range(FOUR_W // LANES)], axis=1) + bf_ref[...]


def _out_kernel(x_ref, att_ref, cv_ref, ab_ref, gf_ref, c_ref, s_ref, cw_ref, sw_ref, wf_ref, bf_ref,
                w_ref, gate_ref, fg_ref, o_ref, rc_ref, rs_ref, il_ref, *, tm, half, final):
    @pl.when(pl.program_id(1) == 0)
    def _():
        _fourier_prepare(ab_ref, cw_ref, sw_ref, rc_ref, rs_ref, half)

    r0 = pl.multiple_of(pl.program_id(1) * (tm // 2), tm // 2)
    four = _fourier_rows(c_ref, s_ref, rc_ref, rs_ref, wf_ref, bf_ref, il_ref, r0, tm // 2)
    fo = (four * gf_ref[0].astype(F32)).astype(BF16)
    y = _dot(jnp.concatenate([att_ref[0], cv_ref[0], fo], axis=1), w_ref[...])
    xn = x_ref[0] + gate_ref[0] * y
    if final:
        ms = jnp.mean(xn * xn, axis=-1, keepdims=True)
        xn = xn * lax.rsqrt(ms + EPS) * fg_ref[...]
    o_ref[0] = xn


def _out_proj(x, att, cv, ab, gf, pos_tabs, w_four_bf, b_four, w_out_bf, mod3, final_g, *, tm, mod_row, final):
    bn, s, _ = x.shape
    half = s // 2
    c_half, s_half, cw, sw = pos_tabs
    tile = lambda w: pl.BlockSpec((1, tm, w), lambda b, i: (b, i, 0))
    const = lambda shape: pl.BlockSpec(shape, lambda b, i: (0,) * len(shape))
    return pl.pallas_call(
        functools.partial(_out_kernel, tm=tm, half=half, final=final),
        grid=(bn, s // tm),
        in_specs=[
            tile(D_MODEL), tile(ATT_W), tile(CONV_W),
            pl.BlockSpec((1, s, 2 * FOUR_W), lambda b, i: (b, 0, 0)),
            tile(FOUR_W),
            const((half, half)), const((half, half)), const((half, FOUR_W)), const((half, FOUR_W)),
            const((FOUR_W, FOUR_W)), const((1, FOUR_W)),
            const((D_MODEL, D_MODEL)),
            _mod_spec(mod_row, 2),
            const((1, D_MODEL)),
        ],
        out_specs=tile(D_MODEL),
        out_shape=jax.ShapeDtypeStruct((bn, s, D_MODEL), F32),
        scratch_shapes=[pltpu.VMEM((half, 2 * FOUR_W), BF16), pltpu.VMEM((half, 2 * FOUR_W), BF16),
                        pltpu.VMEM((FOUR_W // LANES, tm, LANES), F32)],
        compiler_params=SEQUENTIAL,
        name="out_proj",
    )(x, att, cv, ab, gf, c_half, s_half, cw, sw, w_four_bf, b_four.reshape(1, FOUR_W),
      w_out_bf, mod3, final_g.reshape(1, D_MODEL))


def _rope_tables(s):
    rows = s // GRID_W
    row_pos = jnp.repeat(jnp.arange(rows, dtype=F32), GRID_W)
    col_pos = jnp.tile(jnp.arange(GRID_W, dtype=F32), rows)
    half = HEAD_DIM // 4
    freqs = ROPE_BASE ** (-jnp.arange(half, dtype=F32) / half)
    zero = jnp.zeros((s, half), F32)

    def per_head(fn_lo, fn_hi):
        parts = []
        for pos in (row_pos, col_pos):
            ang = pos[:, None] * freqs[None, :]
            parts += [fn_lo(ang), fn_hi(ang)]
        return jnp.tile(jnp.concatenate(parts, axis=-1), (1, LANES // HEAD_DIM))

    cos = per_head(jnp.cos, jnp.cos)
    sin_lo = per_head(lambda a: -jnp.sin(a), lambda a: zero)
    sin_hi = per_head(lambda a: zero, jnp.sin)
    scale = Q_SCALE
    return jnp.concatenate([cos * scale, sin_lo * scale, sin_hi * scale, cos, sin_lo, sin_hi], axis=-1)


def _identity_rope_tables(s):
    one = jnp.ones((s, LANES), F32)
    zero = jnp.zeros((s, LANES), F32)
    return jnp.concatenate([one * Q_SCALE, zero, zero, one, zero, zero], axis=-1)


def _channel_dft(n_pos):
    c = np.arange(FOUR_HEAD_DIM)
    ang = 2.0 * np.pi * ((c[:, None] * c[None, :]) % FOUR_HEAD_DIM) / FOUR_HEAD_DIM
    eye = np.eye(FOUR_HEADS)
    scale = 1.0 / np.sqrt(float(n_pos * FOUR_HEAD_DIM))
    tab = np.concatenate([np.kron(eye, np.cos(ang)), np.kron(eye, np.sin(ang))], axis=1) * scale
    return jnp.asarray(tab.astype(np.float32))


def _position_dft(n_pos):
    half = n_pos // 2
    k = np.arange(half)
    ang = 2.0 * np.pi * ((k[:, None] * k[None, :]) % half) / half
    beta = np.pi * k / half
    ones = np.ones((1, FOUR_W))
    f32 = lambda a: jnp.asarray(a.astype(np.float32))
    return (f32(np.cos(ang)).astype(BF16), f32(np.sin(ang)).astype(BF16),
            f32(np.cos(beta)[:, None] * ones), f32(np.sin(beta)[:, None] * ones))


def kernel(x, c, ctx, c_ctx, w_ada, b_ada, norm_g, w_in, attn_sink, conv_w, conv_b,
           conv_ln_g, conv_ln_b, w_four, b_four, w_out, final_g):
    bn, s, _ = x.shape
    n_ctx = ctx.shape[1]
    assert bn < MOD_ROWS
    ctx_row = bn

    cc = jnp.zeros((MOD_ROWS, D_MODEL), F32).at[:bn].set(c).at[ctx_row].set(c_ctx)
    mod = _modulation(cc, w_ada, b_ada)

    w_in_bf = w_in.astype(BF16)
    w_out_bf = w_out.astype(BF16)
    w_four_bf = w_four.astype(BF16)
    rope_x = _rope_tables(s)
    rope_c = _identity_rope_tables(n_ctx)
    dft_x = _channel_dft(s).astype(BF16)
    dft_ctx = _channel_dft(n_ctx).astype(BF16)
    pos_x = _position_dft(s)
    pos_ctx = _position_dft(n_ctx)

    h_ctx = ctx
    for l in range(DEPTH):
        mod3 = mod[l].reshape(MOD_ROWS, 1, 3 * D_MODEL)
        conv_args = _conv_args(conv_w[l], conv_b[l], conv_ln_g[l], conv_ln_b[l])
        sink = attn_sink[l]
        last = l == DEPTH - 1
        if not last:
            qc, ktc, vc, gatt_c, uc_c, gcv_c, ab_c, gf_c = _in_proj(
                h_ctx, mod3, norm_g[l], w_in_bf[l], rope_c, dft_ctx, tm=n_ctx, mod_row=ctx_row)
            kvc = (ktc, vc)
            att_c, cv_c = _ctx_attention(sink, qc, kvc, gatt_c, uc_c, gcv_c, conv_args)
        else:
            kvc = _kv_proj(h_ctx, mod3, norm_g[l], w_in_bf[l][:, OFF_KV:OFF_KV + 2 * KV_W],
                           tm=n_ctx, mod_row=ctx_row)
        q, kt, v, gatt, uc, gcv, ab, gf = _in_proj(
            x, mod3, norm_g[l], w_in_bf[l], rope_x, dft_x, tm=512, mod_row=None)
        att, cv = _local_attention(sink, q, (kt, v), kvc, gatt, uc, gcv, conv_args)
        if not last:
            h_ctx = _out_proj(h_ctx, att_c, cv_c, ab_c, gf_c, pos_ctx, w_four_bf[l], b_four[l], w_out_bf[l],
                              mod3, final_g, tm=n_ctx, mod_row=ctx_row, final=False)
        x = _out_proj(x, att, cv, ab, gf, pos_x, w_four_bf[l], b_four[l], w_out_bf[l],
                      mod3, final_g, tm=512, mod_row=None, final=last)
    return x
```

```python
import functools

import numpy as np
import jax
import jax.numpy as jnp
from jax import lax
from jax.experimental import pallas as pl
from jax.experimental.pallas import tpu as pltpu

F32 = jnp.float32
BF16 = jnp.bfloat16

D_MODEL = 1024
DEPTH = 2
GRID_W = 64
HEAD_DIM = 64
ATT_W = 512
N_HEADS = 8
N_KV_HEADS = 2
GQA_GROUP = N_HEADS // N_KV_HEADS
KV_W = N_KV_HEADS * HEAD_DIM
CONV_W = 256
FOUR_W = 256
FOUR_HEADS = 4
FOUR_HEAD_DIM = FOUR_W // FOUR_HEADS
CONV_K = 31
SUBLANES = 8
CONV_HALO = 16
WINDOW = 128
BLOCK = 128
ROPE_BASE = 10000.0
EPS = 1e-6
NEG_INF = -1e30
LOG2E = 1.4426950408889634
Q_SCALE = HEAD_DIM ** -0.5 * LOG2E
IN_W = 2 * ATT_W + 2 * KV_W + 3 * CONV_W + 2 * FOUR_W
OFF_Q = 0
OFF_KV = ATT_W
OFF_GATT = OFF_KV + 2 * KV_W
OFF_CONV = OFF_GATT + ATT_W
OFF_GCONV = OFF_CONV + 2 * CONV_W
OFF_FOUR = OFF_GCONV + CONV_W
OFF_GFOUR = OFF_FOUR + FOUR_W
LANES = 128
KVX_W = 4 * LANES
MOD_ROWS = 16
SEQUENTIAL = pltpu.CompilerParams(dimension_semantics=("arbitrary", "arbitrary"))


def _dot(a, b):
    return jnp.dot(a, b, preferred_element_type=F32)


def _silu(x):
    return x * jax.nn.sigmoid(x)


def _ada_kernel(cc_ref, w_ref, b_ref, o_ref):
    a = _silu(cc_ref[...])
    w = w_ref[0]
    a_hi = a.astype(BF16)
    a_lo = (a - a_hi.astype(F32)).astype(BF16)
    w_hi = w.astype(BF16)
    w_lo = (w - w_hi.astype(F32)).astype(BF16)
    acc = _dot(a_hi, w_hi) + _dot(a_lo, w_hi) + _dot(a_hi, w_lo)
    o_ref[0] = acc + b_ref[0]


def _modulation(cc, w_ada, b_ada):
    tn = 768
    return pl.pallas_call(
        _ada_kernel,
        grid=(DEPTH, 3 * D_MODEL // tn),
        in_specs=[
            pl.BlockSpec((MOD_ROWS, D_MODEL), lambda l, j: (0, 0)),
            pl.BlockSpec((1, D_MODEL, tn), lambda l, j: (l, 0, j)),
            pl.BlockSpec((1, 1, tn), lambda l, j: (l, 0, j)),
        ],
        out_specs=pl.BlockSpec((1, MOD_ROWS, tn), lambda l, j: (l, 0, j)),
        out_shape=jax.ShapeDtypeStruct((DEPTH, MOD_ROWS, 3 * D_MODEL), F32),
        name="ada_modulation",
    )(cc, w_ada, b_ada.reshape(DEPTH, 1, 3 * D_MODEL))


def _zero_after(token):
    bits = pltpu.bitcast(jnp.broadcast_to(token[:SUBLANES], (SUBLANES, LANES)), jnp.uint32)
    zero = lax.shift_right_logical(lax.shift_right_logical(bits, jnp.uint32(16)), jnp.uint32(16))
    return pltpu.bitcast(zero, F32)[:1]


def _conv_tile(pad_ref, w_ref, t0, lt, zero=None):
    lead = CONV_HALO - CONV_K // 2
    lanes = slice(lt * LANES, (lt + 1) * LANES)
    acc = None
    for r in range(SUBLANES):
        z = None
        for a in range((CONV_K - r + SUBLANES - 1) // SUBLANES):
            j = SUBLANES * a + r
            w = w_ref[j:j + 1, lanes]
            term = pad_ref[pl.ds(t0 + SUBLANES * a, BLOCK + SUBLANES), lanes] * (w if zero is None else w + zero)
            z = term if z is None else z + term
        zs = z[r + lead:r + lead + BLOCK]
        acc = zs if acc is None else acc + zs
    return acc


def _conv_finish(tiles, b_ref, lg_ref, lb_ref):
    y = jnp.concatenate(tiles, axis=1) + b_ref[...]
    mu = jnp.mean(y, axis=-1, keepdims=True)
    yc = y - mu
    var = jnp.mean(yc * yc, axis=-1, keepdims=True)
    return _silu(yc * lax.rsqrt(var + EPS) * lg_ref[...] + lb_ref[...]).astype(BF16)


def _fill_conv_pad(pad_ref, u_ref, seq):
    zeros = jnp.zeros((CONV_HALO, CONV_W), F32)
    pad_ref[0:CONV_HALO, :] = zeros
    pad_ref[CONV_HALO + seq:2 * CONV_HALO + seq, :] = zeros
    pad_ref[CONV_HALO:CONV_HALO + seq, :] = u_ref[0].astype(F32)


def _conv_fillers(pad_ref, conv_refs, cv_ref, t0, n_rows):
    w_ref, b_ref, lg_ref, lb_ref = conv_refs
    pieces = []
    for j in range(n_rows // BLOCK):
        tiles = []
        for lt in range(CONV_W // LANES):
            pieces.append(lambda token, j=j, lt=lt, tiles=tiles: tiles.append(
                _conv_tile(pad_ref, w_ref, t0 + j * BLOCK, lt, _zero_after(token))))

        def finish(token, j=j, tiles=tiles):
            cv_ref[0, j * BLOCK:(j + 1) * BLOCK, :] = _conv_finish(tiles, b_ref, lg_ref, lb_ref)
        pieces.append(finish)
    return pieces


def _conv_specs(index):
    vec = pl.BlockSpec((1, CONV_W), index)
    return [pl.BlockSpec((CONV_K, CONV_W), index), vec, vec, vec]


def _conv_args(conv_w, conv_b, ln_g, ln_b):
    row = lambda a: a.reshape(1, CONV_W)
    return conv_w, row(conv_b), row(ln_g), row(ln_b)


def _project(x_ref, sh_ref, sc_ref, g_ref, w_ref):
    x = x_ref[0]
    ms = jnp.mean(x * x, axis=-1, keepdims=True)
    gain = g_ref[...] * (1.0 + sc_ref[0])
    h = x * lax.rsqrt(ms + EPS) * gain + sh_ref[0]
    return _dot(h.astype(BF16), w_ref[0])


def _rope_block(blk, rope_ref, base):
    cos = rope_ref[:, base:base + LANES]
    sin_lo = rope_ref[:, base + LANES:base + 2 * LANES]
    sin_hi = rope_ref[:, base + 2 * LANES:base + 3 * LANES]
    return (blk * cos + pltpu.roll(blk, LANES - 16, 1) * sin_lo + pltpu.roll(blk, 16, 1) * sin_hi)


def _store_padded_kv(kt_ref, v_ref, kblk, vblk):
    tm = kblk.shape[0]
    kt = kblk.T.astype(BF16)
    zeros = jnp.zeros((HEAD_DIM, tm), BF16)
    k_blocks = ((kt[:HEAD_DIM], zeros), (zeros, kt[:HEAD_DIM]), (kt[HEAD_DIM:], zeros), (zeros, kt[HEAD_DIM:]))
    for i, halves in enumerate(k_blocks):
        kt_ref[0, i * LANES:(i + 1) * LANES, :] = jnp.concatenate(halves, axis=0)
    lo = lax.broadcasted_iota(jnp.int32, vblk.shape, 1) < HEAD_DIM
    swapped = pltpu.roll(vblk, HEAD_DIM, 1)
    v_blocks = (jnp.where(lo, vblk, 0.0), jnp.where(lo, 0.0, swapped),
                jnp.where(lo, swapped, 0.0), jnp.where(lo, 0.0, vblk))
    for i, b in enumerate(v_blocks):
        v_ref[0, :, i * LANES:(i + 1) * LANES] = b.astype(BF16)


def _project_epilogue(p, rope_ref, dft_ref, q_ref, kt_ref, v_ref, gatt_ref, gcv_ref, ab_ref, gf_ref):
    for c in range(ATT_W // LANES):
        blk = _rope_block(p[:, OFF_Q + c * LANES:OFF_Q + (c + 1) * LANES], rope_ref, 0)
        q_ref[0, :, c * LANES:(c + 1) * LANES] = blk.astype(BF16)
    _store_padded_kv(kt_ref, v_ref, _rope_block(p[:, OFF_KV:OFF_KV + KV_W], rope_ref, 3 * LANES),
                     p[:, OFF_KV + KV_W:OFF_KV + 2 * KV_W])
    gatt_ref[0] = _silu(p[:, OFF_GATT:OFF_GATT + ATT_W]).astype(BF16)
    gcv_ref[0] = _silu(p[:, OFF_GCONV:OFF_GCONV + CONV_W]).astype(BF16)
    fu = p[:, OFF_FOUR:OFF_FOUR + FOUR_W].astype(BF16)
    ab_ref[0] = _dot(fu, dft_ref[...]).astype(BF16)
    gf_ref[0] = _silu(p[:, OFF_GFOUR:OFF_GFOUR + FOUR_W]).astype(BF16)
    return (p[:, OFF_CONV:OFF_CONV + CONV_W] * jax.nn.sigmoid(p[:, OFF_CONV + CONV_W:OFF_CONV + 2 * CONV_W])
            ).astype(BF16)


def _in_kernel(x_ref, sh_ref, sc_ref, g_ref, w_ref, rope_ref, dft_ref,
               q_ref, kt_ref, v_ref, gatt_ref, gcv_ref, ab_ref, gf_ref, uc_ref):
    p = _project(x_ref, sh_ref, sc_ref, g_ref, w_ref)
    uc_ref[0] = _project_epilogue(p, rope_ref, dft_ref, q_ref, kt_ref, v_ref, gatt_ref, gcv_ref, ab_ref, gf_ref)


def _mod_spec(mod_row, j):
    row = (lambda b: mod_row) if mod_row is not None else (lambda b: b)
    return pl.BlockSpec((1, 1, D_MODEL), lambda b, i: (row(b), 0, j))


def _in_specs(tm, mod_row, layer):
    return [
        pl.BlockSpec((1, tm, D_MODEL), lambda b, i: (b, i, 0)),
        _mod_spec(mod_row, 0), _mod_spec(mod_row, 1),
        pl.BlockSpec((1, D_MODEL), lambda b, i: (0, 0)),
        pl.BlockSpec((1, D_MODEL, IN_W), lambda b, i: (layer, 0, 0)),
        pl.BlockSpec((tm, 6 * LANES), lambda b, i: (i, 0)),
        pl.BlockSpec((FOUR_W, 2 * FOUR_W), lambda b, i: (0, 0)),
    ]


def _in_outs(bn, s, tm):
    widths = (ATT_W, None, KVX_W, ATT_W, CONV_W, 2 * FOUR_W, FOUR_W, CONV_W)
    specs = [pl.BlockSpec((1, tm, w), lambda b, i: (b, i, 0)) if w else
             pl.BlockSpec((1, KVX_W, tm), lambda b, i: (b, 0, i)) for w in widths]
    shapes = [jax.ShapeDtypeStruct((bn, s, w) if w else (bn, KVX_W, s), BF16) for w in widths]
    return specs, shapes


def _in_proj(x, mod3, norm_g, w_bf, rope_tab, dft_c, *, tm, mod_row, layer):
    bn, s, _ = x.shape
    specs, shapes = _in_outs(bn, s, tm)
    return pl.pallas_call(
        _in_kernel,
        grid=(bn, s // tm),
        in_specs=_in_specs(tm, mod_row, layer),
        out_specs=specs,
        out_shape=shapes,
        name="in_proj",
    )(x, mod3, mod3, norm_g.reshape(1, D_MODEL), w_bf, rope_tab, dft_c)


def _kv_kernel(x_ref, sh_ref, sc_ref, g_ref, w_ref, kt_ref, v_ref):
    kvp = _project(x_ref, sh_ref, sc_ref, g_ref, w_ref)
    _store_padded_kv(kt_ref, v_ref, kvp[:, 0:KV_W], kvp[:, KV_W:2 * KV_W])


def _kv_proj(x, mod3, norm_g, w_bf, *, tm, mod_row, layer):
    bn, s, _ = x.shape
    assert OFF_KV % (2 * KV_W) == 0
    return pl.pallas_call(
        _kv_kernel,
        grid=(bn, s // tm),
        in_specs=[
            pl.BlockSpec((1, tm, D_MODEL), lambda b, i: (b, i, 0)),
            _mod_spec(mod_row, 0), _mod_spec(mod_row, 1),
            pl.BlockSpec((1, D_MODEL), lambda b, i: (0, 0)),
            pl.BlockSpec((1, D_MODEL, 2 * KV_W), lambda b, i: (layer, 0, OFF_KV // (2 * KV_W))),
        ],
        out_specs=[pl.BlockSpec((1, KVX_W, tm), lambda b, i: (b, 0, i)),
                   pl.BlockSpec((1, tm, KVX_W), lambda b, i: (b, i, 0))],
        out_shape=[jax.ShapeDtypeStruct((bn, KVX_W, s), BF16), jax.ShapeDtypeStruct((bn, s, KVX_W), BF16)],
        name="ctx_kv_proj",
    )(x, mod3, mod3, norm_g.reshape(1, D_MODEL), w_bf)


class _Chain:
    def __init__(self, q2, sink_col, kset):
        self.q2, self.sink_col, self.kset = q2, sink_col, kset

    def scores(self):
        self.s = []
        for kt, _, bias in self.kset:
            sc = _dot(self.q2, kt)
            self.s.append(sc if bias is None else sc + bias)

    def softmax(self):
        m = self.sink_col
        for sc in self.s:
            m = jnp.maximum(m, sc.max(axis=-1, keepdims=True))
        self.p = [jnp.exp2(sc - m).astype(BF16) for sc in self.s]
        self.extra = jnp.exp2(self.sink_col - m)
        self.row_max = m
        self.s = None

    def values(self):
        r = None
        for p, (_, v, _) in zip(self.p, self.kset):
            pv = _dot(p, jnp.concatenate([v, jnp.ones_like(v)], axis=1))
            r = pv if r is None else r + pv
        self.p = None
        return r[:, :LANES] / (r[:, LANES:] + self.extra)


def _run_chains(chains, finish, fillers, ahead):
    n = len(chains)
    fillers = list(fillers)
    for i in range(-ahead, n):
        if 0 <= i + ahead < n:
            chains[i + ahead].scores()
        if 0 <= i < n:
            finish(i, chains[i].values())
        if 0 <= i + 1 < n:
            chains[i + 1].softmax()
            if fillers:
                fillers.pop(0)(chains[i + 1].row_max)
    for f in fillers:
        f(chains[-1].row_max)


def _kv_block(kvh, e):
    k0 = (2 * kvh + e) * LANES
    return slice(k0, k0 + LANES)


def _group_chains(sink_ref, q_ref, rows, tq, kvh, ksets):
    c0 = 2 * kvh * LANES
    q2 = jnp.concatenate([q_ref[0, rows, c0:c0 + LANES], q_ref[0, rows, c0 + LANES:c0 + 2 * LANES]], axis=0)
    chains = []
    for e in range(2):
        sink_col = jnp.concatenate([jnp.full((tq, 1), sink_ref[GQA_GROUP * kvh + e] * LOG2E, F32),
                                    jnp.full((tq, 1), sink_ref[GQA_GROUP * kvh + 2 + e] * LOG2E, F32)], axis=0)
        chains.append(_Chain(q2, sink_col, ksets[e]))
    return chains


def _make_finish(groups, g_ref, o_ref):
    partial = {}

    def finish(i, out):
        if i % 2 == 0:
            partial[i // 2] = out
            return
        rows, tq, kvh = groups[i // 2]
        o = partial.pop(i // 2) + out
        c0 = 2 * kvh * LANES
        for hf in range(2):
            cols = slice(c0 + hf * LANES, c0 + (hf + 1) * LANES)
            g = g_ref[0, rows, cols].astype(F32)
            o_ref[0, rows, cols] = (o[hf * tq:(hf + 1) * tq] * g).astype(BF16)
    return finish


def _local_attn_kernel(sink_ref, q_ref, kt_ref, v_ref, ktc_ref, vc_ref, bias_ref, g_ref, uc_ref,
                       cw_ref, cb_ref, lg_ref, lb_ref, o_ref, cv_ref, pad_ref, *, seq, sub, ahead):
    @pl.when(pl.program_id(1) == 0)
    def _():
        _fill_conv_pad(pad_ref, uc_ref, seq)

    nblk = seq // BLOCK
    chains, groups = [], []
    for sb in range(sub):
        n = pl.program_id(1) * sub + sb
        start = pl.multiple_of(jnp.clip((n - 1) * BLOCK, 0, seq - 3 * BLOCK), BLOCK)
        win = pl.ds(start, 3 * BLOCK)
        bias = bias_ref[jnp.where(n == 0, 0, jnp.where(n == nblk - 1, 2, 1))]
        bias2 = jnp.concatenate([bias, bias], axis=0)
        rows = slice(sb * BLOCK, (sb + 1) * BLOCK)
        for kvh in range(N_KV_HEADS):
            ksets = []
            for e in range(2):
                blk = _kv_block(kvh, e)
                ksets.append([(kt_ref[0, blk, win], v_ref[0, win, blk], bias2),
                              (ktc_ref[0, blk, :], vc_ref[0, :, blk], None)])
            chains += _group_chains(sink_ref, q_ref, rows, BLOCK, kvh, ksets)
            groups.append((rows, BLOCK, kvh))
    t0 = pl.multiple_of(pl.program_id(1) * (sub * BLOCK), sub * BLOCK)
    fillers = _conv_fillers(pad_ref, (cw_ref, cb_ref, lg_ref, lb_ref), cv_ref, t0, sub * BLOCK)
    _run_chains(chains, _make_finish(groups, g_ref, o_ref), fillers, ahead)


def _band_bias():
    i = np.arange(BLOCK)[:, None]
    k = np.arange(3 * BLOCK)[None, :]
    tabs = [np.where(np.abs(BLOCK * qb + i - k) <= WINDOW, 0.0, NEG_INF) for qb in range(3)]
    return jnp.asarray(np.stack(tabs).astype(np.float32))


def _local_attention(sink, q, kv, kvc, gatt, uc, conv_args):
    bn, s, _ = q.shape
    n_ctx = kvc[1].shape[1]
    sub = 8
    ahead = 4
    tq = sub * BLOCK
    assert s % tq == 0 and s >= 3 * BLOCK
    tile = lambda w: pl.BlockSpec((1, tq, w), lambda b, n: (b, n, 0))
    whole = lambda rows, w: pl.BlockSpec((1, rows, w), lambda b, n: (b, 0, 0))
    kv_specs = lambda rows: [whole(KVX_W, rows), whole(rows, KVX_W)]
    return pl.pallas_call(
        functools.partial(_local_attn_kernel, seq=s, sub=sub, ahead=ahead),
        grid=(bn, s // tq),
        in_specs=[
            pl.BlockSpec(memory_space=pltpu.SMEM),
            tile(ATT_W), *kv_specs(s), *kv_specs(n_ctx),
            pl.BlockSpec((3, BLOCK, 3 * BLOCK), lambda b, n: (0, 0, 0)),
            tile(ATT_W), whole(s, CONV_W),
        ] + _conv_specs(lambda b, n: (0, 0)),
        out_specs=[tile(ATT_W), tile(CONV_W)],
        out_shape=[jax.ShapeDtypeStruct((bn, s, ATT_W), BF16), jax.ShapeDtypeStruct((bn, s, CONV_W), BF16)],
        scratch_shapes=[pltpu.VMEM((s + 2 * CONV_HALO, CONV_W), F32)],
        compiler_params=SEQUENTIAL,
        name="local_attention",
    )(sink, q, *kv, *kvc, _band_bias(), gatt, uc, *conv_args)


def _ctx_attn_kernel(sink_ref, q_ref, ktc_ref, vc_ref, g_ref, uc_ref, cw_ref, cb_ref, lg_ref, lb_ref,
                     o_ref, cv_ref, pad_ref, *, n_ctx, ahead):
    _fill_conv_pad(pad_ref, uc_ref, n_ctx)
    rows = slice(0, n_ctx)
    chains, groups = [], []
    for kvh in range(N_KV_HEADS):
        ksets = []
        for e in range(2):
            blk = _kv_block(kvh, e)
            ksets.append([(ktc_ref[0, blk, :], vc_ref[0, :, blk], None)])
        chains += _group_chains(sink_ref, q_ref, rows, n_ctx, kvh, ksets)
        groups.append((rows, n_ctx, kvh))
    fillers = _conv_fillers(pad_ref, (cw_ref, cb_ref, lg_ref, lb_ref), cv_ref, 0, n_ctx)
    _run_chains(chains, _make_finish(groups, g_ref, o_ref), fillers, ahead)


def _ctx_attention(sink, q, kvc, gatt, uc, conv_args):
    bn, n_ctx, _ = q.shape
    ahead = 2
    whole = lambda w: pl.BlockSpec((1, n_ctx, w), lambda b: (b, 0, 0))
    return pl.pallas_call(
        functools.partial(_ctx_attn_kernel, n_ctx=n_ctx, ahead=ahead),
        grid=(bn,),
        in_specs=[pl.BlockSpec(memory_space=pltpu.SMEM), whole(ATT_W),
                  pl.BlockSpec((1, KVX_W, n_ctx), lambda b: (b, 0, 0)), whole(KVX_W), whole(ATT_W),
                  whole(CONV_W)] + _conv_specs(lambda b: (0, 0)),
        out_specs=[whole(ATT_W), whole(CONV_W)],
        out_shape=[jax.ShapeDtypeStruct((bn, n_ctx, ATT_W), BF16),
                   jax.ShapeDtypeStruct((bn, n_ctx, CONV_W), BF16)],
        scratch_shapes=[pltpu.VMEM((n_ctx + 2 * CONV_HALO, CONV_W), F32)],
        name="ctx_attention",
    )(sink, q, *kvc, gatt, uc, *conv_args)


def _fourier_prepare(ab_ref, cw_ref, sw_ref, rc_ref, rs_ref, half):
    top = ab_ref[0, 0:half, :].astype(F32)
    bot = ab_ref[0, half:2 * half, :].astype(F32)
    plus = top + bot
    minus = top - bot
    a_m, b_m = minus[:, :FOUR_W], minus[:, FOUR_W:]
    cw, sw = cw_ref[...], sw_ref[...]
    rc_ref[:, 0:FOUR_W] = plus[:, :FOUR_W].astype(BF16)
    rc_ref[:, FOUR_W:] = (cw * a_m - sw * b_m).astype(BF16)
    rs_ref[:, 0:FOUR_W] = plus[:, FOUR_W:].astype(BF16)
    rs_ref[:, FOUR_W:] = (sw * a_m + cw * b_m).astype(BF16)


def _fourier_rows(c_ref, s_ref, rc_ref, rs_ref, wf_ref, bf_ref, il_ref, r0, chunk):
    e = _dot(c_ref[pl.ds(r0, chunk), :], rc_ref[...]) - _dot(s_ref[pl.ds(r0, chunk), :], rs_ref[...])
    eb = e.astype(BF16)
    for p in range(2):
        f = _dot(eb[:, p * FOUR_W:(p + 1) * FOUR_W], wf_ref[...])
        for lt in range(FOUR_W // LANES):
            il_ref[lt, pl.ds(p, chunk, stride=2), :] = f[:, lt * LANES:(lt + 1) * LANES]
    return jnp.concatenate([il_ref[lt] for lt in range(FOUR_W // LANES)], axis=1) + bf_ref[...]


def _out_kernel(x_ref, att_ref, cv_ref, gcv_ref, ab_ref, gf_ref, c_ref, s_ref, cw_ref, sw_ref, wf_ref, bf_ref,
                w_ref, gate_ref, fg_ref, o_ref, rc_ref, rs_ref, il_ref, *, tm, half, final):
    @pl.when(pl.program_id(1) == 0)
    def _():
        _fourier_prepare(ab_ref, cw_ref, sw_ref, rc_ref, rs_ref, half)

    r0 = pl.multiple_of(pl.program_id(1) * (tm // 2), tm // 2)
    four = _fourier_rows(c_ref, s_ref, rc_ref, rs_ref, wf_ref, bf_ref, il_ref, r0, tm // 2)
    fo = (four * gf_ref[0].astype(F32)).astype(BF16)
    cv = cv_ref[0] * gcv_ref[0]
    y = _dot(jnp.concatenate([att_ref[0], cv, fo], axis=1), w_ref[0])
    xn = x_ref[0] + gate_ref[0] * y
    if final:
        ms = jnp.mean(xn * xn, axis=-1, keepdims=True)
        xn = xn * lax.rsqrt(ms + EPS) * fg_ref[...]
    o_ref[0] = xn


def _out_proj(x, att, cv, gcv, ab, gf, pos_tabs, w_four_bf, b_four, w_out_bf, mod3, final_g,
              *, tm, mod_row, final, layer):
    bn, s, _ = x.shape
    half = s // 2
    c_half, s_half, cw, sw = pos_tabs
    tile = lambda w: pl.BlockSpec((1, tm, w), lambda b, i: (b, i, 0))
    const = lambda shape: pl.BlockSpec(shape, lambda b, i: (0,) * len(shape))
    return pl.pallas_call(
        functools.partial(_out_kernel, tm=tm, half=half, final=final),
        grid=(bn, s // tm),
        in_specs=[
            tile(D_MODEL), tile(ATT_W), tile(CONV_W), tile(CONV_W),
            pl.BlockSpec((1, s, 2 * FOUR_W), lambda b, i: (b, 0, 0)),
            tile(FOUR_W),
            const((half, half)), const((half, half)), const((half, FOUR_W)), const((half, FOUR_W)),
            const((FOUR_W, FOUR_W)), const((1, FOUR_W)),
            pl.BlockSpec((1, D_MODEL, D_MODEL), lambda b, i: (layer, 0, 0)),
            _mod_spec(mod_row, 2),
            const((1, D_MODEL)),
        ],
        out_specs=tile(D_MODEL),
        out_shape=jax.ShapeDtypeStruct((bn, s, D_MODEL), F32),
        scratch_shapes=[pltpu.VMEM((half, 2 * FOUR_W), BF16), pltpu.VMEM((half, 2 * FOUR_W), BF16),
                        pltpu.VMEM((FOUR_W // LANES, tm, LANES), F32)],
        compiler_params=SEQUENTIAL,
        name="out_proj",
    )(x, att, cv, gcv, ab, gf, c_half, s_half, cw, sw, w_four_bf, b_four.reshape(1, FOUR_W),
      w_out_bf, mod3, final_g.reshape(1, D_MODEL))


def _rope_tables(s):
    t = np.arange(s)
    half = HEAD_DIM // 4
    freqs = ROPE_BASE ** (-np.arange(half) / half)
    zero = np.zeros((s, half))

    def per_head(fn_lo, fn_hi):
        parts = []
        for pos in (t // GRID_W, t % GRID_W):
            ang = pos[:, None] * freqs[None, :]
            parts += [fn_lo(ang), fn_hi(ang)]
        return np.tile(np.concatenate(parts, axis=-1), (1, LANES // HEAD_DIM))

    cos = per_head(np.cos, np.cos)
    sin_lo = per_head(lambda a: -np.sin(a), lambda a: zero)
    sin_hi = per_head(lambda a: zero, np.sin)
    tab = np.concatenate([cos * Q_SCALE, sin_lo * Q_SCALE, sin_hi * Q_SCALE, cos, sin_lo, sin_hi], axis=-1)
    return jnp.asarray(tab.astype(np.float32))


def _identity_rope_tables(s):
    one = np.ones((s, LANES))
    zero = np.zeros((s, LANES))
    return jnp.asarray(np.concatenate([one * Q_SCALE, zero, zero, one, zero, zero], axis=-1).astype(np.float32))


def _channel_dft(n_pos):
    c = np.arange(FOUR_HEAD_DIM)
    ang = 2.0 * np.pi * ((c[:, None] * c[None, :]) % FOUR_HEAD_DIM) / FOUR_HEAD_DIM
    eye = np.eye(FOUR_HEADS)
    scale = 1.0 / np.sqrt(float(n_pos * FOUR_HEAD_DIM))
    tab = np.concatenate([np.kron(eye, np.cos(ang)), np.kron(eye, np.sin(ang))], axis=1) * scale
    return jnp.asarray(tab.astype(np.float32))


def _position_dft(n_pos):
    half = n_pos // 2
    k = np.arange(half)
    ang = 2.0 * np.pi * ((k[:, None] * k[None, :]) % half) / half
    beta = np.pi * k / half
    ones = np.ones((1, FOUR_W))
    f32 = lambda a: jnp.asarray(a.astype(np.float32))
    return (f32(np.cos(ang)).astype(BF16), f32(np.sin(ang)).astype(BF16),
            f32(np.cos(beta)[:, None] * ones), f32(np.sin(beta)[:, None] * ones))


def kernel(x, c, ctx, c_ctx, w_ada, b_ada, norm_g, w_in, attn_sink, conv_w, conv_b,
           conv_ln_g, conv_ln_b, w_four, b_four, w_out, final_g):
    bn, s, _ = x.shape
    n_ctx = ctx.shape[1]
    assert bn < MOD_ROWS
    ctx_row = bn

    cc = jnp.zeros((MOD_ROWS, D_MODEL), F32).at[:bn].set(c).at[ctx_row].set(c_ctx)
    mod = _modulation(cc, w_ada, b_ada)

    w_in_bf = w_in.astype(BF16)
    w_out_bf = w_out.astype(BF16)
    w_four_bf = w_four.astype(BF16)
    rope_x = _rope_tables(s)
    rope_c = _identity_rope_tables(n_ctx)
    dft_x = _channel_dft(s).astype(BF16)
    dft_ctx = _channel_dft(n_ctx).astype(BF16)
    pos_x = _position_dft(s)
    pos_ctx = _position_dft(n_ctx)

    h_ctx = ctx
    for l in range(DEPTH):
        mod3 = mod[l].reshape(MOD_ROWS, 1, 3 * D_MODEL)
        conv_args = _conv_args(conv_w[l], conv_b[l], conv_ln_g[l], conv_ln_b[l])
        four_args = (w_four_bf[l], b_four[l], w_out_bf, mod3, final_g)
        sink = attn_sink[l]
        last = l == DEPTH - 1
        if not last:
            qc, ktc, vc, gatt_c, gcv_c, ab_c, gf_c, uc_c = _in_proj(
                h_ctx, mod3, norm_g[l], w_in_bf, rope_c, dft_ctx, tm=n_ctx, mod_row=ctx_row, layer=l)
            kvc = (ktc, vc)
            att_c, cv_c = _ctx_attention(sink, qc, kvc, gatt_c, uc_c, conv_args)
        else:
            kvc = _kv_proj(h_ctx, mod3, norm_g[l], w_in_bf, tm=n_ctx, mod_row=ctx_row, layer=l)
        q, kt, v, gatt, gcv, ab, gf, uc = _in_proj(
            x, mod3, norm_g[l], w_in_bf, rope_x, dft_x, tm=512, mod_row=None, layer=l)
        att, cv = _local_attention(sink, q, (kt, v), kvc, gatt, uc, conv_args)
        if not last:
            h_ctx = _out_proj(h_ctx, att_c, cv_c, gcv_c, ab_c, gf_c, pos_ctx, *four_args,
                              tm=n_ctx, mod_row=ctx_row, final=False, layer=l)
        x = _out_proj(x, att, cv, gcv, ab, gf, pos_x, *four_args, tm=512, mod_row=None, final=last, layer=l)
    return x
```

```python
import functools

import numpy as np
import jax
import jax.numpy as jnp
from jax import lax
from jax.experimental import pallas as pl
from jax.experimental.pallas import tpu as pltpu

F32 = jnp.float32
BF16 = jnp.bfloat16

D_MODEL = 1024
DEPTH = 2
GRID_W = 64
HEAD_DIM = 64
ATT_W = 512
N_HEADS = 8
N_KV_HEADS = 2
GQA_GROUP = N_HEADS // N_KV_HEADS
KV_W = N_KV_HEADS * HEAD_DIM
CONV_W = 256
FOUR_W = 256
FOUR_HEADS = 4
FOUR_HEAD_DIM = FOUR_W // FOUR_HEADS
CONV_K = 31
SUBLANES = 8
CONV_HALO = 16
WINDOW = 128
BLOCK = 128
ROPE_BASE = 10000.0
EPS = 1e-6
NEG_INF = -1e30
LOG2E = 1.4426950408889634
Q_SCALE = HEAD_DIM ** -0.5 * LOG2E
IN_W = 2 * ATT_W + 2 * KV_W + 3 * CONV_W + 2 * FOUR_W
OFF_Q = 0
OFF_KV = ATT_W
OFF_GATT = OFF_KV + 2 * KV_W
OFF_CONV = OFF_GATT + ATT_W
OFF_GCONV = OFF_CONV + 2 * CONV_W
OFF_FOUR = OFF_GCONV + CONV_W
OFF_GFOUR = OFF_FOUR + FOUR_W
LANES = 128
KVX_W = 4 * LANES
MOD_ROWS = 16
SEQUENTIAL = pltpu.CompilerParams(dimension_semantics=("arbitrary", "arbitrary"))


def _dot(a, b):
    return jnp.dot(a, b, preferred_element_type=F32)


def _silu(x):
    return x * jax.nn.sigmoid(x)


def _ada_kernel(cc_ref, w_ref, b_ref, o_ref):
    a = _silu(cc_ref[...])
    w = w_ref[0]
    a_hi = a.astype(BF16)
    a_lo = (a - a_hi.astype(F32)).astype(BF16)
    w_hi = w.astype(BF16)
    w_lo = (w - w_hi.astype(F32)).astype(BF16)
    acc = _dot(a_hi, w_hi) + _dot(a_lo, w_hi) + _dot(a_hi, w_lo)
    o_ref[0] = acc + b_ref[0]


def _modulation(cc, w_ada, b_ada):
    tn = 768
    return pl.pallas_call(
        _ada_kernel,
        grid=(DEPTH, 3 * D_MODEL // tn),
        in_specs=[
            pl.BlockSpec((MOD_ROWS, D_MODEL), lambda l, j: (0, 0)),
            pl.BlockSpec((1, D_MODEL, tn), lambda l, j: (l, 0, j)),
            pl.BlockSpec((1, 1, tn), lambda l, j: (l, 0, j)),
        ],
        out_specs=pl.BlockSpec((1, MOD_ROWS, tn), lambda l, j: (l, 0, j)),
        out_shape=jax.ShapeDtypeStruct((DEPTH, MOD_ROWS, 3 * D_MODEL), F32),
        name="ada_modulation",
    )(cc, w_ada, b_ada.reshape(DEPTH, 1, 3 * D_MODEL))


def _zero_after(token):
    bits = pltpu.bitcast(jnp.broadcast_to(token[:SUBLANES], (SUBLANES, LANES)), jnp.uint32)
    zero = lax.shift_right_logical(lax.shift_right_logical(bits, jnp.uint32(16)), jnp.uint32(16))
    return pltpu.bitcast(zero, F32)[:1]


def _conv_tile(pad_ref, w_ref, t0, lt, zero=None):
    lead = CONV_HALO - CONV_K // 2
    lanes = slice(lt * LANES, (lt + 1) * LANES)
    acc = None
    for r in range(SUBLANES):
        z = None
        for a in range((CONV_K - r + SUBLANES - 1) // SUBLANES):
            j = SUBLANES * a + r
            w = w_ref[j:j + 1, lanes]
            term = pad_ref[pl.ds(t0 + SUBLANES * a, BLOCK + SUBLANES), lanes] * (w if zero is None else w + zero)
            z = term if z is None else z + term
        zs = z[r + lead:r + lead + BLOCK]
        acc = zs if acc is None else acc + zs
    return acc


def _conv_finish(tiles, b_ref, lg_ref, lb_ref):
    y = jnp.concatenate(tiles, axis=1) + b_ref[...]
    mu = jnp.mean(y, axis=-1, keepdims=True)
    yc = y - mu
    var = jnp.mean(yc * yc, axis=-1, keepdims=True)
    return _silu(yc * lax.rsqrt(var + EPS) * lg_ref[...] + lb_ref[...]).astype(BF16)


def _fill_conv_pad(pad_ref, u_ref, seq):
    zeros = jnp.zeros((CONV_HALO, CONV_W), F32)
    pad_ref[0:CONV_HALO, :] = zeros
    pad_ref[CONV_HALO + seq:2 * CONV_HALO + seq, :] = zeros
    pad_ref[CONV_HALO:CONV_HALO + seq, :] = u_ref[0].astype(F32)


def _conv_fillers(pad_ref, conv_refs, cv_ref, t0, n_rows):
    w_ref, b_ref, lg_ref, lb_ref = conv_refs
    pieces = []
    for j in range(n_rows // BLOCK):
        tiles = []
        for lt in range(CONV_W // LANES):
            pieces.append(lambda token, j=j, lt=lt, tiles=tiles: tiles.append(
                _conv_tile(pad_ref, w_ref, t0 + j * BLOCK, lt, _zero_after(token))))

        def finish(token, j=j, tiles=tiles):
            cv_ref[0, j * BLOCK:(j + 1) * BLOCK, :] = _conv_finish(tiles, b_ref, lg_ref, lb_ref)
        pieces.append(finish)
    return pieces


def _conv_specs(index):
    vec = pl.BlockSpec((1, CONV_W), index)
    return [pl.BlockSpec((CONV_K, CONV_W), index), vec, vec, vec]


def _conv_args(conv_w, conv_b, ln_g, ln_b):
    row = lambda a: a.reshape(1, CONV_W)
    return conv_w, row(conv_b), row(ln_g), row(ln_b)


def _cast_weight_once(w_ref, wbf_ref):
    @pl.when((pl.program_id(0) == 0) & (pl.program_id(1) == 0))
    def _():
        wbf_ref[...] = w_ref[0].astype(BF16)


def _project(x_ref, sh_ref, sc_ref, g_ref, w_ref, wbf_ref):
    _cast_weight_once(w_ref, wbf_ref)
    x = x_ref[0]
    ms = jnp.mean(x * x, axis=-1, keepdims=True)
    gain = g_ref[...] * (1.0 + sc_ref[0])
    h = x * lax.rsqrt(ms + EPS) * gain + sh_ref[0]
    return _dot(h.astype(BF16), wbf_ref[...])


def _rope_block(blk, rope_ref, base):
    cos = rope_ref[:, base:base + LANES]
    sin_lo = rope_ref[:, base + LANES:base + 2 * LANES]
    sin_hi = rope_ref[:, base + 2 * LANES:base + 3 * LANES]
    return (blk * cos + pltpu.roll(blk, LANES - 16, 1) * sin_lo + pltpu.roll(blk, 16, 1) * sin_hi)


def _store_padded_kv(kt_ref, v_ref, kblk, vblk):
    tm = kblk.shape[0]
    kt = kblk.T.astype(BF16)
    zeros = jnp.zeros((HEAD_DIM, tm), BF16)
    k_blocks = ((kt[:HEAD_DIM], zeros), (zeros, kt[:HEAD_DIM]), (kt[HEAD_DIM:], zeros), (zeros, kt[HEAD_DIM:]))
    for i, halves in enumerate(k_blocks):
        kt_ref[0, i * LANES:(i + 1) * LANES, :] = jnp.concatenate(halves, axis=0)
    lo = lax.broadcasted_iota(jnp.int32, vblk.shape, 1) < HEAD_DIM
    swapped = pltpu.roll(vblk, HEAD_DIM, 1)
    v_blocks = (jnp.where(lo, vblk, 0.0), jnp.where(lo, 0.0, swapped),
                jnp.where(lo, swapped, 0.0), jnp.where(lo, 0.0, vblk))
    for i, b in enumerate(v_blocks):
        v_ref[0, :, i * LANES:(i + 1) * LANES] = b.astype(BF16)


def _project_epilogue(p, rope_ref, dft_ref, q_ref, kt_ref, v_ref, gatt_ref, gcv_ref, ab_ref, gf_ref):
    for c in range(ATT_W // LANES):
        blk = _rope_block(p[:, OFF_Q + c * LANES:OFF_Q + (c + 1) * LANES], rope_ref, 0)
        q_ref[0, :, c * LANES:(c + 1) * LANES] = blk.astype(BF16)
    _store_padded_kv(kt_ref, v_ref, _rope_block(p[:, OFF_KV:OFF_KV + KV_W], rope_ref, 3 * LANES),
                     p[:, OFF_KV + KV_W:OFF_KV + 2 * KV_W])
    gatt_ref[0] = _silu(p[:, OFF_GATT:OFF_GATT + ATT_W]).astype(BF16)
    gcv_ref[0] = _silu(p[:, OFF_GCONV:OFF_GCONV + CONV_W]).astype(BF16)
    fu = p[:, OFF_FOUR:OFF_FOUR + FOUR_W].astype(BF16)
    ab_ref[0] = _dot(fu, dft_ref[...]).astype(BF16)
    gf_ref[0] = _silu(p[:, OFF_GFOUR:OFF_GFOUR + FOUR_W]).astype(BF16)
    return (p[:, OFF_CONV:OFF_CONV + CONV_W] * jax.nn.sigmoid(p[:, OFF_CONV + CONV_W:OFF_CONV + 2 * CONV_W])
            ).astype(BF16)


def _in_kernel(x_ref, sh_ref, sc_ref, g_ref, w_ref, rope_ref, dft_ref,
               q_ref, kt_ref, v_ref, gatt_ref, gcv_ref, ab_ref, gf_ref, uc_ref, wbf_ref):
    p = _project(x_ref, sh_ref, sc_ref, g_ref, w_ref, wbf_ref)
    uc_ref[0] = _project_epilogue(p, rope_ref, dft_ref, q_ref, kt_ref, v_ref, gatt_ref, gcv_ref, ab_ref, gf_ref)


def _mod_spec(mod_row, j):
    row = (lambda b: mod_row) if mod_row is not None else (lambda b: b)
    return pl.BlockSpec((1, 1, D_MODEL), lambda b, i: (row(b), 0, j))


def _in_specs(tm, mod_row, layer):
    return [
        pl.BlockSpec((1, tm, D_MODEL), lambda b, i: (b, i, 0)),
        _mod_spec(mod_row, 0), _mod_spec(mod_row, 1),
        pl.BlockSpec((1, D_MODEL), lambda b, i: (0, 0)),
        pl.BlockSpec((1, D_MODEL, IN_W), lambda b, i: (layer, 0, 0)),
        pl.BlockSpec((tm, 6 * LANES), lambda b, i: (i, 0)),
        pl.BlockSpec((FOUR_W, 2 * FOUR_W), lambda b, i: (0, 0)),
    ]


def _in_outs(bn, s, tm):
    widths = (ATT_W, None, KVX_W, ATT_W, CONV_W, 2 * FOUR_W, FOUR_W, CONV_W)
    specs = [pl.BlockSpec((1, tm, w), lambda b, i: (b, i, 0)) if w else
             pl.BlockSpec((1, KVX_W, tm), lambda b, i: (b, 0, i)) for w in widths]
    shapes = [jax.ShapeDtypeStruct((bn, s, w) if w else (bn, KVX_W, s), BF16) for w in widths]
    return specs, shapes


def _in_proj(x, mod3, norm_g, w, rope_tab, dft_c, *, tm, mod_row, layer):
    bn, s, _ = x.shape
    specs, shapes = _in_outs(bn, s, tm)
    return pl.pallas_call(
        _in_kernel,
        grid=(bn, s // tm),
        in_specs=_in_specs(tm, mod_row, layer),
        out_specs=specs,
        out_shape=shapes,
        scratch_shapes=[pltpu.VMEM((D_MODEL, IN_W), BF16)],
        compiler_params=SEQUENTIAL,
        name="in_proj",
    )(x, mod3, mod3, norm_g.reshape(1, D_MODEL), w, rope_tab, dft_c)


def _kv_kernel(x_ref, sh_ref, sc_ref, g_ref, w_ref, kt_ref, v_ref, wbf_ref):
    kvp = _project(x_ref, sh_ref, sc_ref, g_ref, w_ref, wbf_ref)
    _store_padded_kv(kt_ref, v_ref, kvp[:, 0:KV_W], kvp[:, KV_W:2 * KV_W])


def _kv_proj(x, mod3, norm_g, w, *, tm, mod_row, layer):
    bn, s, _ = x.shape
    assert OFF_KV % (2 * KV_W) == 0
    return pl.pallas_call(
        _kv_kernel,
        grid=(bn, s // tm),
        in_specs=[
            pl.BlockSpec((1, tm, D_MODEL), lambda b, i: (b, i, 0)),
            _mod_spec(mod_row, 0), _mod_spec(mod_row, 1),
            pl.BlockSpec((1, D_MODEL), lambda b, i: (0, 0)),
            pl.BlockSpec((1, D_MODEL, 2 * KV_W), lambda b, i: (layer, 0, OFF_KV // (2 * KV_W))),
        ],
        out_specs=[pl.BlockSpec((1, KVX_W, tm), lambda b, i: (b, 0, i)),
                   pl.BlockSpec((1, tm, KVX_W), lambda b, i: (b, i, 0))],
        out_shape=[jax.ShapeDtypeStruct((bn, KVX_W, s), BF16), jax.ShapeDtypeStruct((bn, s, KVX_W), BF16)],
        scratch_shapes=[pltpu.VMEM((D_MODEL, 2 * KV_W), BF16)],
        compiler_params=SEQUENTIAL,
        name="ctx_kv_proj",
    )(x, mod3, mod3, norm_g.reshape(1, D_MODEL), w)


class _Chain:
    def __init__(self, q2, sink_col, kset):
        self.q2, self.sink_col, self.kset = q2, sink_col, kset

    def scores(self):
        self.s = []
        for kt, _, bias in self.kset:
            sc = _dot(self.q2, kt)
            self.s.append(sc if bias is None else sc + bias)

    def softmax(self):
        m = self.sink_col
        for sc in self.s:
            m = jnp.maximum(m, sc.max(axis=-1, keepdims=True))
        self.p = [jnp.exp2(sc - m).astype(BF16) for sc in self.s]
        self.extra = jnp.exp2(self.sink_col - m)
        self.row_max = m
        self.s = None

    def values(self):
        r = None
        for p, (_, v, _) in zip(self.p, self.kset):
            pv = _dot(p, jnp.concatenate([v, jnp.ones_like(v)], axis=1))
            r = pv if r is None else r + pv
        self.p = None
        return r[:, :LANES] / (r[:, LANES:] + self.extra)


def _run_chains(chains, finish, fillers, ahead):
    n = len(chains)
    fillers = list(fillers)
    for i in range(-ahead, n):
        if 0 <= i + ahead < n:
            chains[i + ahead].scores()
        if 0 <= i < n:
            finish(i, chains[i].values())
        if 0 <= i + 1 < n:
            chains[i + 1].softmax()
            if fillers:
                fillers.pop(0)(chains[i + 1].row_max)
    for f in fillers:
        f(chains[-1].row_max)


def _kv_block(kvh, e):
    k0 = (2 * kvh + e) * LANES
    return slice(k0, k0 + LANES)


def _group_chains(sink_ref, q_ref, rows, tq, kvh, ksets):
    c0 = 2 * kvh * LANES
    q2 = jnp.concatenate([q_ref[0, rows, c0:c0 + LANES], q_ref[0, rows, c0 + LANES:c0 + 2 * LANES]], axis=0)
    chains = []
    for e in range(2):
        sink_col = jnp.concatenate([jnp.full((tq, 1), sink_ref[GQA_GROUP * kvh + e] * LOG2E, F32),
                                    jnp.full((tq, 1), sink_ref[GQA_GROUP * kvh + 2 + e] * LOG2E, F32)], axis=0)
        chains.append(_Chain(q2, sink_col, ksets[e]))
    return chains


def _make_finish(groups, g_ref, o_ref):
    partial = {}

    def finish(i, out):
        if i % 2 == 0:
            partial[i // 2] = out
            return
        rows, tq, kvh = groups[i // 2]
        o = partial.pop(i // 2) + out
        c0 = 2 * kvh * LANES
        for hf in range(2):
            cols = slice(c0 + hf * LANES, c0 + (hf + 1) * LANES)
            g = g_ref[0, rows, cols].astype(F32)
            o_ref[0, rows, cols] = (o[hf * tq:(hf + 1) * tq] * g).astype(BF16)
    return finish


def _local_attn_kernel(sink_ref, q_ref, kt_ref, v_ref, ktc_ref, vc_ref, bias_ref, g_ref, uc_ref,
                       cw_ref, cb_ref, lg_ref, lb_ref, o_ref, cv_ref, pad_ref, *, seq, sub, ahead):
    @pl.when(pl.program_id(1) == 0)
    def _():
        _fill_conv_pad(pad_ref, uc_ref, seq)

    nblk = seq // BLOCK
    chains, groups = [], []
    for sb in range(sub):
        n = pl.program_id(1) * sub + sb
        start = pl.multiple_of(jnp.clip((n - 1) * BLOCK, 0, seq - 3 * BLOCK), BLOCK)
        win = pl.ds(start, 3 * BLOCK)
        bias = bias_ref[jnp.where(n == 0, 0, jnp.where(n == nblk - 1, 2, 1))]
        bias2 = jnp.concatenate([bias, bias], axis=0)
        rows = slice(sb * BLOCK, (sb + 1) * BLOCK)
        for kvh in range(N_KV_HEADS):
            ksets = []
            for e in range(2):
                blk = _kv_block(kvh, e)
                ksets.append([(kt_ref[0, blk, win], v_ref[0, win, blk], bias2),
                              (ktc_ref[0, blk, :], vc_ref[0, :, blk], None)])
            chains += _group_chains(sink_ref, q_ref, rows, BLOCK, kvh, ksets)
            groups.append((rows, BLOCK, kvh))
    t0 = pl.multiple_of(pl.program_id(1) * (sub * BLOCK), sub * BLOCK)
    fillers = _conv_fillers(pad_ref, (cw_ref, cb_ref, lg_ref, lb_ref), cv_ref, t0, sub * BLOCK)
    _run_chains(chains, _make_finish(groups, g_ref, o_ref), fillers, ahead)


def _band_bias():
    i = np.arange(BLOCK)[:, None]
    k = np.arange(3 * BLOCK)[None, :]
    tabs = [np.where(np.abs(BLOCK * qb + i - k) <= WINDOW, 0.0, NEG_INF) for qb in range(3)]
    return jnp.asarray(np.stack(tabs).astype(np.float32))


def _local_attention(sink, q, kv, kvc, gatt, uc, conv_args):
    bn, s, _ = q.shape
    n_ctx = kvc[1].shape[1]
    sub = 8
    ahead = 4
    tq = sub * BLOCK
    assert s % tq == 0 and s >= 3 * BLOCK
    tile = lambda w: pl.BlockSpec((1, tq, w), lambda b, n: (b, n, 0))
    whole = lambda rows, w: pl.BlockSpec((1, rows, w), lambda b, n: (b, 0, 0))
    kv_specs = lambda rows: [whole(KVX_W, rows), whole(rows, KVX_W)]
    return pl.pallas_call(
        functools.partial(_local_attn_kernel, seq=s, sub=sub, ahead=ahead),
        grid=(bn, s // tq),
        in_specs=[
            pl.BlockSpec(memory_space=pltpu.SMEM),
            tile(ATT_W), *kv_specs(s), *kv_specs(n_ctx),
            pl.BlockSpec((3, BLOCK, 3 * BLOCK), lambda b, n: (0, 0, 0)),
            tile(ATT_W), whole(s, CONV_W),
        ] + _conv_specs(lambda b, n: (0, 0)),
        out_specs=[tile(ATT_W), tile(CONV_W)],
        out_shape=[jax.ShapeDtypeStruct((bn, s, ATT_W), BF16), jax.ShapeDtypeStruct((bn, s, CONV_W), BF16)],
        scratch_shapes=[pltpu.VMEM((s + 2 * CONV_HALO, CONV_W), F32)],
        compiler_params=SEQUENTIAL,
        name="local_attention",
    )(sink, q, *kv, *kvc, _band_bias(), gatt, uc, *conv_args)


def _ctx_attn_kernel(sink_ref, q_ref, ktc_ref, vc_ref, g_ref, uc_ref, cw_ref, cb_ref, lg_ref, lb_ref,
                     o_ref, cv_ref, pad_ref, *, n_ctx, ahead):
    _fill_conv_pad(pad_ref, uc_ref, n_ctx)
    rows = slice(0, n_ctx)
    chains, groups = [], []
    for kvh in range(N_KV_HEADS):
        ksets = []
        for e in range(2):
            blk = _kv_block(kvh, e)
            ksets.append([(ktc_ref[0, blk, :], vc_ref[0, :, blk], None)])
        chains += _group_chains(sink_ref, q_ref, rows, n_ctx, kvh, ksets)
        groups.append((rows, n_ctx, kvh))
    fillers = _conv_fillers(pad_ref, (cw_ref, cb_ref, lg_ref, lb_ref), cv_ref, 0, n_ctx)
    _run_chains(chains, _make_finish(groups, g_ref, o_ref), fillers, ahead)


def _ctx_attention(sink, q, kvc, gatt, uc, conv_args):
    bn, n_ctx, _ = q.shape
    ahead = 2
    whole = lambda w: pl.BlockSpec((1, n_ctx, w), lambda b: (b, 0, 0))
    return pl.pallas_call(
        functools.partial(_ctx_attn_kernel, n_ctx=n_ctx, ahead=ahead),
        grid=(bn,),
        in_specs=[pl.BlockSpec(memory_space=pltpu.SMEM), whole(ATT_W),
                  pl.BlockSpec((1, KVX_W, n_ctx), lambda b: (b, 0, 0)), whole(KVX_W), whole(ATT_W),
                  whole(CONV_W)] + _conv_specs(lambda b: (0, 0)),
        out_specs=[whole(ATT_W), whole(CONV_W)],
        out_shape=[jax.ShapeDtypeStruct((bn, n_ctx, ATT_W), BF16),
                   jax.ShapeDtypeStruct((bn, n_ctx, CONV_W), BF16)],
        scratch_shapes=[pltpu.VMEM((n_ctx + 2 * CONV_HALO, CONV_W), F32)],
        name="ctx_attention",
    )(sink, q, *kvc, gatt, uc, *conv_args)


def _fourier_prepare(ab_ref, cw_ref, sw_ref, rc_ref, rs_ref, half):
    top = ab_ref[0, 0:half, :].astype(F32)
    bot = ab_ref[0, half:2 * half, :].astype(F32)
    plus = top + bot
    minus = top - bot
    a_m, b_m = minus[:, :FOUR_W], minus[:, FOUR_W:]
    cw, sw = cw_ref[...], sw_ref[...]
    rc_ref[:, 0:FOUR_W] = plus[:, :FOUR_W].astype(BF16)
    rc_ref[:, FOUR_W:] = (cw * a_m - sw * b_m).astype(BF16)
    rs_ref[:, 0:FOUR_W] = plus[:, FOUR_W:].astype(BF16)
    rs_ref[:, FOUR_W:] = (sw * a_m + cw * b_m).astype(BF16)


def _fourier_rows(c_ref, s_ref, rc_ref, rs_ref, wf_ref, bf_ref, il_ref, r0, chunk):
    e = _dot(c_ref[pl.ds(r0, chunk), :], rc_ref[...]) - _dot(s_ref[pl.ds(r0, chunk), :], rs_ref[...])
    eb = e.astype(BF16)
    for p in range(2):
        f = _dot(eb[:, p * FOUR_W:(p + 1) * FOUR_W], wf_ref[...])
        for lt in range(FOUR_W // LANES):
            il_ref[lt, pl.ds(p, chunk, stride=2), :] = f[:, lt * LANES:(lt + 1) * LANES]
    return jnp.concatenate([il_ref[lt] for lt in range(FOUR_W // LANES)], axis=1) + bf_ref[...]


def _out_kernel(x_ref, att_ref, cv_ref, gcv_ref, ab_ref, gf_ref, c_ref, s_ref, cw_ref, sw_ref, wf_ref, bf_ref,
                w_ref, gate_ref, fg_ref, o_ref, rc_ref, rs_ref, il_ref, wbf_ref, *, tm, half, final):
    _cast_weight_once(w_ref, wbf_ref)

    @pl.when(pl.program_id(1) == 0)
    def _():
        _fourier_prepare(ab_ref, cw_ref, sw_ref, rc_ref, rs_ref, half)

    r0 = pl.multiple_of(pl.program_id(1) * (tm // 2), tm // 2)
    four = _fourier_rows(c_ref, s_ref, rc_ref, rs_ref, wf_ref, bf_ref, il_ref, r0, tm // 2)
    fo = (four * gf_ref[0].astype(F32)).astype(BF16)
    cv = cv_ref[0] * gcv_ref[0]
    y = _dot(jnp.concatenate([att_ref[0], cv, fo], axis=1), wbf_ref[...])
    xn = x_ref[0] + gate_ref[0] * y
    if final:
        ms = jnp.mean(xn * xn, axis=-1, keepdims=True)
        xn = xn * lax.rsqrt(ms + EPS) * fg_ref[...]
    o_ref[0] = xn


def _out_proj(x, att, cv, gcv, ab, gf, pos_tabs, w_four_bf, b_four, w_out, mod3, final_g,
              *, tm, mod_row, final, layer):
    bn, s, _ = x.shape
    half = s // 2
    c_half, s_half, cw, sw = pos_tabs
    tile = lambda w: pl.BlockSpec((1, tm, w), lambda b, i: (b, i, 0))
    const = lambda shape: pl.BlockSpec(shape, lambda b, i: (0,) * len(shape))
    return pl.pallas_call(
        functools.partial(_out_kernel, tm=tm, half=half, final=final),
        grid=(bn, s // tm),
        in_specs=[
            tile(D_MODEL), tile(ATT_W), tile(CONV_W), tile(CONV_W),
            pl.BlockSpec((1, s, 2 * FOUR_W), lambda b, i: (b, 0, 0)),
            tile(FOUR_W),
            const((half, half)), const((half, half)), const((half, FOUR_W)), const((half, FOUR_W)),
            const((FOUR_W, FOUR_W)), const((1, FOUR_W)),
            pl.BlockSpec((1, D_MODEL, D_MODEL), lambda b, i: (layer, 0, 0)),
            _mod_spec(mod_row, 2),
            const((1, D_MODEL)),
        ],
        out_specs=tile(D_MODEL),
        out_shape=jax.ShapeDtypeStruct((bn, s, D_MODEL), F32),
        scratch_shapes=[pltpu.VMEM((half, 2 * FOUR_W), BF16), pltpu.VMEM((half, 2 * FOUR_W), BF16),
                        pltpu.VMEM((FOUR_W // LANES, tm, LANES), F32),
                        pltpu.VMEM((D_MODEL, D_MODEL), BF16)],
        compiler_params=SEQUENTIAL,
        name="out_proj",
    )(x, att, cv, gcv, ab, gf, c_half, s_half, cw, sw, w_four_bf, b_four.reshape(1, FOUR_W),
      w_out, mod3, final_g.reshape(1, D_MODEL))


def _rope_tables(s):
    t = np.arange(s)
    half = HEAD_DIM // 4
    freqs = ROPE_BASE ** (-np.arange(half) / half)
    zero = np.zeros((s, half))

    def per_head(fn_lo, fn_hi):
        parts = []
        for pos in (t // GRID_W, t % GRID_W):
            ang = pos[:, None] * freqs[None, :]
            parts += [fn_lo(ang), fn_hi(ang)]
        return np.tile(np.concatenate(parts, axis=-1), (1, LANES // HEAD_DIM))

    cos = per_head(np.cos, np.cos)
    sin_lo = per_head(lambda a: -np.sin(a), lambda a: zero)
    sin_hi = per_head(lambda a: zero, np.sin)
    tab = np.concatenate([cos * Q_SCALE, sin_lo * Q_SCALE, sin_hi * Q_SCALE, cos, sin_lo, sin_hi], axis=-1)
    return jnp.asarray(tab.astype(np.float32))


def _identity_rope_tables(s):
    one = np.ones((s, LANES))
    zero = np.zeros((s, LANES))
    return jnp.asarray(np.concatenate([one * Q_SCALE, zero, zero, one, zero, zero], axis=-1).astype(np.float32))


def _channel_dft(n_pos):
    c = np.arange(FOUR_HEAD_DIM)
    ang = 2.0 * np.pi * ((c[:, None] * c[None, :]) % FOUR_HEAD_DIM) / FOUR_HEAD_DIM
    eye = np.eye(FOUR_HEADS)
    scale = 1.0 / np.sqrt(float(n_pos * FOUR_HEAD_DIM))
    tab = np.concatenate([np.kron(eye, np.cos(ang)), np.kron(eye, np.sin(ang))], axis=1) * scale
    return jnp.asarray(tab.astype(np.float32))


def _position_dft(n_pos):
    half = n_pos // 2
    k = np.arange(half)
    ang = 2.0 * np.pi * ((k[:, None] * k[None, :]) % half) / half
    beta = np.pi * k / half
    ones = np.ones((1, FOUR_W))
    f32 = lambda a: jnp.asarray(a.astype(np.float32))
    return (f32(np.cos(ang)).astype(BF16), f32(np.sin(ang)).astype(BF16),
            f32(np.cos(beta)[:, None] * ones), f32(np.sin(beta)[:, None] * ones))


def kernel(x, c, ctx, c_ctx, w_ada, b_ada, norm_g, w_in, attn_sink, conv_w, conv_b,
           conv_ln_g, conv_ln_b, w_four, b_four, w_out, final_g):
    bn, s, _ = x.shape
    n_ctx = ctx.shape[1]
    assert bn < MOD_ROWS
    ctx_row = bn

    cc = jnp.zeros((MOD_ROWS, D_MODEL), F32).at[:bn].set(c).at[ctx_row].set(c_ctx)
    mod = _modulation(cc, w_ada, b_ada)

    w_four_bf = w_four.astype(BF16)
    rope_x = _rope_tables(s)
    rope_c = _identity_rope_tables(n_ctx)
    dft_x = _channel_dft(s).astype(BF16)
    dft_ctx = _channel_dft(n_ctx).astype(BF16)
    pos_x = _position_dft(s)
    pos_ctx = _position_dft(n_ctx)

    h_ctx = ctx
    for l in range(DEPTH):
        mod3 = mod[l].reshape(MOD_ROWS, 1, 3 * D_MODEL)
        conv_args = _conv_args(conv_w[l], conv_b[l], conv_ln_g[l], conv_ln_b[l])
        four_args = (w_four_bf[l], b_four[l], w_out, mod3, final_g)
        sink = attn_sink[l]
        last = l == DEPTH - 1
        if not last:
            qc, ktc, vc, gatt_c, gcv_c, ab_c, gf_c, uc_c = _in_proj(
                h_ctx, mod3, norm_g[l], w_in, rope_c, dft_ctx, tm=n_ctx, mod_row=ctx_row, layer=l)
            kvc = (ktc, vc)
            att_c, cv_c = _ctx_attention(sink, qc, kvc, gatt_c, uc_c, conv_args)
        else:
            kvc = _kv_proj(h_ctx, mod3, norm_g[l], w_in, tm=n_ctx, mod_row=ctx_row, layer=l)
        q, kt, v, gatt, gcv, ab, gf, uc = _in_proj(
            x, mod3, norm_g[l], w_in, rope_x, dft_x, tm=512, mod_row=None, layer=l)
        att, cv = _local_attention(sink, q, (kt, v), kvc, gatt, uc, conv_args)
        if not last:
            h_ctx = _out_proj(h_ctx, att_c, cv_c, gcv_c, ab_c, gf_c, pos_ctx, *four_args,
                              tm=n_ctx, mod_row=ctx_row, final=False, layer=l)
        x = _out_proj(x, att, cv, gcv, ab, gf, pos_x, *four_args, tm=1024, mod_row=None, final=last, layer=l)
    return x
```

```python
import functools

import numpy as np
import jax
import jax.numpy as jnp
from jax import lax
from jax.experimental import pallas as pl
from jax.experimental.pallas import tpu as pltpu

F32 = jnp.float32
BF16 = jnp.bfloat16

D_MODEL = 1024
DEPTH = 2
GRID_W = 64
HEAD_DIM = 64
ATT_W = 512
N_HEADS = 8
N_KV_HEADS = 2
GQA_GROUP = N_HEADS // N_KV_HEADS
KV_W = N_KV_HEADS * HEAD_DIM
CONV_W = 256
FOUR_W = 256
FOUR_HEADS = 4
FOUR_HEAD_DIM = FOUR_W // FOUR_HEADS
CONV_K = 31
SUBLANES = 8
CONV_HALO = 16
WINDOW = 128
BLOCK = 128
ROPE_BASE = 10000.0
EPS = 1e-6
NEG_INF = -1e30
LOG2E = 1.4426950408889634
Q_SCALE = HEAD_DIM ** -0.5 * LOG2E
IN_W = 2 * ATT_W + 2 * KV_W + 3 * CONV_W + 2 * FOUR_W
OFF_Q = 0
OFF_KV = ATT_W
OFF_GATT = OFF_KV + 2 * KV_W
OFF_CONV = OFF_GATT + ATT_W
OFF_GCONV = OFF_CONV + 2 * CONV_W
OFF_FOUR = OFF_GCONV + CONV_W
OFF_GFOUR = OFF_FOUR + FOUR_W
LANES = 128
KVX_W = 4 * LANES
MOD_ROWS = 16
SEQUENTIAL = pltpu.CompilerParams(dimension_semantics=("arbitrary", "arbitrary"))


def _dot(a, b):
    return jnp.dot(a, b, preferred_element_type=F32)


def _silu(x):
    return x * jax.nn.sigmoid(x)


def _ada_kernel(cc_ref, w_ref, b_ref, o_ref):
    a = _silu(cc_ref[...])
    w = w_ref[0]
    a_hi = a.astype(BF16)
    a_lo = (a - a_hi.astype(F32)).astype(BF16)
    w_hi = w.astype(BF16)
    o_ref[0] = _dot(a_hi, w_hi) + _dot(a_lo, w_hi) + b_ref[0]


def _modulation(cc, w_ada, b_ada):
    tn = 1536
    return pl.pallas_call(
        _ada_kernel,
        grid=(DEPTH, 3 * D_MODEL // tn),
        in_specs=[
            pl.BlockSpec((MOD_ROWS, D_MODEL), lambda l, j: (0, 0)),
            pl.BlockSpec((1, D_MODEL, tn), lambda l, j: (l, 0, j)),
            pl.BlockSpec((1, 1, tn), lambda l, j: (l, 0, j)),
        ],
        out_specs=pl.BlockSpec((1, MOD_ROWS, tn), lambda l, j: (l, 0, j)),
        out_shape=jax.ShapeDtypeStruct((DEPTH, MOD_ROWS, 3 * D_MODEL), F32),
        name="ada_modulation",
    )(cc, w_ada, b_ada.reshape(DEPTH, 1, 3 * D_MODEL))


def _zero_after(token):
    bits = pltpu.bitcast(jnp.broadcast_to(token[:SUBLANES], (SUBLANES, LANES)), jnp.uint32)
    zero = lax.shift_right_logical(lax.shift_right_logical(bits, jnp.uint32(16)), jnp.uint32(16))
    return pltpu.bitcast(zero, F32)[:1]


def _conv_tile(pad_ref, w_ref, t0, lt, zero=None):
    lead = CONV_HALO - CONV_K // 2
    lanes = slice(lt * LANES, (lt + 1) * LANES)
    acc = None
    for r in range(SUBLANES):
        z = None
        for a in range((CONV_K - r + SUBLANES - 1) // SUBLANES):
            j = SUBLANES * a + r
            w = w_ref[j:j + 1, lanes]
            term = pad_ref[pl.ds(t0 + SUBLANES * a, BLOCK + SUBLANES), lanes] * (w if zero is None else w + zero)
            z = term if z is None else z + term
        zs = z[r + lead:r + lead + BLOCK]
        acc = zs if acc is None else acc + zs
    return acc


def _conv_finish(tiles, b_ref, lg_ref, lb_ref):
    y = jnp.concatenate(tiles, axis=1) + b_ref[...]
    mu = jnp.mean(y, axis=-1, keepdims=True)
    yc = y - mu
    var = jnp.mean(yc * yc, axis=-1, keepdims=True)
    return _silu(yc * lax.rsqrt(var + EPS) * lg_ref[...] + lb_ref[...]).astype(BF16)


def _fill_conv_pad(pad_ref, u_ref, seq):
    zeros = jnp.zeros((CONV_HALO, CONV_W), F32)
    pad_ref[0:CONV_HALO, :] = zeros
    pad_ref[CONV_HALO + seq:2 * CONV_HALO + seq, :] = zeros
    pad_ref[CONV_HALO:CONV_HALO + seq, :] = u_ref[0].astype(F32)


def _conv_fillers(pad_ref, conv_refs, cv_ref, t0, n_rows):
    w_ref, b_ref, lg_ref, lb_ref = conv_refs
    pieces = []
    for j in range(n_rows // BLOCK):
        tiles = []
        for lt in range(CONV_W // LANES):
            pieces.append(lambda token, j=j, lt=lt, tiles=tiles: tiles.append(
                _conv_tile(pad_ref, w_ref, t0 + j * BLOCK, lt, _zero_after(token))))

        def finish(token, j=j, tiles=tiles):
            cv_ref[0, j * BLOCK:(j + 1) * BLOCK, :] = _conv_finish(tiles, b_ref, lg_ref, lb_ref)
        pieces.append(finish)
    return pieces


def _conv_specs(index):
    vec = pl.BlockSpec((1, CONV_W), index)
    return [pl.BlockSpec((CONV_K, CONV_W), index), vec, vec, vec]


def _conv_args(conv_w, conv_b, ln_g, ln_b):
    row = lambda a: a.reshape(1, CONV_W)
    return conv_w, row(conv_b), row(ln_g), row(ln_b)


def _cast_weight_once(w_ref, wbf_ref):
    @pl.when((pl.program_id(0) == 0) & (pl.program_id(1) == 0))
    def _():
        wbf_ref[...] = w_ref[0].astype(BF16)


def _project(x_ref, sh_ref, sc_ref, g_ref, w_ref, wbf_ref):
    _cast_weight_once(w_ref, wbf_ref)
    x = x_ref[0]
    ms = jnp.mean(x * x, axis=-1, keepdims=True)
    gain = g_ref[...] * (1.0 + sc_ref[0])
    h = x * lax.rsqrt(ms + EPS) * gain + sh_ref[0]
    return _dot(h.astype(BF16), wbf_ref[...])


def _rope_block(blk, rope_ref, base):
    cos = rope_ref[:, base:base + LANES]
    sin_lo = rope_ref[:, base + LANES:base + 2 * LANES]
    sin_hi = rope_ref[:, base + 2 * LANES:base + 3 * LANES]
    return (blk * cos + pltpu.roll(blk, LANES - 16, 1) * sin_lo + pltpu.roll(blk, 16, 1) * sin_hi)


def _store_padded_kv(kt_ref, v_ref, kblk, vblk):
    tm = kblk.shape[0]
    kt = kblk.T.astype(BF16)
    zeros = jnp.zeros((HEAD_DIM, tm), BF16)
    k_blocks = ((kt[:HEAD_DIM], zeros), (zeros, kt[:HEAD_DIM]), (kt[HEAD_DIM:], zeros), (zeros, kt[HEAD_DIM:]))
    for i, halves in enumerate(k_blocks):
        kt_ref[0, i * LANES:(i + 1) * LANES, :] = jnp.concatenate(halves, axis=0)
    lo = lax.broadcasted_iota(jnp.int32, vblk.shape, 1) < HEAD_DIM
    swapped = pltpu.roll(vblk, HEAD_DIM, 1)
    v_blocks = (jnp.where(lo, vblk, 0.0), jnp.where(lo, 0.0, swapped),
                jnp.where(lo, swapped, 0.0), jnp.where(lo, 0.0, vblk))
    for i, b in enumerate(v_blocks):
        v_ref[0, :, i * LANES:(i + 1) * LANES] = b.astype(BF16)


def _project_epilogue(p, rope_ref, dft_ref, q_ref, kt_ref, v_ref, gatt_ref, gcv_ref, ab_ref, gf_ref):
    for c in range(ATT_W // LANES):
        blk = _rope_block(p[:, OFF_Q + c * LANES:OFF_Q + (c + 1) * LANES], rope_ref, 0)
        q_ref[0, :, c * LANES:(c + 1) * LANES] = blk.astype(BF16)
    _store_padded_kv(kt_ref, v_ref, _rope_block(p[:, OFF_KV:OFF_KV + KV_W], rope_ref, 3 * LANES),
                     p[:, OFF_KV + KV_W:OFF_KV + 2 * KV_W])
    gatt_ref[0] = _silu(p[:, OFF_GATT:OFF_GATT + ATT_W]).astype(BF16)
    gcv_ref[0] = _silu(p[:, OFF_GCONV:OFF_GCONV + CONV_W]).astype(BF16)
    fu = p[:, OFF_FOUR:OFF_FOUR + FOUR_W].astype(BF16)
    ab_ref[0] = _dot(fu, dft_ref[...]).astype(BF16)
    gf_ref[0] = _silu(p[:, OFF_GFOUR:OFF_GFOUR + FOUR_W]).astype(BF16)
    return (p[:, OFF_CONV:OFF_CONV + CONV_W] * jax.nn.sigmoid(p[:, OFF_CONV + CONV_W:OFF_CONV + 2 * CONV_W])
            ).astype(BF16)


def _in_kernel(x_ref, sh_ref, sc_ref, g_ref, w_ref, rope_ref, dft_ref,
               q_ref, kt_ref, v_ref, gatt_ref, gcv_ref, ab_ref, gf_ref, uc_ref, wbf_ref):
    p = _project(x_ref, sh_ref, sc_ref, g_ref, w_ref, wbf_ref)
    uc_ref[0] = _project_epilogue(p, rope_ref, dft_ref, q_ref, kt_ref, v_ref, gatt_ref, gcv_ref, ab_ref, gf_ref)


def _mod_spec(mod_row, j):
    row = (lambda b: mod_row) if mod_row is not None else (lambda b: b)
    return pl.BlockSpec((1, 1, D_MODEL), lambda b, i: (row(b), 0, j))


def _in_specs(tm, mod_row, layer):
    return [
        pl.BlockSpec((1, tm, D_MODEL), lambda b, i: (b, i, 0)),
        _mod_spec(mod_row, 0), _mod_spec(mod_row, 1),
        pl.BlockSpec((1, D_MODEL), lambda b, i: (0, 0)),
        pl.BlockSpec((1, D_MODEL, IN_W), lambda b, i: (layer, 0, 0)),
        pl.BlockSpec((tm, 6 * LANES), lambda b, i: (i, 0)),
        pl.BlockSpec((FOUR_W, 2 * FOUR_W), lambda b, i: (0, 0)),
    ]


def _in_outs(bn, s, tm):
    widths = (ATT_W, None, KVX_W, ATT_W, CONV_W, 2 * FOUR_W, FOUR_W, CONV_W)
    specs = [pl.BlockSpec((1, tm, w), lambda b, i: (b, i, 0)) if w else
             pl.BlockSpec((1, KVX_W, tm), lambda b, i: (b, 0, i)) for w in widths]
    shapes = [jax.ShapeDtypeStruct((bn, s, w) if w else (bn, KVX_W, s), BF16) for w in widths]
    return specs, shapes


def _in_proj(x, mod3, norm_g, w, rope_tab, dft_c, *, tm, mod_row, layer):
    bn, s, _ = x.shape
    specs, shapes = _in_outs(bn, s, tm)
    return pl.pallas_call(
        _in_kernel,
        grid=(bn, s // tm),
        in_specs=_in_specs(tm, mod_row, layer),
        out_specs=specs,
        out_shape=shapes,
        scratch_shapes=[pltpu.VMEM((D_MODEL, IN_W), BF16)],
        compiler_params=SEQUENTIAL,
        name="in_proj",
    )(x, mod3, mod3, norm_g.reshape(1, D_MODEL), w, rope_tab, dft_c)


def _kv_kernel(x_ref, sh_ref, sc_ref, g_ref, w_ref, kt_ref, v_ref, wbf_ref):
    kvp = _project(x_ref, sh_ref, sc_ref, g_ref, w_ref, wbf_ref)
    _store_padded_kv(kt_ref, v_ref, kvp[:, 0:KV_W], kvp[:, KV_W:2 * KV_W])


def _kv_proj(x, mod3, norm_g, w, *, tm, mod_row, layer):
    bn, s, _ = x.shape
    assert OFF_KV % (2 * KV_W) == 0
    return pl.pallas_call(
        _kv_kernel,
        grid=(bn, s // tm),
        in_specs=[
            pl.BlockSpec((1, tm, D_MODEL), lambda b, i: (b, i, 0)),
            _mod_spec(mod_row, 0), _mod_spec(mod_row, 1),
            pl.BlockSpec((1, D_MODEL), lambda b, i: (0, 0)),
            pl.BlockSpec((1, D_MODEL, 2 * KV_W), lambda b, i: (layer, 0, OFF_KV // (2 * KV_W))),
        ],
        out_specs=[pl.BlockSpec((1, KVX_W, tm), lambda b, i: (b, 0, i)),
                   pl.BlockSpec((1, tm, KVX_W), lambda b, i: (b, i, 0))],
        out_shape=[jax.ShapeDtypeStruct((bn, KVX_W, s), BF16), jax.ShapeDtypeStruct((bn, s, KVX_W), BF16)],
        scratch_shapes=[pltpu.VMEM((D_MODEL, 2 * KV_W), BF16)],
        compiler_params=SEQUENTIAL,
        name="ctx_kv_proj",
    )(x, mod3, mod3, norm_g.reshape(1, D_MODEL), w)


class _Chain:
    def __init__(self, q2, sink_col, kset):
        self.q2, self.sink_col, self.kset = q2, sink_col, kset

    def scores(self):
        self.s = []
        for kt, _, bias in self.kset:
            sc = _dot(self.q2, kt)
            self.s.append(sc if bias is None else sc + bias)

    def softmax(self):
        m = self.sink_col
        for sc in self.s:
            m = jnp.maximum(m, sc.max(axis=-1, keepdims=True))
        self.p = [jnp.exp2(sc - m).astype(BF16) for sc in self.s]
        self.extra = jnp.exp2(self.sink_col - m)
        self.row_max = m
        self.s = None

    def values(self):
        r = None
        for p, (_, v, _) in zip(self.p, self.kset):
            pv = _dot(p, jnp.concatenate([v, jnp.ones_like(v)], axis=1))
            r = pv if r is None else r + pv
        self.p = None
        return r[:, :LANES] / (r[:, LANES:] + self.extra)


def _run_chains(chains, finish, fillers, ahead):
    n = len(chains)
    fillers = list(fillers)
    for i in range(-ahead, n):
        if 0 <= i < n:
            finish(i, chains[i].values())
        if 0 <= i + ahead < n:
            chains[i + ahead].scores()
        if 0 <= i + 1 < n:
            chains[i + 1].softmax()
            if fillers:
                fillers.pop(0)(chains[i + 1].row_max)
    for f in fillers:
        f(chains[-1].row_max)


def _kv_block(kvh, e):
    k0 = (2 * kvh + e) * LANES
    return slice(k0, k0 + LANES)


def _group_chains(sink_ref, q_ref, rows, tq, kvh, ksets):
    c0 = 2 * kvh * LANES
    q2 = jnp.concatenate([q_ref[0, rows, c0:c0 + LANES], q_ref[0, rows, c0 + LANES:c0 + 2 * LANES]], axis=0)
    chains = []
    for e in range(2):
        sink_col = jnp.concatenate([jnp.full((tq, 1), sink_ref[GQA_GROUP * kvh + e] * LOG2E, F32),
                                    jnp.full((tq, 1), sink_ref[GQA_GROUP * kvh + 2 + e] * LOG2E, F32)], axis=0)
        chains.append(_Chain(q2, sink_col, ksets[e]))
    return chains


def _make_finish(groups, g_ref, o_ref):
    partial = {}

    def finish(i, out):
        if i % 2 == 0:
            partial[i // 2] = out
            return
        rows, tq, kvh = groups[i // 2]
        o = partial.pop(i // 2) + out
        c0 = 2 * kvh * LANES
        for hf in range(2):
            cols = slice(c0 + hf * LANES, c0 + (hf + 1) * LANES)
            g = g_ref[0, rows, cols].astype(F32)
            o_ref[0, rows, cols] = (o[hf * tq:(hf + 1) * tq] * g).astype(BF16)
    return finish


def _local_attn_kernel(sink_ref, q_ref, kt_ref, v_ref, ktc_ref, vc_ref, bias_ref, g_ref, uc_ref,
                       cw_ref, cb_ref, lg_ref, lb_ref, o_ref, cv_ref, pad_ref, *, seq, sub, ahead):
    @pl.when(pl.program_id(1) == 0)
    def _():
        _fill_conv_pad(pad_ref, uc_ref, seq)

    nblk = seq // BLOCK
    chains, groups = [], []
    for sb in range(sub):
        n = pl.program_id(1) * sub + sb
        start = pl.multiple_of(jnp.clip((n - 1) * BLOCK, 0, seq - 3 * BLOCK), BLOCK)
        win = pl.ds(start, 3 * BLOCK)
        bias = bias_ref[jnp.where(n == 0, 0, jnp.where(n == nblk - 1, 2, 1))]
        bias2 = jnp.concatenate([bias, bias], axis=0)
        rows = slice(sb * BLOCK, (sb + 1) * BLOCK)
        for kvh in range(N_KV_HEADS):
            ksets = []
            for e in range(2):
                blk = _kv_block(kvh, e)
                ksets.append([(kt_ref[0, blk, win], v_ref[0, win, blk], bias2),
                              (ktc_ref[0, blk, :], vc_ref[0, :, blk], None)])
            chains += _group_chains(sink_ref, q_ref, rows, BLOCK, kvh, ksets)
            groups.append((rows, BLOCK, kvh))
    t0 = pl.multiple_of(pl.program_id(1) * (sub * BLOCK), sub * BLOCK)
    fillers = _conv_fillers(pad_ref, (cw_ref, cb_ref, lg_ref, lb_ref), cv_ref, t0, sub * BLOCK)
    _run_chains(chains, _make_finish(groups, g_ref, o_ref), fillers, ahead)


def _band_bias():
    i = np.arange(BLOCK)[:, None]
    k = np.arange(3 * BLOCK)[None, :]
    tabs = [np.where(np.abs(BLOCK * qb + i - k) <= WINDOW, 0.0, NEG_INF) for qb in range(3)]
    return jnp.asarray(np.stack(tabs).astype(np.float32))


def _local_attention(sink, q, kv, kvc, gatt, uc, conv_args):
    bn, s, _ = q.shape
    n_ctx = kvc[1].shape[1]
    sub = 8
    ahead = 8
    tq = sub * BLOCK
    assert s % tq == 0 and s >= 3 * BLOCK
    tile = lambda w: pl.BlockSpec((1, tq, w), lambda b, n: (b, n, 0))
    whole = lambda rows, w: pl.BlockSpec((1, rows, w), lambda b, n: (b, 0, 0))
    kv_specs = lambda rows: [whole(KVX_W, rows), whole(rows, KVX_W)]
    return pl.pallas_call(
        functools.partial(_local_attn_kernel, seq=s, sub=sub, ahead=ahead),
        grid=(bn, s // tq),
        in_specs=[
            pl.BlockSpec(memory_space=pltpu.SMEM),
            tile(ATT_W), *kv_specs(s), *kv_specs(n_ctx),
            pl.BlockSpec((3, BLOCK, 3 * BLOCK), lambda b, n: (0, 0, 0)),
            tile(ATT_W), whole(s, CONV_W),
        ] + _conv_specs(lambda b, n: (0, 0)),
        out_specs=[tile(ATT_W), tile(CONV_W)],
        out_shape=[jax.ShapeDtypeStruct((bn, s, ATT_W), BF16), jax.ShapeDtypeStruct((bn, s, CONV_W), BF16)],
        scratch_shapes=[pltpu.VMEM((s + 2 * CONV_HALO, CONV_W), F32)],
        compiler_params=SEQUENTIAL,
        name="local_attention",
    )(sink, q, *kv, *kvc, _band_bias(), gatt, uc, *conv_args)


def _ctx_attn_kernel(sink_ref, q_ref, ktc_ref, vc_ref, g_ref, uc_ref, cw_ref, cb_ref, lg_ref, lb_ref,
                     o_ref, cv_ref, pad_ref, *, n_ctx, ahead):
    _fill_conv_pad(pad_ref, uc_ref, n_ctx)
    rows = slice(0, n_ctx)
    chains, groups = [], []
    for kvh in range(N_KV_HEADS):
        ksets = []
        for e in range(2):
            blk = _kv_block(kvh, e)
            ksets.append([(ktc_ref[0, blk, :], vc_ref[0, :, blk], None)])
        chains += _group_chains(sink_ref, q_ref, rows, n_ctx, kvh, ksets)
        groups.append((rows, n_ctx, kvh))
    fillers = _conv_fillers(pad_ref, (cw_ref, cb_ref, lg_ref, lb_ref), cv_ref, 0, n_ctx)
    _run_chains(chains, _make_finish(groups, g_ref, o_ref), fillers, ahead)


def _ctx_attention(sink, q, kvc, gatt, uc, conv_args):
    bn, n_ctx, _ = q.shape
    ahead = 2
    whole = lambda w: pl.BlockSpec((1, n_ctx, w), lambda b: (b, 0, 0))
    return pl.pallas_call(
        functools.partial(_ctx_attn_kernel, n_ctx=n_ctx, ahead=ahead),
        grid=(bn,),
        in_specs=[pl.BlockSpec(memory_space=pltpu.SMEM), whole(ATT_W),
                  pl.BlockSpec((1, KVX_W, n_ctx), lambda b: (b, 0, 0)), whole(KVX_W), whole(ATT_W),
                  whole(CONV_W)] + _conv_specs(lambda b: (0, 0)),
        out_specs=[whole(ATT_W), whole(CONV_W)],
        out_shape=[jax.ShapeDtypeStruct((bn, n_ctx, ATT_W), BF16),
                   jax.ShapeDtypeStruct((bn, n_ctx, CONV_W), BF16)],
        scratch_shapes=[pltpu.VMEM((n_ctx + 2 * CONV_HALO, CONV_W), F32)],
        name="ctx_attention",
    )(sink, q, *kvc, gatt, uc, *conv_args)


def _fourier_prepare(ab_ref, cw_ref, sw_ref, rc_ref, rs_ref, half):
    top = ab_ref[0, 0:half, :].astype(F32)
    bot = ab_ref[0, half:2 * half, :].astype(F32)
    plus = top + bot
    minus = top - bot
    a_m, b_m = minus[:, :FOUR_W], minus[:, FOUR_W:]
    cw, sw = cw_ref[...], sw_ref[...]
    rc_ref[:, 0:FOUR_W] = plus[:, :FOUR_W].astype(BF16)
    rc_ref[:, FOUR_W:] = (cw * a_m - sw * b_m).astype(BF16)
    rs_ref[:, 0:FOUR_W] = plus[:, FOUR_W:].astype(BF16)
    rs_ref[:, FOUR_W:] = (sw * a_m + cw * b_m).astype(BF16)


def _fourier_rows(c_ref, s_ref, rc_ref, rs_ref, wf_ref, bf_ref, il_ref, r0, chunk):
    e = _dot(c_ref[pl.ds(r0, chunk), :], rc_ref[...]) - _dot(s_ref[pl.ds(r0, chunk), :], rs_ref[...])
    eb = e.astype(BF16)
    for p in range(2):
        f = _dot(eb[:, p * FOUR_W:(p + 1) * FOUR_W], wf_ref[...])
        for lt in range(FOUR_W // LANES):
            il_ref[lt, pl.ds(p, chunk, stride=2), :] = f[:, lt * LANES:(lt + 1) * LANES]
    return jnp.concatenate([il_ref[lt] for lt in range(FOUR_W // LANES)], axis=1) + bf_ref[...]


def _out_kernel(x_ref, att_ref, cv_ref, gcv_ref, ab_ref, gf_ref, c_ref, s_ref, cw_ref, sw_ref, wf_ref, bf_ref,
                w_ref, gate_ref, fg_ref, o_ref, rc_ref, rs_ref, il_ref, wbf_ref, *, tm, half, final):
    _cast_weight_once(w_ref, wbf_ref)

    @pl.when(pl.program_id(1) == 0)
    def _():
        _fourier_prepare(ab_ref, cw_ref, sw_ref, rc_ref, rs_ref, half)

    r0 = pl.multiple_of(pl.program_id(1) * (tm // 2), tm // 2)
    four = _fourier_rows(c_ref, s_ref, rc_ref, rs_ref, wf_ref, bf_ref, il_ref, r0, tm // 2)
    fo = (four * gf_ref[0].astype(F32)).astype(BF16)
    cv = cv_ref[0] * gcv_ref[0]
    y = _dot(jnp.concatenate([att_ref[0], cv, fo], axis=1), wbf_ref[...])
    xn = x_ref[0] + gate_ref[0] * y
    if final:
        ms = jnp.mean(xn * xn, axis=-1, keepdims=True)
        xn = xn * lax.rsqrt(ms + EPS) * fg_ref[...]
    o_ref[0] = xn


def _out_proj(x, att, cv, gcv, ab, gf, pos_tabs, w_four_bf, b_four, w_out, mod3, final_g,
              *, tm, mod_row, final, layer):
    bn, s, _ = x.shape
    half = s // 2
    c_half, s_half, cw, sw = pos_tabs
    tile = lambda w: pl.BlockSpec((1, tm, w), lambda b, i: (b, i, 0))
    const = lambda shape: pl.BlockSpec(shape, lambda b, i: (0,) * len(shape))
    return pl.pallas_call(
        functools.partial(_out_kernel, tm=tm, half=half, final=final),
        grid=(bn, s // tm),
        in_specs=[
            tile(D_MODEL), tile(ATT_W), tile(CONV_W), tile(CONV_W),
            pl.BlockSpec((1, s, 2 * FOUR_W), lambda b, i: (b, 0, 0)),
            tile(FOUR_W),
            const((half, half)), const((half, half)), const((half, FOUR_W)), const((half, FOUR_W)),
            const((FOUR_W, FOUR_W)), const((1, FOUR_W)),
            pl.BlockSpec((1, D_MODEL, D_MODEL), lambda b, i: (layer, 0, 0)),
            _mod_spec(mod_row, 2),
            const((1, D_MODEL)),
        ],
        out_specs=tile(D_MODEL),
        out_shape=jax.ShapeDtypeStruct((bn, s, D_MODEL), F32),
        scratch_shapes=[pltpu.VMEM((half, 2 * FOUR_W), BF16), pltpu.VMEM((half, 2 * FOUR_W), BF16),
                        pltpu.VMEM((FOUR_W // LANES, tm, LANES), F32),
                        pltpu.VMEM((D_MODEL, D_MODEL), BF16)],
        compiler_params=SEQUENTIAL,
        name="out_proj",
    )(x, att, cv, gcv, ab, gf, c_half, s_half, cw, sw, w_four_bf, b_four.reshape(1, FOUR_W),
      w_out, mod3, final_g.reshape(1, D_MODEL))


def _rope_tables(s):
    t = np.arange(s)
    half = HEAD_DIM // 4
    freqs = ROPE_BASE ** (-np.arange(half) / half)
    zero = np.zeros((s, half))

    def per_head(fn_lo, fn_hi):
        parts = []
        for pos in (t // GRID_W, t % GRID_W):
            ang = pos[:, None] * freqs[None, :]
            parts += [fn_lo(ang), fn_hi(ang)]
        return np.tile(np.concatenate(parts, axis=-1), (1, LANES // HEAD_DIM))

    cos = per_head(np.cos, np.cos)
    sin_lo = per_head(lambda a: -np.sin(a), lambda a: zero)
    sin_hi = per_head(lambda a: zero, np.sin)
    tab = np.concatenate([cos * Q_SCALE, sin_lo * Q_SCALE, sin_hi * Q_SCALE, cos, sin_lo, sin_hi], axis=-1)
    return jnp.asarray(tab.astype(np.float32))


def _identity_rope_tables(s):
    one = np.ones((s, LANES))
    zero = np.zeros((s, LANES))
    return jnp.asarray(np.concatenate([one * Q_SCALE, zero, zero, one, zero, zero], axis=-1).astype(np.float32))


def _channel_dft(n_pos):
    c = np.arange(FOUR_HEAD_DIM)
    ang = 2.0 * np.pi * ((c[:, None] * c[None, :]) % FOUR_HEAD_DIM) / FOUR_HEAD_DIM
    eye = np.eye(FOUR_HEADS)
    scale = 1.0 / np.sqrt(float(n_pos * FOUR_HEAD_DIM))
    tab = np.concatenate([np.kron(eye, np.cos(ang)), np.kron(eye, np.sin(ang))], axis=1) * scale
    return jnp.asarray(tab.astype(np.float32))


def _position_dft(n_pos):
    half = n_pos // 2
    k = np.arange(half)
    ang = 2.0 * np.pi * ((k[:, None] * k[None, :]) % half) / half
    beta = np.pi * k / half
    ones = np.ones((1, FOUR_W))
    f32 = lambda a: jnp.asarray(a.astype(np.float32))
    return (f32(np.cos(ang)).astype(BF16), f32(np.sin(ang)).astype(BF16),
            f32(np.cos(beta)[:, None] * ones), f32(np.sin(beta)[:, None] * ones))


def kernel(x, c, ctx, c_ctx, w_ada, b_ada, norm_g, w_in, attn_sink, conv_w, conv_b,
           conv_ln_g, conv_ln_b, w_four, b_four, w_out, final_g):
    bn, s, _ = x.shape
    n_ctx = ctx.shape[1]
    assert bn < MOD_ROWS
    ctx_row = bn

    cc = jnp.zeros((MOD_ROWS, D_MODEL), F32).at[:bn].set(c).at[ctx_row].set(c_ctx)
    mod = _modulation(cc, w_ada, b_ada)

    w_four_bf = w_four.astype(BF16)
    rope_x = _rope_tables(s)
    rope_c = _identity_rope_tables(n_ctx)
    dft_x = _channel_dft(s).astype(BF16)
    dft_ctx = _channel_dft(n_ctx).astype(BF16)
    pos_x = _position_dft(s)
    pos_ctx = _position_dft(n_ctx)

    h_ctx = ctx
    for l in range(DEPTH):
        mod3 = mod[l].reshape(MOD_ROWS, 1, 3 * D_MODEL)
        conv_args = _conv_args(conv_w[l], conv_b[l], conv_ln_g[l], conv_ln_b[l])
        four_args = (w_four_bf[l], b_four[l], w_out, mod3, final_g)
        sink = attn_sink[l]
        last = l == DEPTH - 1
        if not last:
            qc, ktc, vc, gatt_c, gcv_c, ab_c, gf_c, uc_c = _in_proj(
                h_ctx, mod3, norm_g[l], w_in, rope_c, dft_ctx, tm=n_ctx, mod_row=ctx_row, layer=l)
            kvc = (ktc, vc)
            att_c, cv_c = _ctx_attention(sink, qc, kvc, gatt_c, uc_c, conv_args)
        else:
            kvc = _kv_proj(h_ctx, mod3, norm_g[l], w_in, tm=n_ctx, mod_row=ctx_row, layer=l)
        q, kt, v, gatt, gcv, ab, gf, uc = _in_proj(
            x, mod3, norm_g[l], w_in, rope_x, dft_x, tm=512, mod_row=None, layer=l)
        att, cv = _local_attention(sink, q, (kt, v), kvc, gatt, uc, conv_args)
        if not last:
            h_ctx = _out_proj(h_ctx, att_c, cv_c, gcv_c, ab_c, gf_c, pos_ctx, *four_args,
                              tm=n_ctx, mod_row=ctx_row, final=False, layer=l)
        x = _out_proj(x, att, cv, gcv, ab, gf, pos_x, *four_args, tm=1024, mod_row=None, final=last, layer=l)
    return x
```

```python
import functools

import numpy as np
import jax
import jax.numpy as jnp
from jax import lax
from jax.experimental import pallas as pl
from jax.experimental.pallas import tpu as pltpu

F32 = jnp.float32
BF16 = jnp.bfloat16

D_MODEL = 1024
DEPTH = 2
GRID_W = 64
HEAD_DIM = 64
ATT_W = 512
N_HEADS = 8
N_KV_HEADS = 2
GQA_GROUP = N_HEADS // N_KV_HEADS
KV_W = N_KV_HEADS * HEAD_DIM
CONV_W = 256
FOUR_W = 256
FOUR_HEADS = 4
FOUR_HEAD_DIM = FOUR_W // FOUR_HEADS
CONV_K = 31
SUBLANES = 8
CONV_HALO = 16
WINDOW = 128
BLOCK = 128
ROPE_BASE = 10000.0
EPS = 1e-6
NEG_INF = -1e30
LOG2E = 1.4426950408889634
Q_SCALE = HEAD_DIM ** -0.5 * LOG2E
IN_W = 2 * ATT_W + 2 * KV_W + 3 * CONV_W + 2 * FOUR_W
OFF_Q = 0
OFF_KV = ATT_W
OFF_GATT = OFF_KV + 2 * KV_W
OFF_CONV = OFF_GATT + ATT_W
OFF_GCONV = OFF_CONV + 2 * CONV_W
OFF_FOUR = OFF_GCONV + CONV_W
OFF_GFOUR = OFF_FOUR + FOUR_W
LANES = 128
KVX_W = 4 * LANES
MOD_ROWS = 16
SEQUENTIAL = pltpu.CompilerParams(dimension_semantics=("arbitrary", "arbitrary"))
ADA_TN = 1536
IN_TM = 1024
OUT_TM = 1024
ATT_BLOCKS = 8
ATT_AHEAD = 8
CTX_AHEAD = 2


def _dot(a, b):
    return jnp.dot(a, b, preferred_element_type=F32)


def _silu(x):
    return x * jax.nn.sigmoid(x)


def _ada_kernel(cc_ref, w_ref, b_ref, o_ref):
    a = _silu(cc_ref[...])
    w = w_ref[0]
    a_hi = a.astype(BF16)
    a_lo = (a - a_hi.astype(F32)).astype(BF16)
    w_hi = w.astype(BF16)
    o_ref[0] = _dot(a_hi, w_hi) + _dot(a_lo, w_hi) + b_ref[0]


def _modulation(cc, w_ada, b_ada):
    tn = ADA_TN
    return pl.pallas_call(
        _ada_kernel,
        grid=(DEPTH, 3 * D_MODEL // tn),
        in_specs=[
            pl.BlockSpec((MOD_ROWS, D_MODEL), lambda l, j: (0, 0)),
            pl.BlockSpec((1, D_MODEL, tn), lambda l, j: (l, 0, j)),
            pl.BlockSpec((1, 1, tn), lambda l, j: (l, 0, j)),
        ],
        out_specs=pl.BlockSpec((1, MOD_ROWS, tn), lambda l, j: (l, 0, j)),
        out_shape=jax.ShapeDtypeStruct((DEPTH, MOD_ROWS, 3 * D_MODEL), F32),
        name="ada_modulation",
    )(cc, w_ada, b_ada.reshape(DEPTH, 1, 3 * D_MODEL))


def _zero_after(token):
    bits = pltpu.bitcast(jnp.broadcast_to(token[:SUBLANES], (SUBLANES, LANES)), jnp.uint32)
    zero = lax.shift_right_logical(lax.shift_right_logical(bits, jnp.uint32(16)), jnp.uint32(16))
    return pltpu.bitcast(zero, F32)[:1]


def _conv_tile(pad_ref, w_ref, t0, lt, zero=None):
    lead = CONV_HALO - CONV_K // 2
    lanes = slice(lt * LANES, (lt + 1) * LANES)
    acc = None
    for r in range(SUBLANES):
        z = None
        for a in range((CONV_K - r + SUBLANES - 1) // SUBLANES):
            j = SUBLANES * a + r
            w = w_ref[j:j + 1, lanes]
            term = pad_ref[pl.ds(t0 + SUBLANES * a, BLOCK + SUBLANES), lanes] * (w if zero is None else w + zero)
            z = term if z is None else z + term
        zs = z[r + lead:r + lead + BLOCK]
        acc = zs if acc is None else acc + zs
    return acc


def _conv_finish(tiles, b_ref, lg_ref, lb_ref):
    y = jnp.concatenate(tiles, axis=1) + b_ref[...]
    mu = jnp.mean(y, axis=-1, keepdims=True)
    yc = y - mu
    var = jnp.mean(yc * yc, axis=-1, keepdims=True)
    return _silu(yc * lax.rsqrt(var + EPS) * lg_ref[...] + lb_ref[...]).astype(BF16)


def _fill_conv_pad(pad_ref, u_ref, seq):
    zeros = jnp.zeros((CONV_HALO, CONV_W), F32)
    pad_ref[0:CONV_HALO, :] = zeros
    pad_ref[CONV_HALO + seq:2 * CONV_HALO + seq, :] = zeros
    pad_ref[CONV_HALO:CONV_HALO + seq, :] = u_ref[0].astype(F32)


def _conv_fillers(pad_ref, conv_refs, cv_ref, t0, n_rows):
    w_ref, b_ref, lg_ref, lb_ref = conv_refs
    pieces = []
    for j in range(n_rows // BLOCK):
        tiles = []
        for lt in range(CONV_W // LANES):
            pieces.append(lambda token, j=j, lt=lt, tiles=tiles: tiles.append(
                _conv_tile(pad_ref, w_ref, t0 + j * BLOCK, lt, _zero_after(token))))

        def finish(token, j=j, tiles=tiles):
            cv_ref[0, j * BLOCK:(j + 1) * BLOCK, :] = _conv_finish(tiles, b_ref, lg_ref, lb_ref)
        pieces.append(finish)
    return pieces


def _conv_specs(index):
    vec = pl.BlockSpec((1, CONV_W), index)
    return [pl.BlockSpec((CONV_K, CONV_W), index), vec, vec, vec]


def _conv_args(conv_w, conv_b, ln_g, ln_b):
    row = lambda a: a.reshape(1, CONV_W)
    return conv_w, row(conv_b), row(ln_g), row(ln_b)


def _cast_weight_once(w_ref, wbf_ref):
    @pl.when((pl.program_id(0) == 0) & (pl.program_id(1) == 0))
    def _():
        wbf_ref[...] = w_ref[0].astype(BF16)


def _project(x_ref, sh_ref, sc_ref, g_ref, w_ref, wbf_ref):
    _cast_weight_once(w_ref, wbf_ref)
    x = x_ref[0]
    ms = jnp.mean(x * x, axis=-1, keepdims=True)
    gain = g_ref[...] * (1.0 + sc_ref[0])
    h = x * lax.rsqrt(ms + EPS) * gain + sh_ref[0]
    return _dot(h.astype(BF16), wbf_ref[...])


def _rope_block(blk, rope_ref, base):
    cos = rope_ref[:, base:base + LANES]
    sin_lo = rope_ref[:, base + LANES:base + 2 * LANES]
    sin_hi = rope_ref[:, base + 2 * LANES:base + 3 * LANES]
    return (blk * cos + pltpu.roll(blk, LANES - 16, 1) * sin_lo + pltpu.roll(blk, 16, 1) * sin_hi)


def _store_padded_kv(kt_ref, v_ref, kblk, vblk):
    tm = kblk.shape[0]
    kt = kblk.T.astype(BF16)
    zeros = jnp.zeros((HEAD_DIM, tm), BF16)
    k_blocks = ((kt[:HEAD_DIM], zeros), (zeros, kt[:HEAD_DIM]), (kt[HEAD_DIM:], zeros), (zeros, kt[HEAD_DIM:]))
    for i, halves in enumerate(k_blocks):
        kt_ref[0, i * LANES:(i + 1) * LANES, :] = jnp.concatenate(halves, axis=0)
    lo = lax.broadcasted_iota(jnp.int32, vblk.shape, 1) < HEAD_DIM
    swapped = pltpu.roll(vblk, HEAD_DIM, 1)
    v_blocks = (jnp.where(lo, vblk, 0.0), jnp.where(lo, 0.0, swapped),
                jnp.where(lo, swapped, 0.0), jnp.where(lo, 0.0, vblk))
    for i, b in enumerate(v_blocks):
        v_ref[0, :, i * LANES:(i + 1) * LANES] = b.astype(BF16)


def _project_epilogue(p, rope_ref, dft_ref, q_ref, kt_ref, v_ref, gatt_ref, gcv_ref, ab_ref, gf_ref):
    for c in range(ATT_W // LANES):
        blk = _rope_block(p[:, OFF_Q + c * LANES:OFF_Q + (c + 1) * LANES], rope_ref, 0)
        q_ref[0, :, c * LANES:(c + 1) * LANES] = blk.astype(BF16)
    _store_padded_kv(kt_ref, v_ref, _rope_block(p[:, OFF_KV:OFF_KV + KV_W], rope_ref, 3 * LANES),
                     p[:, OFF_KV + KV_W:OFF_KV + 2 * KV_W])
    gatt_ref[0] = _silu(p[:, OFF_GATT:OFF_GATT + ATT_W]).astype(BF16)
    gcv_ref[0] = _silu(p[:, OFF_GCONV:OFF_GCONV + CONV_W]).astype(BF16)
    fu = p[:, OFF_FOUR:OFF_FOUR + FOUR_W].astype(BF16)
    ab_ref[0] = _dot(fu, dft_ref[...]).astype(BF16)
    gf_ref[0] = _silu(p[:, OFF_GFOUR:OFF_GFOUR + FOUR_W]).astype(BF16)
    return (p[:, OFF_CONV:OFF_CONV + CONV_W] * jax.nn.sigmoid(p[:, OFF_CONV + CONV_W:OFF_CONV + 2 * CONV_W])
            ).astype(BF16)


def _in_kernel(x_ref, sh_ref, sc_ref, g_ref, w_ref, rope_ref, dft_ref,
               q_ref, kt_ref, v_ref, gatt_ref, gcv_ref, ab_ref, gf_ref, uc_ref, wbf_ref):
    p = _project(x_ref, sh_ref, sc_ref, g_ref, w_ref, wbf_ref)
    uc_ref[0] = _project_epilogue(p, rope_ref, dft_ref, q_ref, kt_ref, v_ref, gatt_ref, gcv_ref, ab_ref, gf_ref)


def _mod_spec(mod_row, j):
    row = (lambda b: mod_row) if mod_row is not None else (lambda b: b)
    return pl.BlockSpec((1, 1, D_MODEL), lambda b, i: (row(b), 0, j))


def _in_specs(tm, mod_row, layer):
    return [
        pl.BlockSpec((1, tm, D_MODEL), lambda b, i: (b, i, 0)),
        _mod_spec(mod_row, 0), _mod_spec(mod_row, 1),
        pl.BlockSpec((1, D_MODEL), lambda b, i: (0, 0)),
        pl.BlockSpec((1, D_MODEL, IN_W), lambda b, i: (layer, 0, 0)),
        pl.BlockSpec((tm, 6 * LANES), lambda b, i: (i, 0)),
        pl.BlockSpec((FOUR_W, 2 * FOUR_W), lambda b, i: (0, 0)),
    ]


def _in_outs(bn, s, tm):
    widths = (ATT_W, None, KVX_W, ATT_W, CONV_W, 2 * FOUR_W, FOUR_W, CONV_W)
    specs = [pl.BlockSpec((1, tm, w), lambda b, i: (b, i, 0)) if w else
             pl.BlockSpec((1, KVX_W, tm), lambda b, i: (b, 0, i)) for w in widths]
    shapes = [jax.ShapeDtypeStruct((bn, s, w) if w else (bn, KVX_W, s), BF16) for w in widths]
    return specs, shapes


def _in_proj(x, mod3, norm_g, w, rope_tab, dft_c, *, tm, mod_row, layer):
    bn, s, _ = x.shape
    specs, shapes = _in_outs(bn, s, tm)
    return pl.pallas_call(
        _in_kernel,
        grid=(bn, s // tm),
        in_specs=_in_specs(tm, mod_row, layer),
        out_specs=specs,
        out_shape=shapes,
        scratch_shapes=[pltpu.VMEM((D_MODEL, IN_W), BF16)],
        compiler_params=SEQUENTIAL,
        name="in_proj",
    )(x, mod3, mod3, norm_g.reshape(1, D_MODEL), w, rope_tab, dft_c)


def _kv_kernel(x_ref, sh_ref, sc_ref, g_ref, w_ref, kt_ref, v_ref, wbf_ref):
    kvp = _project(x_ref, sh_ref, sc_ref, g_ref, w_ref, wbf_ref)
    _store_padded_kv(kt_ref, v_ref, kvp[:, 0:KV_W], kvp[:, KV_W:2 * KV_W])


def _kv_proj(x, mod3, norm_g, w, *, tm, mod_row, layer):
    bn, s, _ = x.shape
    assert OFF_KV % (2 * KV_W) == 0
    return pl.pallas_call(
        _kv_kernel,
        grid=(bn, s // tm),
        in_specs=[
            pl.BlockSpec((1, tm, D_MODEL), lambda b, i: (b, i, 0)),
            _mod_spec(mod_row, 0), _mod_spec(mod_row, 1),
            pl.BlockSpec((1, D_MODEL), lambda b, i: (0, 0)),
            pl.BlockSpec((1, D_MODEL, 2 * KV_W), lambda b, i: (layer, 0, OFF_KV // (2 * KV_W))),
        ],
        out_specs=[pl.BlockSpec((1, KVX_W, tm), lambda b, i: (b, 0, i)),
                   pl.BlockSpec((1, tm, KVX_W), lambda b, i: (b, i, 0))],
        out_shape=[jax.ShapeDtypeStruct((bn, KVX_W, s), BF16), jax.ShapeDtypeStruct((bn, s, KVX_W), BF16)],
        scratch_shapes=[pltpu.VMEM((D_MODEL, 2 * KV_W), BF16)],
        compiler_params=SEQUENTIAL,
        name="ctx_kv_proj",
    )(x, mod3, mod3, norm_g.reshape(1, D_MODEL), w)


class _Chain:
    def __init__(self, q2, sink_col, kset):
        self.q2, self.sink_col, self.kset = q2, sink_col, kset

    def scores(self):
        self.s = []
        for kt, _, bias in self.kset:
            sc = _dot(self.q2, kt)
            self.s.append(sc if bias is None else sc + bias)

    def softmax(self):
        m = self.sink_col
        for sc in self.s:
            m = jnp.maximum(m, sc.max(axis=-1, keepdims=True))
        self.p = [jnp.exp2(sc - m).astype(BF16) for sc in self.s]
        self.extra = jnp.exp2(self.sink_col - m)
        self.row_max = m
        self.s = None

    def values(self):
        r = None
        for p, (_, v, _) in zip(self.p, self.kset):
            pv = _dot(p, jnp.concatenate([v, jnp.ones_like(v)], axis=1))
            r = pv if r is None else r + pv
        self.p = None
        return r[:, :LANES] / (r[:, LANES:] + self.extra)


def _run_chains(chains, finish, fillers, ahead):
    n = len(chains)
    fillers = list(fillers)
    for i in range(-ahead, n):
        if 0 <= i < n:
            finish(i, chains[i].values())
        if 0 <= i + ahead < n:
            chains[i + ahead].scores()
        if 0 <= i + 1 < n:
            chains[i + 1].softmax()
            if fillers:
                fillers.pop(0)(chains[i + 1].row_max)
    for f in fillers:
        f(chains[-1].row_max)


def _kv_block(kvh, e):
    k0 = (2 * kvh + e) * LANES
    return slice(k0, k0 + LANES)


def _group_chains(sink_ref, q_ref, rows, tq, kvh, ksets):
    c0 = 2 * kvh * LANES
    q2 = jnp.concatenate([q_ref[0, rows, c0:c0 + LANES], q_ref[0, rows, c0 + LANES:c0 + 2 * LANES]], axis=0)
    chains = []
    for e in range(2):
        sink_col = jnp.concatenate([jnp.full((tq, 1), sink_ref[GQA_GROUP * kvh + e] * LOG2E, F32),
                                    jnp.full((tq, 1), sink_ref[GQA_GROUP * kvh + 2 + e] * LOG2E, F32)], axis=0)
        chains.append(_Chain(q2, sink_col, ksets[e]))
    return chains


def _make_finish(groups, g_ref, o_ref):
    partial = {}

    def finish(i, out):
        if i % 2 == 0:
            partial[i // 2] = out
            return
        rows, tq, kvh = groups[i // 2]
        o = partial.pop(i // 2) + out
        c0 = 2 * kvh * LANES
        for hf in range(2):
            cols = slice(c0 + hf * LANES, c0 + (hf + 1) * LANES)
            g = g_ref[0, rows, cols].astype(F32)
            o_ref[0, rows, cols] = (o[hf * tq:(hf + 1) * tq] * g).astype(BF16)
    return finish


def _local_attn_kernel(sink_ref, q_ref, kt_ref, v_ref, ktc_ref, vc_ref, bias_ref, g_ref, uc_ref,
                       cw_ref, cb_ref, lg_ref, lb_ref, o_ref, cv_ref, pad_ref, *, seq, sub, ahead):
    @pl.when(pl.program_id(1) == 0)
    def _():
        _fill_conv_pad(pad_ref, uc_ref, seq)

    nblk = seq // BLOCK
    chains, groups = [], []
    for sb in range(sub):
        n = pl.program_id(1) * sub + sb
        start = pl.multiple_of(jnp.clip((n - 1) * BLOCK, 0, seq - 3 * BLOCK), BLOCK)
        win = pl.ds(start, 3 * BLOCK)
        bias = bias_ref[jnp.where(n == 0, 0, jnp.where(n == nblk - 1, 2, 1))]
        bias2 = jnp.concatenate([bias, bias], axis=0)
        rows = slice(sb * BLOCK, (sb + 1) * BLOCK)
        for kvh in range(N_KV_HEADS):
            ksets = []
            for e in range(2):
                blk = _kv_block(kvh, e)
                ksets.append([(kt_ref[0, blk, win], v_ref[0, win, blk], bias2),
                              (ktc_ref[0, blk, :], vc_ref[0, :, blk], None)])
            chains += _group_chains(sink_ref, q_ref, rows, BLOCK, kvh, ksets)
            groups.append((rows, BLOCK, kvh))
    t0 = pl.multiple_of(pl.program_id(1) * (sub * BLOCK), sub * BLOCK)
    fillers = _conv_fillers(pad_ref, (cw_ref, cb_ref, lg_ref, lb_ref), cv_ref, t0, sub * BLOCK)
    _run_chains(chains, _make_finish(groups, g_ref, o_ref), fillers, ahead)


def _band_bias():
    i = np.arange(BLOCK)[:, None]
    k = np.arange(3 * BLOCK)[None, :]
    tabs = [np.where(np.abs(BLOCK * qb + i - k) <= WINDOW, 0.0, NEG_INF) for qb in range(3)]
    return jnp.asarray(np.stack(tabs).astype(np.float32))


def _local_attention(sink, q, kv, kvc, gatt, uc, conv_args):
    bn, s, _ = q.shape
    n_ctx = kvc[1].shape[1]
    sub = ATT_BLOCKS
    ahead = ATT_AHEAD
    tq = sub * BLOCK
    assert s % tq == 0 and s >= 3 * BLOCK
    tile = lambda w: pl.BlockSpec((1, tq, w), lambda b, n: (b, n, 0))
    whole = lambda rows, w: pl.BlockSpec((1, rows, w), lambda b, n: (b, 0, 0))
    kv_specs = lambda rows: [whole(KVX_W, rows), whole(rows, KVX_W)]
    return pl.pallas_call(
        functools.partial(_local_attn_kernel, seq=s, sub=sub, ahead=ahead),
        grid=(bn, s // tq),
        in_specs=[
            pl.BlockSpec(memory_space=pltpu.SMEM),
            tile(ATT_W), *kv_specs(s), *kv_specs(n_ctx),
            pl.BlockSpec((3, BLOCK, 3 * BLOCK), lambda b, n: (0, 0, 0)),
            tile(ATT_W), whole(s, CONV_W),
        ] + _conv_specs(lambda b, n: (0, 0)),
        out_specs=[tile(ATT_W), tile(CONV_W)],
        out_shape=[jax.ShapeDtypeStruct((bn, s, ATT_W), BF16), jax.ShapeDtypeStruct((bn, s, CONV_W), BF16)],
        scratch_shapes=[pltpu.VMEM((s + 2 * CONV_HALO, CONV_W), F32)],
        compiler_params=SEQUENTIAL,
        name="local_attention",
    )(sink, q, *kv, *kvc, _band_bias(), gatt, uc, *conv_args)


def _ctx_attn_kernel(sink_ref, q_ref, ktc_ref, vc_ref, g_ref, uc_ref, cw_ref, cb_ref, lg_ref, lb_ref,
                     o_ref, cv_ref, pad_ref, *, n_ctx, ahead):
    _fill_conv_pad(pad_ref, uc_ref, n_ctx)
    rows = slice(0, n_ctx)
    chains, groups = [], []
    for kvh in range(N_KV_HEADS):
        ksets = []
        for e in range(2):
            blk = _kv_block(kvh, e)
            ksets.append([(ktc_ref[0, blk, :], vc_ref[0, :, blk], None)])
        chains += _group_chains(sink_ref, q_ref, rows, n_ctx, kvh, ksets)
        groups.append((rows, n_ctx, kvh))
    fillers = _conv_fillers(pad_ref, (cw_ref, cb_ref, lg_ref, lb_ref), cv_ref, 0, n_ctx)
    _run_chains(chains, _make_finish(groups, g_ref, o_ref), fillers, ahead)


def _ctx_attention(sink, q, kvc, gatt, uc, conv_args):
    bn, n_ctx, _ = q.shape
    ahead = CTX_AHEAD
    whole = lambda w: pl.BlockSpec((1, n_ctx, w), lambda b: (b, 0, 0))
    return pl.pallas_call(
        functools.partial(_ctx_attn_kernel, n_ctx=n_ctx, ahead=ahead),
        grid=(bn,),
        in_specs=[pl.BlockSpec(memory_space=pltpu.SMEM), whole(ATT_W),
                  pl.BlockSpec((1, KVX_W, n_ctx), lambda b: (b, 0, 0)), whole(KVX_W), whole(ATT_W),
                  whole(CONV_W)] + _conv_specs(lambda b: (0, 0)),
        out_specs=[whole(ATT_W), whole(CONV_W)],
        out_shape=[jax.ShapeDtypeStruct((bn, n_ctx, ATT_W), BF16),
                   jax.ShapeDtypeStruct((bn, n_ctx, CONV_W), BF16)],
        scratch_shapes=[pltpu.VMEM((n_ctx + 2 * CONV_HALO, CONV_W), F32)],
        name="ctx_attention",
    )(sink, q, *kvc, gatt, uc, *conv_args)


def _fourier_prepare(ab_ref, cw_ref, sw_ref, rc_ref, rs_ref, half):
    top = ab_ref[0, 0:half, :].astype(F32)
    bot = ab_ref[0, half:2 * half, :].astype(F32)
    plus = top + bot
    minus = top - bot
    a_m, b_m = minus[:, :FOUR_W], minus[:, FOUR_W:]
    cw, sw = cw_ref[...], sw_ref[...]
    rc_ref[:, 0:FOUR_W] = plus[:, :FOUR_W].astype(BF16)
    rc_ref[:, FOUR_W:] = (cw * a_m - sw * b_m).astype(BF16)
    rs_ref[:, 0:FOUR_W] = plus[:, FOUR_W:].astype(BF16)
    rs_ref[:, FOUR_W:] = (sw * a_m + cw * b_m).astype(BF16)


def _fourier_rows(c_ref, s_ref, rc_ref, rs_ref, wf_ref, bf_ref, il_ref, r0, chunk):
    e = _dot(c_ref[pl.ds(r0, chunk), :], rc_ref[...]) - _dot(s_ref[pl.ds(r0, chunk), :], rs_ref[...])
    eb = e.astype(BF16)
    for p in range(2):
        f = _dot(eb[:, p * FOUR_W:(p + 1) * FOUR_W], wf_ref[...])
        for lt in range(FOUR_W // LANES):
            il_ref[lt, pl.ds(p, chunk, stride=2), :] = f[:, lt * LANES:(lt + 1) * LANES]
    return jnp.concatenate([il_ref[lt] for lt in range(FOUR_W // LANES)], axis=1) + bf_ref[...]


def _out_kernel(x_ref, att_ref, cv_ref, gcv_ref, ab_ref, gf_ref, c_ref, s_ref, cw_ref, sw_ref, wf_ref, bf_ref,
                w_ref, gate_ref, fg_ref, o_ref, rc_ref, rs_ref, il_ref, wbf_ref, *, tm, half, final):
    _cast_weight_once(w_ref, wbf_ref)

    @pl.when(pl.program_id(1) == 0)
    def _():
        _fourier_prepare(ab_ref, cw_ref, sw_ref, rc_ref, rs_ref, half)

    r0 = pl.multiple_of(pl.program_id(1) * (tm // 2), tm // 2)
    four = _fourier_rows(c_ref, s_ref, rc_ref, rs_ref, wf_ref, bf_ref, il_ref, r0, tm // 2)
    fo = (four * gf_ref[0].astype(F32)).astype(BF16)
    cv = cv_ref[0] * gcv_ref[0]
    y = _dot(jnp.concatenate([att_ref[0], cv, fo], axis=1), wbf_ref[...])
    xn = x_ref[0] + gate_ref[0] * y
    if final:
        ms = jnp.mean(xn * xn, axis=-1, keepdims=True)
        xn = xn * lax.rsqrt(ms + EPS) * fg_ref[...]
    o_ref[0] = xn


def _out_proj(x, att, cv, gcv, ab, gf, pos_tabs, w_four_bf, b_four, w_out, mod3, final_g,
              *, tm, mod_row, final, layer):
    bn, s, _ = x.shape
    half = s // 2
    c_half, s_half, cw, sw = pos_tabs
    tile = lambda w: pl.BlockSpec((1, tm, w), lambda b, i: (b, i, 0))
    const = lambda shape: pl.BlockSpec(shape, lambda b, i: (0,) * len(shape))
    return pl.pallas_call(
        functools.partial(_out_kernel, tm=tm, half=half, final=final),
        grid=(bn, s // tm),
        in_specs=[
            tile(D_MODEL), tile(ATT_W), tile(CONV_W), tile(CONV_W),
            pl.BlockSpec((1, s, 2 * FOUR_W), lambda b, i: (b, 0, 0)),
            tile(FOUR_W),
            const((half, half)), const((half, half)), const((half, FOUR_W)), const((half, FOUR_W)),
            const((FOUR_W, FOUR_W)), const((1, FOUR_W)),
            pl.BlockSpec((1, D_MODEL, D_MODEL), lambda b, i: (layer, 0, 0)),
            _mod_spec(mod_row, 2),
            const((1, D_MODEL)),
        ],
        out_specs=tile(D_MODEL),
        out_shape=jax.ShapeDtypeStruct((bn, s, D_MODEL), F32),
        scratch_shapes=[pltpu.VMEM((half, 2 * FOUR_W), BF16), pltpu.VMEM((half, 2 * FOUR_W), BF16),
                        pltpu.VMEM((FOUR_W // LANES, tm, LANES), F32),
                        pltpu.VMEM((D_MODEL, D_MODEL), BF16)],
        compiler_params=SEQUENTIAL,
        name="out_proj",
    )(x, att, cv, gcv, ab, gf, c_half, s_half, cw, sw, w_four_bf, b_four.reshape(1, FOUR_W),
      w_out, mod3, final_g.reshape(1, D_MODEL))


def _rope_tables(s):
    t = np.arange(s)
    half = HEAD_DIM // 4
    freqs = ROPE_BASE ** (-np.arange(half) / half)
    zero = np.zeros((s, half))

    def per_head(fn_lo, fn_hi):
        parts = []
        for pos in (t // GRID_W, t % GRID_W):
            ang = pos[:, None] * freqs[None, :]
            parts += [fn_lo(ang), fn_hi(ang)]
        return np.tile(np.concatenate(parts, axis=-1), (1, LANES // HEAD_DIM))

    cos = per_head(np.cos, np.cos)
    sin_lo = per_head(lambda a: -np.sin(a), lambda a: zero)
    sin_hi = per_head(lambda a: zero, np.sin)
    tab = np.concatenate([cos * Q_SCALE, sin_lo * Q_SCALE, sin_hi * Q_SCALE, cos, sin_lo, sin_hi], axis=-1)
    return jnp.asarray(tab.astype(np.float32))


def _identity_rope_tables(s):
    one = np.ones((s, LANES))
    zero = np.zeros((s, LANES))
    return jnp.asarray(np.concatenate([one * Q_SCALE, zero, zero, one, zero, zero], axis=-1).astype(np.float32))


def _channel_dft(n_pos):
    c = np.arange(FOUR_HEAD_DIM)
    ang = 2.0 * np.pi * ((c[:, None] * c[None, :]) % FOUR_HEAD_DIM) / FOUR_HEAD_DIM
    eye = np.eye(FOUR_HEADS)
    scale = 1.0 / np.sqrt(float(n_pos * FOUR_HEAD_DIM))
    tab = np.concatenate([np.kron(eye, np.cos(ang)), np.kron(eye, np.sin(ang))], axis=1) * scale
    return jnp.asarray(tab.astype(np.float32))


def _position_dft(n_pos):
    half = n_pos // 2
    k = np.arange(half)
    ang = 2.0 * np.pi * ((k[:, None] * k[None, :]) % half) / half
    beta = np.pi * k / half
    ones = np.ones((1, FOUR_W))
    f32 = lambda a: jnp.asarray(a.astype(np.float32))
    return (f32(np.cos(ang)).astype(BF16), f32(np.sin(ang)).astype(BF16),
            f32(np.cos(beta)[:, None] * ones), f32(np.sin(beta)[:, None] * ones))


def kernel(x, c, ctx, c_ctx, w_ada, b_ada, norm_g, w_in, attn_sink, conv_w, conv_b,
           conv_ln_g, conv_ln_b, w_four, b_four, w_out, final_g):
    bn, s, _ = x.shape
    n_ctx = ctx.shape[1]
    assert bn < MOD_ROWS
    ctx_row = bn

    cc = jnp.zeros((MOD_ROWS, D_MODEL), F32).at[:bn].set(c).at[ctx_row].set(c_ctx)
    mod = _modulation(cc, w_ada, b_ada)

    w_four_bf = w_four.astype(BF16)
    rope_x = _rope_tables(s)
    rope_c = _identity_rope_tables(n_ctx)
    dft_x = _channel_dft(s).astype(BF16)
    dft_ctx = _channel_dft(n_ctx).astype(BF16)
    pos_x = _position_dft(s)
    pos_ctx = _position_dft(n_ctx)

    h_ctx = ctx
    for l in range(DEPTH):
        mod3 = mod[l].reshape(MOD_ROWS, 1, 3 * D_MODEL)
        conv_args = _conv_args(conv_w[l], conv_b[l], conv_ln_g[l], conv_ln_b[l])
        four_args = (w_four_bf[l], b_four[l], w_out, mod3, final_g)
        sink = attn_sink[l]
        last = l == DEPTH - 1
        if not last:
            qc, ktc, vc, gatt_c, gcv_c, ab_c, gf_c, uc_c = _in_proj(
                h_ctx, mod3, norm_g[l], w_in, rope_c, dft_ctx, tm=n_ctx, mod_row=ctx_row, layer=l)
            kvc = (ktc, vc)
            att_c, cv_c = _ctx_attention(sink, qc, kvc, gatt_c, uc_c, conv_args)
        else:
            kvc = _kv_proj(h_ctx, mod3, norm_g[l], w_in, tm=n_ctx, mod_row=ctx_row, layer=l)
        q, kt, v, gatt, gcv, ab, gf, uc = _in_proj(
            x, mod3, norm_g[l], w_in, rope_x, dft_x, tm=IN_TM, mod_row=None, layer=l)
        att, cv = _local_attention(sink, q, (kt, v), kvc, gatt, uc, conv_args)
        if not last:
            h_ctx = _out_proj(h_ctx, att_c, cv_c, gcv_c, ab_c, gf_c, pos_ctx, *four_args,
                              tm=n_ctx, mod_row=ctx_row, final=False, layer=l)
        x = _out_proj(x, att, cv, gcv, ab, gf, pos_x, *four_args, tm=OUT_TM, mod_row=None, final=last, layer=l)
    return x
```

```python
import functools

import numpy as np
import jax
import jax.numpy as jnp
from jax import lax
from jax.experimental import pallas as pl
from jax.experimental.pallas import tpu as pltpu

F32 = jnp.float32
BF16 = jnp.bfloat16

D_MODEL = 1024
DEPTH = 2
GRID_W = 64
HEAD_DIM = 64
ATT_W = 512
N_HEADS = 8
N_KV_HEADS = 2
GQA_GROUP = N_HEADS // N_KV_HEADS
KV_W = N_KV_HEADS * HEAD_DIM
CONV_W = 256
FOUR_W = 256
FOUR_HEADS = 4
FOUR_HEAD_DIM = FOUR_W // FOUR_HEADS
CONV_K = 31
SUBLANES = 8
CONV_HALO = 16
WINDOW = 128
BLOCK = 128
ROPE_BASE = 10000.0
EPS = 1e-6
NEG_INF = -1e30
LOG2E = 1.4426950408889634
Q_SCALE = HEAD_DIM ** -0.5 * LOG2E
IN_W = 2 * ATT_W + 2 * KV_W + 3 * CONV_W + 2 * FOUR_W
OFF_Q = 0
OFF_KV = ATT_W
OFF_GATT = OFF_KV + 2 * KV_W
OFF_CONV = OFF_GATT + ATT_W
OFF_GCONV = OFF_CONV + 2 * CONV_W
OFF_FOUR = OFF_GCONV + CONV_W
OFF_GFOUR = OFF_FOUR + FOUR_W
LANES = 128
KVX_W = 4 * LANES
MOD_ROWS = 16
SEQUENTIAL = pltpu.CompilerParams(dimension_semantics=("arbitrary", "arbitrary"))
ADA_TN = 1536
IN_TM = 1024
OUT_TM = 1024
ATT_BLOCKS = 8
ATT_AHEAD = 8
CTX_AHEAD = 2


def _dot(a, b):
    return jnp.dot(a, b, preferred_element_type=F32)


def _silu(x):
    return x * jax.nn.sigmoid(x)


def _ada_kernel(cc_ref, w_ref, b_ref, o_ref):
    a = _silu(cc_ref[...])
    w = w_ref[0]
    a_hi = a.astype(BF16)
    a_lo = (a - a_hi.astype(F32)).astype(BF16)
    w_hi = w.astype(BF16)
    o_ref[0] = _dot(a_hi, w_hi) + _dot(a_lo, w_hi) + b_ref[0]


def _modulation(cc, w_ada, b_ada):
    tn = ADA_TN
    return pl.pallas_call(
        _ada_kernel,
        grid=(DEPTH, 3 * D_MODEL // tn),
        in_specs=[
            pl.BlockSpec((MOD_ROWS, D_MODEL), lambda l, j: (0, 0)),
            pl.BlockSpec((1, D_MODEL, tn), lambda l, j: (l, 0, j)),
            pl.BlockSpec((1, 1, tn), lambda l, j: (l, 0, j)),
        ],
        out_specs=pl.BlockSpec((1, MOD_ROWS, tn), lambda l, j: (l, 0, j)),
        out_shape=jax.ShapeDtypeStruct((DEPTH, MOD_ROWS, 3 * D_MODEL), F32),
        name="ada_modulation",
    )(cc, w_ada, b_ada.reshape(DEPTH, 1, 3 * D_MODEL))


def _zero_after(token):
    bits = pltpu.bitcast(jnp.broadcast_to(token[:SUBLANES], (SUBLANES, LANES)), jnp.uint32)
    zero = lax.shift_right_logical(lax.shift_right_logical(bits, jnp.uint32(16)), jnp.uint32(16))
    return pltpu.bitcast(zero, F32)[:1]


def _conv_tile(pad_ref, w_ref, t0, lt, zero=None):
    lead = CONV_HALO - CONV_K // 2
    lanes = slice(lt * LANES, (lt + 1) * LANES)
    acc = None
    for r in range(SUBLANES):
        z = None
        for a in range((CONV_K - r + SUBLANES - 1) // SUBLANES):
            j = SUBLANES * a + r
            w = w_ref[j:j + 1, lanes]
            term = pad_ref[pl.ds(t0 + SUBLANES * a, BLOCK + SUBLANES), lanes] * (w if zero is None else w + zero)
            z = term if z is None else z + term
        zs = z[r + lead:r + lead + BLOCK]
        acc = zs if acc is None else acc + zs
    return acc


def _conv_finish(tiles, b_ref, lg_ref, lb_ref):
    y = jnp.concatenate(tiles, axis=1) + b_ref[...]
    mu = jnp.mean(y, axis=-1, keepdims=True)
    yc = y - mu
    var = jnp.mean(yc * yc, axis=-1, keepdims=True)
    return _silu(yc * lax.rsqrt(var + EPS) * lg_ref[...] + lb_ref[...]).astype(BF16)


def _fill_conv_pad(pad_ref, u_ref, seq):
    zeros = jnp.zeros((CONV_HALO, CONV_W), F32)
    pad_ref[0:CONV_HALO, :] = zeros
    pad_ref[CONV_HALO + seq:2 * CONV_HALO + seq, :] = zeros
    pad_ref[CONV_HALO:CONV_HALO + seq, :] = u_ref[0].astype(F32)


def _conv_fillers(pad_ref, conv_refs, cv_ref, t0, n_rows):
    w_ref, b_ref, lg_ref, lb_ref = conv_refs
    pieces = []
    for j in range(n_rows // BLOCK):
        tiles = []
        for lt in range(CONV_W // LANES):
            pieces.append(lambda token, j=j, lt=lt, tiles=tiles: tiles.append(
                _conv_tile(pad_ref, w_ref, t0 + j * BLOCK, lt, _zero_after(token))))

        def finish(token, j=j, tiles=tiles):
            cv_ref[0, j * BLOCK:(j + 1) * BLOCK, :] = _conv_finish(tiles, b_ref, lg_ref, lb_ref)
        pieces.append(finish)
    return pieces


def _conv_specs(index):
    vec = pl.BlockSpec((1, CONV_W), index)
    return [pl.BlockSpec((CONV_K, CONV_W), index), vec, vec, vec]


def _conv_args(conv_w, conv_b, ln_g, ln_b):
    row = lambda a: a.reshape(1, CONV_W)
    return conv_w, row(conv_b), row(ln_g), row(ln_b)


def _cast_weight_once(w_ref, wbf_ref):
    @pl.when((pl.program_id(0) == 0) & (pl.program_id(1) == 0))
    def _():
        wbf_ref[...] = w_ref[0].astype(BF16)


def _project(x_ref, sh_ref, sc_ref, g_ref, w_ref, wbf_ref):
    _cast_weight_once(w_ref, wbf_ref)
    x = x_ref[0]
    ms = jnp.mean(x * x, axis=-1, keepdims=True)
    gain = g_ref[...] * (1.0 + sc_ref[0])
    h = x * lax.rsqrt(ms + EPS) * gain + sh_ref[0]
    return _dot(h.astype(BF16), wbf_ref[...])


def _rope_block(blk, rope_ref, base):
    cos = rope_ref[:, base:base + LANES]
    sin_lo = rope_ref[:, base + LANES:base + 2 * LANES]
    sin_hi = rope_ref[:, base + 2 * LANES:base + 3 * LANES]
    return (blk * cos + pltpu.roll(blk, LANES - 16, 1) * sin_lo + pltpu.roll(blk, 16, 1) * sin_hi)


def _store_padded_kv(kt_ref, v_ref, kblk, vblk):
    tm = kblk.shape[0]
    kt = kblk.T.astype(BF16)
    zeros = jnp.zeros((HEAD_DIM, tm), BF16)
    k_blocks = ((kt[:HEAD_DIM], zeros), (zeros, kt[:HEAD_DIM]), (kt[HEAD_DIM:], zeros), (zeros, kt[HEAD_DIM:]))
    for i, halves in enumerate(k_blocks):
        kt_ref[0, i * LANES:(i + 1) * LANES, :] = jnp.concatenate(halves, axis=0)
    lo = lax.broadcasted_iota(jnp.int32, vblk.shape, 1) < HEAD_DIM
    swapped = pltpu.roll(vblk, HEAD_DIM, 1)
    v_blocks = (jnp.where(lo, vblk, 0.0), jnp.where(lo, 0.0, swapped),
                jnp.where(lo, swapped, 0.0), jnp.where(lo, 0.0, vblk))
    for i, b in enumerate(v_blocks):
        v_ref[0, :, i * LANES:(i + 1) * LANES] = b.astype(BF16)


def _project_epilogue(p, rope_ref, dft_ref, q_ref, kt_ref, v_ref, gatt_ref, gcv_ref, ab_ref, gf_ref):
    for c in range(ATT_W // LANES):
        blk = _rope_block(p[:, OFF_Q + c * LANES:OFF_Q + (c + 1) * LANES], rope_ref, 0)
        q_ref[0, :, c * LANES:(c + 1) * LANES] = blk.astype(BF16)
    _store_padded_kv(kt_ref, v_ref, _rope_block(p[:, OFF_KV:OFF_KV + KV_W], rope_ref, 3 * LANES),
                     p[:, OFF_KV + KV_W:OFF_KV + 2 * KV_W])
    gatt_ref[0] = _silu(p[:, OFF_GATT:OFF_GATT + ATT_W]).astype(BF16)
    gcv_ref[0] = _silu(p[:, OFF_GCONV:OFF_GCONV + CONV_W]).astype(BF16)
    fu = p[:, OFF_FOUR:OFF_FOUR + FOUR_W].astype(BF16)
    ab_ref[0] = _dot(fu, dft_ref[...]).astype(BF16)
    gf_ref[0] = _silu(p[:, OFF_GFOUR:OFF_GFOUR + FOUR_W]).astype(BF16)
    return (p[:, OFF_CONV:OFF_CONV + CONV_W] * jax.nn.sigmoid(p[:, OFF_CONV + CONV_W:OFF_CONV + 2 * CONV_W])
            ).astype(BF16)


def _in_kernel(x_ref, sh_ref, sc_ref, g_ref, w_ref, rope_ref, dft_ref,
               q_ref, kt_ref, v_ref, gatt_ref, gcv_ref, ab_ref, gf_ref, uc_ref, wbf_ref):
    p = _project(x_ref, sh_ref, sc_ref, g_ref, w_ref, wbf_ref)
    uc_ref[0] = _project_epilogue(p, rope_ref, dft_ref, q_ref, kt_ref, v_ref, gatt_ref, gcv_ref, ab_ref, gf_ref)


def _mod_spec(mod_row, j):
    row = (lambda b: mod_row) if mod_row is not None else (lambda b: b)
    return pl.BlockSpec((1, 1, D_MODEL), lambda b, i: (row(b), 0, j))


def _in_specs(tm, mod_row, layer):
    return [
        pl.BlockSpec((1, tm, D_MODEL), lambda b, i: (b, i, 0)),
        _mod_spec(mod_row, 0), _mod_spec(mod_row, 1),
        pl.BlockSpec((1, D_MODEL), lambda b, i: (0, 0)),
        pl.BlockSpec((1, D_MODEL, IN_W), lambda b, i: (layer, 0, 0)),
        pl.BlockSpec((tm, 6 * LANES), lambda b, i: (i, 0)),
        pl.BlockSpec((FOUR_W, 2 * FOUR_W), lambda b, i: (0, 0)),
    ]


def _in_outs(bn, s, tm):
    widths = (ATT_W, None, KVX_W, ATT_W, CONV_W, 2 * FOUR_W, FOUR_W, CONV_W)
    specs = [pl.BlockSpec((1, tm, w), lambda b, i: (b, i, 0)) if w else
             pl.BlockSpec((1, KVX_W, tm), lambda b, i: (b, 0, i)) for w in widths]
    shapes = [jax.ShapeDtypeStruct((bn, s, w) if w else (bn, KVX_W, s), BF16) for w in widths]
    return specs, shapes


def _in_proj(x, mod3, norm_g, w, rope_tab, dft_c, *, tm, mod_row, layer):
    bn, s, _ = x.shape
    specs, shapes = _in_outs(bn, s, tm)
    return pl.pallas_call(
        _in_kernel,
        grid=(bn, s // tm),
        in_specs=_in_specs(tm, mod_row, layer),
        out_specs=specs,
        out_shape=shapes,
        scratch_shapes=[pltpu.VMEM((D_MODEL, IN_W), BF16)],
        compiler_params=SEQUENTIAL,
        name="in_proj",
    )(x, mod3, mod3, norm_g.reshape(1, D_MODEL), w, rope_tab, dft_c)


def _kv_kernel(x_ref, sh_ref, sc_ref, g_ref, w_ref, kt_ref, v_ref, wbf_ref):
    kvp = _project(x_ref, sh_ref, sc_ref, g_ref, w_ref, wbf_ref)
    _store_padded_kv(kt_ref, v_ref, kvp[:, 0:KV_W], kvp[:, KV_W:2 * KV_W])


def _kv_proj(x, mod3, norm_g, w, *, tm, mod_row, layer):
    bn, s, _ = x.shape
    assert OFF_KV % (2 * KV_W) == 0
    return pl.pallas_call(
        _kv_kernel,
        grid=(bn, s // tm),
        in_specs=[
            pl.BlockSpec((1, tm, D_MODEL), lambda b, i: (b, i, 0)),
            _mod_spec(mod_row, 0), _mod_spec(mod_row, 1),
            pl.BlockSpec((1, D_MODEL), lambda b, i: (0, 0)),
            pl.BlockSpec((1, D_MODEL, 2 * KV_W), lambda b, i: (layer, 0, OFF_KV // (2 * KV_W))),
        ],
        out_specs=[pl.BlockSpec((1, KVX_W, tm), lambda b, i: (b, 0, i)),
                   pl.BlockSpec((1, tm, KVX_W), lambda b, i: (b, i, 0))],
        out_shape=[jax.ShapeDtypeStruct((bn, KVX_W, s), BF16), jax.ShapeDtypeStruct((bn, s, KVX_W), BF16)],
        scratch_shapes=[pltpu.VMEM((D_MODEL, 2 * KV_W), BF16)],
        compiler_params=SEQUENTIAL,
        name="ctx_kv_proj",
    )(x, mod3, mod3, norm_g.reshape(1, D_MODEL), w)


class _Chain:
    def __init__(self, q2, sink_col, kset):
        self.q2, self.sink_col, self.kset = q2, sink_col, kset

    def scores(self):
        self.s = []
        for kt, _, bias in self.kset:
            sc = _dot(self.q2, kt)
            self.s.append(sc if bias is None else sc + bias)

    def softmax(self):
        m = self.sink_col
        for sc in self.s:
            m = jnp.maximum(m, sc.max(axis=-1, keepdims=True))
        self.p = [jnp.exp2(sc - m).astype(BF16) for sc in self.s]
        self.extra = jnp.exp2(self.sink_col - m)
        self.row_max = m
        self.s = None

    def values(self):
        r = None
        for p, (_, v, _) in zip(self.p, self.kset):
            pv = _dot(p, jnp.concatenate([v, jnp.ones_like(v)], axis=1))
            r = pv if r is None else r + pv
        self.p = None
        return r[:, :LANES] / (r[:, LANES:] + self.extra)


def _run_chains(chains, finish, fillers, ahead):
    n = len(chains)
    fillers = list(fillers)
    for i in range(-ahead, n):
        if 0 <= i < n:
            finish(i, chains[i].values())
        if 0 <= i + ahead < n:
            chains[i + ahead].scores()
        if 0 <= i + 1 < n:
            chains[i + 1].softmax()
            if fillers:
                fillers.pop(0)(chains[i + 1].row_max)
    for f in fillers:
        f(chains[-1].row_max)


def _kv_block(kvh, e):
    k0 = (2 * kvh + e) * LANES
    return slice(k0, k0 + LANES)


def _group_chains(sink_ref, q_ref, rows, tq, kvh, ksets):
    c0 = 2 * kvh * LANES
    q2 = jnp.concatenate([q_ref[0, rows, c0:c0 + LANES], q_ref[0, rows, c0 + LANES:c0 + 2 * LANES]], axis=0)
    chains = []
    for e in range(2):
        sink_col = jnp.concatenate([jnp.full((tq, 1), sink_ref[GQA_GROUP * kvh + e] * LOG2E, F32),
                                    jnp.full((tq, 1), sink_ref[GQA_GROUP * kvh + 2 + e] * LOG2E, F32)], axis=0)
        chains.append(_Chain(q2, sink_col, ksets[e]))
    return chains


def _make_finish(groups, g_ref, o_ref):
    partial = {}

    def finish(i, out):
        if i % 2 == 0:
            partial[i // 2] = out
            return
        rows, tq, kvh = groups[i // 2]
        o = partial.pop(i // 2) + out
        c0 = 2 * kvh * LANES
        for hf in range(2):
            cols = slice(c0 + hf * LANES, c0 + (hf + 1) * LANES)
            g = g_ref[0, rows, cols].astype(F32)
            o_ref[0, rows, cols] = (o[hf * tq:(hf + 1) * tq] * g).astype(BF16)
    return finish


def _local_attn_kernel(sink_ref, q_ref, kt_ref, v_ref, ktc_ref, vc_ref, bias_ref, g_ref, uc_ref,
                       cw_ref, cb_ref, lg_ref, lb_ref, o_ref, cv_ref, pad_ref, *, seq, sub, ahead):
    @pl.when(pl.program_id(1) == 0)
    def _():
        _fill_conv_pad(pad_ref, uc_ref, seq)

    nblk = seq // BLOCK
    chains, groups = [], []
    for sb in range(sub):
        n = pl.program_id(1) * sub + sb
        start = pl.multiple_of(jnp.clip((n - 1) * BLOCK, 0, seq - 3 * BLOCK), BLOCK)
        win = pl.ds(start, 3 * BLOCK)
        bias = bias_ref[jnp.where(n == 0, 0, jnp.where(n == nblk - 1, 2, 1))]
        bias2 = jnp.concatenate([bias, bias], axis=0)
        rows = slice(sb * BLOCK, (sb + 1) * BLOCK)
        for kvh in range(N_KV_HEADS):
            ksets = []
            for e in range(2):
                blk = _kv_block(kvh, e)
                ksets.append([(kt_ref[0, blk, win], v_ref[0, win, blk], bias2),
                              (ktc_ref[0, blk, :], vc_ref[0, :, blk], None)])
            chains += _group_chains(sink_ref, q_ref, rows, BLOCK, kvh, ksets)
            groups.append((rows, BLOCK, kvh))
    t0 = pl.multiple_of(pl.program_id(1) * (sub * BLOCK), sub * BLOCK)
    fillers = _conv_fillers(pad_ref, (cw_ref, cb_ref, lg_ref, lb_ref), cv_ref, t0, sub * BLOCK)
    _run_chains(chains, _make_finish(groups, g_ref, o_ref), fillers, ahead)


def _band_bias():
    i = np.arange(BLOCK)[:, None]
    k = np.arange(3 * BLOCK)[None, :]
    tabs = [np.where(np.abs(BLOCK * qb + i - k) <= WINDOW, 0.0, NEG_INF) for qb in range(3)]
    return jnp.asarray(np.stack(tabs).astype(np.float32))


def _local_attention(sink, q, kv, kvc, gatt, uc, conv_args):
    bn, s, _ = q.shape
    n_ctx = kvc[1].shape[1]
    sub = ATT_BLOCKS
    ahead = ATT_AHEAD
    tq = sub * BLOCK
    assert s % tq == 0 and s >= 3 * BLOCK
    tile = lambda w: pl.BlockSpec((1, tq, w), lambda b, n: (b, n, 0))
    whole = lambda rows, w: pl.BlockSpec((1, rows, w), lambda b, n: (b, 0, 0))
    return pl.pallas_call(
        functools.partial(_local_attn_kernel, seq=s, sub=sub, ahead=ahead),
        grid=(bn, s // tq),
        in_specs=[
            pl.BlockSpec(memory_space=pltpu.SMEM),
            tile(ATT_W), whole(KVX_W, s), whole(s, KVX_W),
            pl.BlockSpec((1, KVX_W, n_ctx), lambda b, n: (0, 0, b)), whole(n_ctx, KVX_W),
            pl.BlockSpec((3, BLOCK, 3 * BLOCK), lambda b, n: (0, 0, 0)),
            tile(ATT_W), whole(s, CONV_W),
        ] + _conv_specs(lambda b, n: (0, 0)),
        out_specs=[tile(ATT_W), tile(CONV_W)],
        out_shape=[jax.ShapeDtypeStruct((bn, s, ATT_W), BF16), jax.ShapeDtypeStruct((bn, s, CONV_W), BF16)],
        scratch_shapes=[pltpu.VMEM((s + 2 * CONV_HALO, CONV_W), F32)],
        compiler_params=SEQUENTIAL,
        name="local_attention",
    )(sink, q, *kv, *kvc, _band_bias(), gatt, uc, *conv_args)


def _ctx_attn_kernel(sink_ref, q_ref, ktc_ref, vc_ref, g_ref, uc_ref, cw_ref, cb_ref, lg_ref, lb_ref,
                     o_ref, cv_ref, pad_ref, *, n_ctx, ahead):
    _fill_conv_pad(pad_ref, uc_ref, n_ctx)
    rows = slice(0, n_ctx)
    chains, groups = [], []
    for kvh in range(N_KV_HEADS):
        ksets = []
        for e in range(2):
            blk = _kv_block(kvh, e)
            ksets.append([(ktc_ref[0, blk, :], vc_ref[0, :, blk], None)])
        chains += _group_chains(sink_ref, q_ref, rows, n_ctx, kvh, ksets)
        groups.append((rows, n_ctx, kvh))
    fillers = _conv_fillers(pad_ref, (cw_ref, cb_ref, lg_ref, lb_ref), cv_ref, 0, n_ctx)
    _run_chains(chains, _make_finish(groups, g_ref, o_ref), fillers, ahead)


def _ctx_attention(sink, q, kvc, gatt, uc, conv_args):
    bn, n_ctx, _ = q.shape
    ahead = CTX_AHEAD
    whole = lambda w: pl.BlockSpec((1, n_ctx, w), lambda b: (b, 0, 0))
    return pl.pallas_call(
        functools.partial(_ctx_attn_kernel, n_ctx=n_ctx, ahead=ahead),
        grid=(bn,),
        in_specs=[pl.BlockSpec(memory_space=pltpu.SMEM), whole(ATT_W),
                  pl.BlockSpec((1, KVX_W, n_ctx), lambda b: (0, 0, b)), whole(KVX_W), whole(ATT_W),
                  whole(CONV_W)] + _conv_specs(lambda b: (0, 0)),
        out_specs=[whole(ATT_W), whole(CONV_W)],
        out_shape=[jax.ShapeDtypeStruct((bn, n_ctx, ATT_W), BF16),
                   jax.ShapeDtypeStruct((bn, n_ctx, CONV_W), BF16)],
        scratch_shapes=[pltpu.VMEM((n_ctx + 2 * CONV_HALO, CONV_W), F32)],
        name="ctx_attention",
    )(sink, q, *kvc, gatt, uc, *conv_args)


def _fourier_prepare(ab_ref, cw_ref, sw_ref, rc_ref, rs_ref, half):
    top = ab_ref[0, 0:half, :].astype(F32)
    bot = ab_ref[0, half:2 * half, :].astype(F32)
    plus = top + bot
    minus = top - bot
    a_m, b_m = minus[:, :FOUR_W], minus[:, FOUR_W:]
    cw, sw = cw_ref[...], sw_ref[...]
    rc_ref[:, 0:FOUR_W] = plus[:, :FOUR_W].astype(BF16)
    rc_ref[:, FOUR_W:] = (cw * a_m - sw * b_m).astype(BF16)
    rs_ref[:, 0:FOUR_W] = plus[:, FOUR_W:].astype(BF16)
    rs_ref[:, FOUR_W:] = (sw * a_m + cw * b_m).astype(BF16)


def _fourier_rows(c_ref, s_ref, rc_ref, rs_ref, wf_ref, bf_ref, il_ref, r0, chunk):
    e = _dot(c_ref[pl.ds(r0, chunk), :], rc_ref[...]) - _dot(s_ref[pl.ds(r0, chunk), :], rs_ref[...])
    eb = e.astype(BF16)
    for p in range(2):
        f = _dot(eb[:, p * FOUR_W:(p + 1) * FOUR_W], wf_ref[...])
        for lt in range(FOUR_W // LANES):
            il_ref[lt, pl.ds(p, chunk, stride=2), :] = f[:, lt * LANES:(lt + 1) * LANES]
    return jnp.concatenate([il_ref[lt] for lt in range(FOUR_W // LANES)], axis=1) + bf_ref[...]


def _out_kernel(x_ref, att_ref, cv_ref, gcv_ref, ab_ref, gf_ref, c_ref, s_ref, cw_ref, sw_ref, wf_ref, bf_ref,
                w_ref, gate_ref, fg_ref, o_ref, rc_ref, rs_ref, il_ref, wbf_ref, *, tm, half, final):
    _cast_weight_once(w_ref, wbf_ref)

    @pl.when(pl.program_id(1) == 0)
    def _():
        _fourier_prepare(ab_ref, cw_ref, sw_ref, rc_ref, rs_ref, half)

    r0 = pl.multiple_of(pl.program_id(1) * (tm // 2), tm // 2)
    four = _fourier_rows(c_ref, s_ref, rc_ref, rs_ref, wf_ref, bf_ref, il_ref, r0, tm // 2)
    fo = (four * gf_ref[0].astype(F32)).astype(BF16)
    cv = cv_ref[0] * gcv_ref[0]
    y = _dot(jnp.concatenate([att_ref[0], cv, fo], axis=1), wbf_ref[...])
    xn = x_ref[0] + gate_ref[0] * y
    if final:
        ms = jnp.mean(xn * xn, axis=-1, keepdims=True)
        xn = xn * lax.rsqrt(ms + EPS) * fg_ref[...]
    o_ref[0] = xn


def _out_proj(x, att, cv, gcv, ab, gf, pos_tabs, w_four_bf, b_four, w_out, mod3, final_g,
              *, tm, mod_row, final, layer):
    bn, s, _ = x.shape
    half = s // 2
    c_half, s_half, cw, sw = pos_tabs
    tile = lambda w: pl.BlockSpec((1, tm, w), lambda b, i: (b, i, 0))
    const = lambda shape: pl.BlockSpec(shape, lambda b, i: (0,) * len(shape))
    return pl.pallas_call(
        functools.partial(_out_kernel, tm=tm, half=half, final=final),
        grid=(bn, s // tm),
        in_specs=[
            tile(D_MODEL), tile(ATT_W), tile(CONV_W), tile(CONV_W),
            pl.BlockSpec((1, s, 2 * FOUR_W), lambda b, i: (b, 0, 0)),
            tile(FOUR_W),
            const((half, half)), const((half, half)), const((half, FOUR_W)), const((half, FOUR_W)),
            const((FOUR_W, FOUR_W)), const((1, FOUR_W)),
            pl.BlockSpec((1, D_MODEL, D_MODEL), lambda b, i: (layer, 0, 0)),
            _mod_spec(mod_row, 2),
            const((1, D_MODEL)),
        ],
        out_specs=tile(D_MODEL),
        out_shape=jax.ShapeDtypeStruct((bn, s, D_MODEL), F32),
        scratch_shapes=[pltpu.VMEM((half, 2 * FOUR_W), BF16), pltpu.VMEM((half, 2 * FOUR_W), BF16),
                        pltpu.VMEM((FOUR_W // LANES, tm, LANES), F32),
                        pltpu.VMEM((D_MODEL, D_MODEL), BF16)],
        compiler_params=SEQUENTIAL,
        name="out_proj",
    )(x, att, cv, gcv, ab, gf, c_half, s_half, cw, sw, w_four_bf, b_four.reshape(1, FOUR_W),
      w_out, mod3, final_g.reshape(1, D_MODEL))


def _rope_tables(s):
    t = np.arange(s)
    half = HEAD_DIM // 4
    freqs = ROPE_BASE ** (-np.arange(half) / half)
    zero = np.zeros((s, half))

    def per_head(fn_lo, fn_hi):
        parts = []
        for pos in (t // GRID_W, t % GRID_W):
            ang = pos[:, None] * freqs[None, :]
            parts += [fn_lo(ang), fn_hi(ang)]
        return np.tile(np.concatenate(parts, axis=-1), (1, LANES // HEAD_DIM))

    cos = per_head(np.cos, np.cos)
    sin_lo = per_head(lambda a: -np.sin(a), lambda a: zero)
    sin_hi = per_head(lambda a: zero, np.sin)
    tab = np.concatenate([cos * Q_SCALE, sin_lo * Q_SCALE, sin_hi * Q_SCALE, cos, sin_lo, sin_hi], axis=-1)
    return jnp.asarray(tab.astype(np.float32))


def _identity_rope_tables(s):
    one = np.ones((s, LANES))
    zero = np.zeros((s, LANES))
    return jnp.asarray(np.concatenate([one * Q_SCALE, zero, zero, one, zero, zero], axis=-1).astype(np.float32))


def _channel_dft(n_pos):
    c = np.arange(FOUR_HEAD_DIM)
    ang = 2.0 * np.pi * ((c[:, None] * c[None, :]) % FOUR_HEAD_DIM) / FOUR_HEAD_DIM
    eye = np.eye(FOUR_HEADS)
    scale = 1.0 / np.sqrt(float(n_pos * FOUR_HEAD_DIM))
    tab = np.concatenate([np.kron(eye, np.cos(ang)), np.kron(eye, np.sin(ang))], axis=1) * scale
    return jnp.asarray(tab.astype(np.float32))


def _position_dft(n_pos):
    half = n_pos // 2
    k = np.arange(half)
    ang = 2.0 * np.pi * ((k[:, None] * k[None, :]) % half) / half
    beta = np.pi * k / half
    ones = np.ones((1, FOUR_W))
    f32 = lambda a: jnp.asarray(a.astype(np.float32))
    return (f32(np.cos(ang)).astype(BF16), f32(np.sin(ang)).astype(BF16),
            f32(np.cos(beta)[:, None] * ones), f32(np.sin(beta)[:, None] * ones))


def kernel(x, c, ctx, c_ctx, w_ada, b_ada, norm_g, w_in, attn_sink, conv_w, conv_b,
           conv_ln_g, conv_ln_b, w_four, b_four, w_out, final_g):
    bn, s, _ = x.shape
    n_ctx = ctx.shape[1]
    assert bn < MOD_ROWS
    ctx_row = bn

    cc = jnp.zeros((MOD_ROWS, D_MODEL), F32).at[:bn].set(c).at[ctx_row].set(c_ctx)
    mod = _modulation(cc, w_ada, b_ada)

    w_four_bf = w_four.astype(BF16)
    rope_x = _rope_tables(s)
    rope_c = _identity_rope_tables(bn * n_ctx)
    dft_x = _channel_dft(s).astype(BF16)
    dft_ctx = _channel_dft(n_ctx).astype(BF16)
    pos_x = _position_dft(s)
    pos_ctx = _position_dft(n_ctx)

    h_ctx = ctx
    for l in range(DEPTH):
        mod3 = mod[l].reshape(MOD_ROWS, 1, 3 * D_MODEL)
        conv_args = _conv_args(conv_w[l], conv_b[l], conv_ln_g[l], conv_ln_b[l])
        four_args = (w_four_bf[l], b_four[l], w_out, mod3, final_g)
        sink = attn_sink[l]
        last = l == DEPTH - 1
        ctx_flat = h_ctx.reshape(1, bn * n_ctx, D_MODEL)
        per_batch = lambda a: a.reshape(bn, n_ctx, a.shape[-1])
        if not last:
            qc, ktc, *rest = _in_proj(ctx_flat, mod3, norm_g[l], w_in, rope_c, dft_ctx,
                                      tm=IN_TM, mod_row=ctx_row, layer=l)
            vc, gatt_c, gcv_c, ab_c, gf_c, uc_c = map(per_batch, rest)
            kvc = (ktc, vc)
            att_c, cv_c = _ctx_attention(sink, per_batch(qc), kvc, gatt_c, uc_c, conv_args)
        else:
            ktc, vc = _kv_proj(ctx_flat, mod3, norm_g[l], w_in, tm=IN_TM, mod_row=ctx_row, layer=l)
            kvc = (ktc, per_batch(vc))
        q, kt, v, gatt, gcv, ab, gf, uc = _in_proj(
            x, mod3, norm_g[l], w_in, rope_x, dft_x, tm=IN_TM, mod_row=None, layer=l)
        att, cv = _local_attention(sink, q, (kt, v), kvc, gatt, uc, conv_args)
        if not last:
            h_ctx = _out_proj(h_ctx, att_c, cv_c, gcv_c, ab_c, gf_c, pos_ctx, *four_args,
                              tm=n_ctx, mod_row=ctx_row, final=False, layer=l)
        x = _out_proj(x, att, cv, gcv, ab, gf, pos_x, *four_args, tm=OUT_TM, mod_row=None, final=last, layer=l)
    return x
```

```python
import functools

import numpy as np
import jax
import jax.numpy as jnp
from jax import lax
from jax.experimental import pallas as pl
from jax.experimental.pallas import tpu as pltpu

F32 = jnp.float32
BF16 = jnp.bfloat16

D_MODEL = 1024
DEPTH = 2
GRID_W = 64
HEAD_DIM = 64
ATT_W = 512
N_HEADS = 8
N_KV_HEADS = 2
GQA_GROUP = N_HEADS // N_KV_HEADS
KV_W = N_KV_HEADS * HEAD_DIM
CONV_W = 256
FOUR_W = 256
FOUR_HEADS = 4
FOUR_HEAD_DIM = FOUR_W // FOUR_HEADS
CONV_K = 31
SUBLANES = 8
CONV_HALO = 16
WINDOW = 128
BLOCK = 128
ROPE_BASE = 10000.0
EPS = 1e-6
NEG_INF = -1e30
LOG2E = 1.4426950408889634
Q_SCALE = HEAD_DIM ** -0.5 * LOG2E
IN_W = 2 * ATT_W + 2 * KV_W + 3 * CONV_W + 2 * FOUR_W
OFF_Q = 0
OFF_KV = ATT_W
OFF_GATT = OFF_KV + 2 * KV_W
OFF_CONV = OFF_GATT + ATT_W
OFF_GCONV = OFF_CONV + 2 * CONV_W
OFF_FOUR = OFF_GCONV + CONV_W
OFF_GFOUR = OFF_FOUR + FOUR_W
LANES = 128
KVX_W = 4 * LANES
MOD_ROWS = 16
SEQUENTIAL = pltpu.CompilerParams(dimension_semantics=("arbitrary", "arbitrary"))
ADA_TN = 1536
IN_TM = 1024
OUT_TM = 1024
ATT_BLOCKS = 8
ATT_AHEAD = 8
CTX_AHEAD = 2


def _dot(a, b):
    return jnp.dot(a, b, preferred_element_type=F32)


def _silu(x):
    return x * jax.nn.sigmoid(x)


def _ada_kernel(cc_ref, w_ref, b_ref, o_ref):
    a = _silu(cc_ref[...])
    w = w_ref[0]
    a_hi = a.astype(BF16)
    a_lo = (a - a_hi.astype(F32)).astype(BF16)
    w_hi = w.astype(BF16)
    o_ref[0] = _dot(a_hi, w_hi) + _dot(a_lo, w_hi) + b_ref[0]


def _modulation(cc, w_ada, b_ada):
    tn = ADA_TN
    return pl.pallas_call(
        _ada_kernel,
        grid=(DEPTH, 3 * D_MODEL // tn),
        in_specs=[
            pl.BlockSpec((MOD_ROWS, D_MODEL), lambda l, j: (0, 0)),
            pl.BlockSpec((1, D_MODEL, tn), lambda l, j: (l, 0, j)),
            pl.BlockSpec((1, 1, tn), lambda l, j: (l, 0, j)),
        ],
        out_specs=pl.BlockSpec((1, MOD_ROWS, tn), lambda l, j: (l, 0, j)),
        out_shape=jax.ShapeDtypeStruct((DEPTH, MOD_ROWS, 3 * D_MODEL), F32),
        name="ada_modulation",
    )(cc, w_ada, b_ada.reshape(DEPTH, 1, 3 * D_MODEL))


def _zero_after(token):
    bits = pltpu.bitcast(jnp.broadcast_to(token[:SUBLANES], (SUBLANES, LANES)), jnp.uint32)
    zero = lax.shift_right_logical(lax.shift_right_logical(bits, jnp.uint32(16)), jnp.uint32(16))
    return pltpu.bitcast(zero, F32)[:1]


def _conv_tile(pad_ref, w_ref, t0, lt, zero=None):
    lead = CONV_HALO - CONV_K // 2
    lanes = slice(lt * LANES, (lt + 1) * LANES)
    acc = None
    for r in range(SUBLANES):
        z = None
        for a in range((CONV_K - r + SUBLANES - 1) // SUBLANES):
            j = SUBLANES * a + r
            w = w_ref[j:j + 1, lanes]
            term = pad_ref[pl.ds(t0 + SUBLANES * a, BLOCK + SUBLANES), lanes] * (w if zero is None else w + zero)
            z = term if z is None else z + term
        shift = r + lead
        zs = z[shift:shift + BLOCK] if shift % SUBLANES == 0 else pltpu.roll(z, BLOCK + SUBLANES - shift, 0)[:BLOCK]
        acc = zs if acc is None else acc + zs
    return acc


def _conv_finish(tiles, b_ref, lg_ref, lb_ref):
    y = jnp.concatenate(tiles, axis=1) + b_ref[...]
    mu = jnp.mean(y, axis=-1, keepdims=True)
    yc = y - mu
    var = jnp.mean(yc * yc, axis=-1, keepdims=True)
    return _silu(yc * lax.rsqrt(var + EPS) * lg_ref[...] + lb_ref[...]).astype(BF16)


def _fill_conv_pad(pad_ref, u_ref, seq):
    zeros = jnp.zeros((CONV_HALO, CONV_W), F32)
    pad_ref[0:CONV_HALO, :] = zeros
    pad_ref[CONV_HALO + seq:2 * CONV_HALO + seq, :] = zeros
    pad_ref[CONV_HALO:CONV_HALO + seq, :] = u_ref[0].astype(F32)


def _conv_fillers(pad_ref, conv_refs, cv_ref, t0, n_rows):
    w_ref, b_ref, lg_ref, lb_ref = conv_refs
    pieces = []
    for j in range(n_rows // BLOCK):
        tiles = []
        for lt in range(CONV_W // LANES):
            pieces.append(lambda token, j=j, lt=lt, tiles=tiles: tiles.append(
                _conv_tile(pad_ref, w_ref, t0 + j * BLOCK, lt, _zero_after(token))))

        def finish(token, j=j, tiles=tiles):
            cv_ref[0, j * BLOCK:(j + 1) * BLOCK, :] = _conv_finish(tiles, b_ref, lg_ref, lb_ref)
        pieces.append(finish)
    return pieces


def _conv_specs(index):
    vec = pl.BlockSpec((1, CONV_W), index)
    return [pl.BlockSpec((CONV_K, CONV_W), index), vec, vec, vec]


def _conv_args(conv_w, conv_b, ln_g, ln_b):
    row = lambda a: a.reshape(1, CONV_W)
    return conv_w, row(conv_b), row(ln_g), row(ln_b)


def _cast_weight_once(w_ref, wbf_ref):
    @pl.when((pl.program_id(0) == 0) & (pl.program_id(1) == 0))
    def _():
        wbf_ref[...] = w_ref[0].astype(BF16)


def _project(x_ref, sh_ref, sc_ref, g_ref, w_ref, wbf_ref):
    _cast_weight_once(w_ref, wbf_ref)
    x = x_ref[0]
    ms = jnp.mean(x * x, axis=-1, keepdims=True)
    gain = g_ref[...] * (1.0 + sc_ref[0])
    h = x * lax.rsqrt(ms + EPS) * gain + sh_ref[0]
    return _dot(h.astype(BF16), wbf_ref[...])


def _rope_block(blk, rope_ref, base):
    cos = rope_ref[:, base:base + LANES]
    sin_lo = rope_ref[:, base + LANES:base + 2 * LANES]
    sin_hi = rope_ref[:, base + 2 * LANES:base + 3 * LANES]
    return (blk * cos + pltpu.roll(blk, LANES - 16, 1) * sin_lo + pltpu.roll(blk, 16, 1) * sin_hi)


def _store_padded_kv(kt_ref, v_ref, kblk, vblk):
    tm = kblk.shape[0]
    kt = kblk.T.astype(BF16)
    zeros = jnp.zeros((HEAD_DIM, tm), BF16)
    k_blocks = ((kt[:HEAD_DIM], zeros), (zeros, kt[:HEAD_DIM]), (kt[HEAD_DIM:], zeros), (zeros, kt[HEAD_DIM:]))
    for i, halves in enumerate(k_blocks):
        kt_ref[0, i * LANES:(i + 1) * LANES, :] = jnp.concatenate(halves, axis=0)
    lo = lax.broadcasted_iota(jnp.int32, vblk.shape, 1) < HEAD_DIM
    swapped = pltpu.roll(vblk, HEAD_DIM, 1)
    v_blocks = (jnp.where(lo, vblk, 0.0), jnp.where(lo, 0.0, swapped),
                jnp.where(lo, swapped, 0.0), jnp.where(lo, 0.0, vblk))
    for i, b in enumerate(v_blocks):
        v_ref[0, :, i * LANES:(i + 1) * LANES] = b.astype(BF16)


def _project_epilogue(p, rope_ref, dft_ref, q_ref, kt_ref, v_ref, gatt_ref, gcv_ref, ab_ref, gf_ref):
    for c in range(ATT_W // LANES):
        blk = _rope_block(p[:, OFF_Q + c * LANES:OFF_Q + (c + 1) * LANES], rope_ref, 0)
        q_ref[0, :, c * LANES:(c + 1) * LANES] = blk.astype(BF16)
    _store_padded_kv(kt_ref, v_ref, _rope_block(p[:, OFF_KV:OFF_KV + KV_W], rope_ref, 3 * LANES),
                     p[:, OFF_KV + KV_W:OFF_KV + 2 * KV_W])
    gatt_ref[0] = _silu(p[:, OFF_GATT:OFF_GATT + ATT_W]).astype(BF16)
    gcv_ref[0] = _silu(p[:, OFF_GCONV:OFF_GCONV + CONV_W]).astype(BF16)
    fu = p[:, OFF_FOUR:OFF_FOUR + FOUR_W].astype(BF16)
    ab_ref[0] = _dot(fu, dft_ref[...]).astype(BF16)
    gf_ref[0] = _silu(p[:, OFF_GFOUR:OFF_GFOUR + FOUR_W]).astype(BF16)
    return (p[:, OFF_CONV:OFF_CONV + CONV_W] * jax.nn.sigmoid(p[:, OFF_CONV + CONV_W:OFF_CONV + 2 * CONV_W])
            ).astype(BF16)


def _in_kernel(x_ref, sh_ref, sc_ref, g_ref, w_ref, rope_ref, dft_ref,
               q_ref, kt_ref, v_ref, gatt_ref, gcv_ref, ab_ref, gf_ref, uc_ref, wbf_ref):
    p = _project(x_ref, sh_ref, sc_ref, g_ref, w_ref, wbf_ref)
    uc_ref[0] = _project_epilogue(p, rope_ref, dft_ref, q_ref, kt_ref, v_ref, gatt_ref, gcv_ref, ab_ref, gf_ref)


def _mod_spec(mod_row, j):
    row = (lambda b: mod_row) if mod_row is not None else (lambda b: b)
    return pl.BlockSpec((1, 1, D_MODEL), lambda b, i: (row(b), 0, j))


def _in_specs(tm, mod_row, layer):
    return [
        pl.BlockSpec((1, tm, D_MODEL), lambda b, i: (b, i, 0)),
        _mod_spec(mod_row, 0), _mod_spec(mod_row, 1),
        pl.BlockSpec((1, D_MODEL), lambda b, i: (0, 0)),
        pl.BlockSpec((1, D_MODEL, IN_W), lambda b, i: (layer, 0, 0)),
        pl.BlockSpec((tm, 6 * LANES), lambda b, i: (i, 0)),
        pl.BlockSpec((FOUR_W, 2 * FOUR_W), lambda b, i: (0, 0)),
    ]


def _in_outs(bn, s, tm):
    widths = (ATT_W, None, KVX_W, ATT_W, CONV_W, 2 * FOUR_W, FOUR_W, CONV_W)
    specs = [pl.BlockSpec((1, tm, w), lambda b, i: (b, i, 0)) if w else
             pl.BlockSpec((1, KVX_W, tm), lambda b, i: (b, 0, i)) for w in widths]
    shapes = [jax.ShapeDtypeStruct((bn, s, w) if w else (bn, KVX_W, s), BF16) for w in widths]
    return specs, shapes


def _in_proj(x, mod3, norm_g, w, rope_tab, dft_c, *, tm, mod_row, layer):
    bn, s, _ = x.shape
    specs, shapes = _in_outs(bn, s, tm)
    return pl.pallas_call(
        _in_kernel,
        grid=(bn, s // tm),
        in_specs=_in_specs(tm, mod_row, layer),
        out_specs=specs,
        out_shape=shapes,
        scratch_shapes=[pltpu.VMEM((D_MODEL, IN_W), BF16)],
        compiler_params=SEQUENTIAL,
        name="in_proj",
    )(x, mod3, mod3, norm_g.reshape(1, D_MODEL), w, rope_tab, dft_c)


def _kv_kernel(x_ref, sh_ref, sc_ref, g_ref, w_ref, kt_ref, v_ref, wbf_ref):
    kvp = _project(x_ref, sh_ref, sc_ref, g_ref, w_ref, wbf_ref)
    _store_padded_kv(kt_ref, v_ref, kvp[:, 0:KV_W], kvp[:, KV_W:2 * KV_W])


def _kv_proj(x, mod3, norm_g, w, *, tm, mod_row, layer):
    bn, s, _ = x.shape
    assert OFF_KV % (2 * KV_W) == 0
    return pl.pallas_call(
        _kv_kernel,
        grid=(bn, s // tm),
        in_specs=[
            pl.BlockSpec((1, tm, D_MODEL), lambda b, i: (b, i, 0)),
            _mod_spec(mod_row, 0), _mod_spec(mod_row, 1),
            pl.BlockSpec((1, D_MODEL), lambda b, i: (0, 0)),
            pl.BlockSpec((1, D_MODEL, 2 * KV_W), lambda b, i: (layer, 0, OFF_KV // (2 * KV_W))),
        ],
        out_specs=[pl.BlockSpec((1, KVX_W, tm), lambda b, i: (b, 0, i)),
                   pl.BlockSpec((1, tm, KVX_W), lambda b, i: (b, i, 0))],
        out_shape=[jax.ShapeDtypeStruct((bn, KVX_W, s), BF16), jax.ShapeDtypeStruct((bn, s, KVX_W), BF16)],
        scratch_shapes=[pltpu.VMEM((D_MODEL, 2 * KV_W), BF16)],
        compiler_params=SEQUENTIAL,
        name="ctx_kv_proj",
    )(x, mod3, mod3, norm_g.reshape(1, D_MODEL), w)


class _Chain:
    def __init__(self, q2, sink_col, kset):
        self.q2, self.sink_col, self.kset = q2, sink_col, kset

    def scores(self):
        self.s = []
        for kt, _, bias in self.kset:
            sc = _dot(self.q2, kt)
            self.s.append(sc if bias is None else sc + bias)

    def softmax(self):
        m = self.sink_col
        for sc in self.s:
            m = jnp.maximum(m, sc.max(axis=-1, keepdims=True))
        self.p = [jnp.exp2(sc - m).astype(BF16) for sc in self.s]
        self.extra = jnp.exp2(self.sink_col - m)
        self.row_max = m
        self.s = None

    def values(self):
        r = None
        for p, (_, v, _) in zip(self.p, self.kset):
            pv = _dot(p, jnp.concatenate([v, jnp.ones_like(v)], axis=1))
            r = pv if r is None else r + pv
        self.p = None
        return r[:, :LANES] / (r[:, LANES:] + self.extra)


def _run_chains(chains, finish, fillers, ahead):
    n = len(chains)
    fillers = list(fillers)
    for i in range(-ahead, n):
        if 0 <= i < n:
            finish(i, chains[i].values())
        if 0 <= i + ahead < n:
            chains[i + ahead].scores()
        if 0 <= i + 1 < n:
            chains[i + 1].softmax()
            if fillers:
                fillers.pop(0)(chains[i + 1].row_max)
    for f in fillers:
        f(chains[-1].row_max)


def _kv_block(kvh, e):
    k0 = (2 * kvh + e) * LANES
    return slice(k0, k0 + LANES)


def _group_chains(sink_ref, q_ref, rows, tq, kvh, ksets):
    c0 = 2 * kvh * LANES
    q2 = jnp.concatenate([q_ref[0, rows, c0:c0 + LANES], q_ref[0, rows, c0 + LANES:c0 + 2 * LANES]], axis=0)
    chains = []
    for e in range(2):
        sink_col = jnp.concatenate([jnp.full((tq, 1), sink_ref[GQA_GROUP * kvh + e] * LOG2E, F32),
                                    jnp.full((tq, 1), sink_ref[GQA_GROUP * kvh + 2 + e] * LOG2E, F32)], axis=0)
        chains.append(_Chain(q2, sink_col, ksets[e]))
    return chains


def _make_finish(groups, g_ref, o_ref):
    partial = {}

    def finish(i, out):
        if i % 2 == 0:
            partial[i // 2] = out
            return
        rows, tq, kvh = groups[i // 2]
        o = partial.pop(i // 2) + out
        c0 = 2 * kvh * LANES
        for hf in range(2):
            cols = slice(c0 + hf * LANES, c0 + (hf + 1) * LANES)
            g = g_ref[0, rows, cols].astype(F32)
            o_ref[0, rows, cols] = (o[hf * tq:(hf + 1) * tq] * g).astype(BF16)
    return finish


def _local_attn_kernel(sink_ref, q_ref, kt_ref, v_ref, ktc_ref, vc_ref, bias_ref, g_ref, uc_ref,
                       cw_ref, cb_ref, lg_ref, lb_ref, o_ref, cv_ref, pad_ref, *, seq, sub, ahead):
    @pl.when(pl.program_id(1) == 0)
    def _():
        _fill_conv_pad(pad_ref, uc_ref, seq)

    nblk = seq // BLOCK
    chains, groups = [], []
    for sb in range(sub):
        n = pl.program_id(1) * sub + sb
        start = pl.multiple_of(jnp.clip((n - 1) * BLOCK, 0, seq - 3 * BLOCK), BLOCK)
        win = pl.ds(start, 3 * BLOCK)
        bias = bias_ref[jnp.where(n == 0, 0, jnp.where(n == nblk - 1, 2, 1))]
        bias2 = jnp.concatenate([bias, bias], axis=0)
        rows = slice(sb * BLOCK, (sb + 1) * BLOCK)
        for kvh in range(N_KV_HEADS):
            ksets = []
            for e in range(2):
                blk = _kv_block(kvh, e)
                ksets.append([(kt_ref[0, blk, win], v_ref[0, win, blk], bias2),
                              (ktc_ref[0, blk, :], vc_ref[0, :, blk], None)])
            chains += _group_chains(sink_ref, q_ref, rows, BLOCK, kvh, ksets)
            groups.append((rows, BLOCK, kvh))
    t0 = pl.multiple_of(pl.program_id(1) * (sub * BLOCK), sub * BLOCK)
    fillers = _conv_fillers(pad_ref, (cw_ref, cb_ref, lg_ref, lb_ref), cv_ref, t0, sub * BLOCK)
    _run_chains(chains, _make_finish(groups, g_ref, o_ref), fillers, ahead)


def _band_bias():
    i = np.arange(BLOCK)[:, None]
    k = np.arange(3 * BLOCK)[None, :]
    tabs = [np.where(np.abs(BLOCK * qb + i - k) <= WINDOW, 0.0, NEG_INF) for qb in range(3)]
    return jnp.asarray(np.stack(tabs).astype(np.float32))


def _local_attention(sink, q, kv, kvc, gatt, uc, conv_args):
    bn, s, _ = q.shape
    n_ctx = kvc[1].shape[1]
    sub = ATT_BLOCKS
    ahead = ATT_AHEAD
    tq = sub * BLOCK
    assert s % tq == 0 and s >= 3 * BLOCK
    tile = lambda w: pl.BlockSpec((1, tq, w), lambda b, n: (b, n, 0))
    whole = lambda rows, w: pl.BlockSpec((1, rows, w), lambda b, n: (b, 0, 0))
    return pl.pallas_call(
        functools.partial(_local_attn_kernel, seq=s, sub=sub, ahead=ahead),
        grid=(bn, s // tq),
        in_specs=[
            pl.BlockSpec(memory_space=pltpu.SMEM),
            tile(ATT_W), whole(KVX_W, s), whole(s, KVX_W),
            pl.BlockSpec((1, KVX_W, n_ctx), lambda b, n: (0, 0, b)), whole(n_ctx, KVX_W),
            pl.BlockSpec((3, BLOCK, 3 * BLOCK), lambda b, n: (0, 0, 0)),
            tile(ATT_W), whole(s, CONV_W),
        ] + _conv_specs(lambda b, n: (0, 0)),
        out_specs=[tile(ATT_W), tile(CONV_W)],
        out_shape=[jax.ShapeDtypeStruct((bn, s, ATT_W), BF16), jax.ShapeDtypeStruct((bn, s, CONV_W), BF16)],
        scratch_shapes=[pltpu.VMEM((s + 2 * CONV_HALO, CONV_W), F32)],
        compiler_params=SEQUENTIAL,
        name="local_attention",
    )(sink, q, *kv, *kvc, _band_bias(), gatt, uc, *conv_args)


def _ctx_attn_kernel(sink_ref, q_ref, ktc_ref, vc_ref, g_ref, uc_ref, cw_ref, cb_ref, lg_ref, lb_ref,
                     o_ref, cv_ref, pad_ref, *, n_ctx, ahead):
    _fill_conv_pad(pad_ref, uc_ref, n_ctx)
    rows = slice(0, n_ctx)
    chains, groups = [], []
    for kvh in range(N_KV_HEADS):
        ksets = []
        for e in range(2):
            blk = _kv_block(kvh, e)
            ksets.append([(ktc_ref[0, blk, :], vc_ref[0, :, blk], None)])
        chains += _group_chains(sink_ref, q_ref, rows, n_ctx, kvh, ksets)
        groups.append((rows, n_ctx, kvh))
    fillers = _conv_fillers(pad_ref, (cw_ref, cb_ref, lg_ref, lb_ref), cv_ref, 0, n_ctx)
    _run_chains(chains, _make_finish(groups, g_ref, o_ref), fillers, ahead)


def _ctx_attention(sink, q, kvc, gatt, uc, conv_args):
    bn, n_ctx, _ = q.shape
    ahead = CTX_AHEAD
    whole = lambda w: pl.BlockSpec((1, n_ctx, w), lambda b: (b, 0, 0))
    return pl.pallas_call(
        functools.partial(_ctx_attn_kernel, n_ctx=n_ctx, ahead=ahead),
        grid=(bn,),
        in_specs=[pl.BlockSpec(memory_space=pltpu.SMEM), whole(ATT_W),
                  pl.BlockSpec((1, KVX_W, n_ctx), lambda b: (0, 0, b)), whole(KVX_W), whole(ATT_W),
                  whole(CONV_W)] + _conv_specs(lambda b: (0, 0)),
        out_specs=[whole(ATT_W), whole(CONV_W)],
        out_shape=[jax.ShapeDtypeStruct((bn, n_ctx, ATT_W), BF16),
                   jax.ShapeDtypeStruct((bn, n_ctx, CONV_W), BF16)],
        scratch_shapes=[pltpu.VMEM((n_ctx + 2 * CONV_HALO, CONV_W), F32)],
        name="ctx_attention",
    )(sink, q, *kvc, gatt, uc, *conv_args)


def _fourier_prepare(ab_ref, cw_ref, sw_ref, rc_ref, rs_ref, half):
    top = ab_ref[0, 0:half, :].astype(F32)
    bot = ab_ref[0, half:2 * half, :].astype(F32)
    plus = top + bot
    minus = top - bot
    a_m, b_m = minus[:, :FOUR_W], minus[:, FOUR_W:]
    cw, sw = cw_ref[...], sw_ref[...]
    rc_ref[:, 0:FOUR_W] = plus[:, :FOUR_W].astype(BF16)
    rc_ref[:, FOUR_W:] = (cw * a_m - sw * b_m).astype(BF16)
    rs_ref[:, 0:FOUR_W] = plus[:, FOUR_W:].astype(BF16)
    rs_ref[:, FOUR_W:] = (sw * a_m + cw * b_m).astype(BF16)


def _fourier_rows(c_ref, s_ref, rc_ref, rs_ref, wf_ref, bf_ref, il_ref, r0, chunk):
    e = _dot(c_ref[pl.ds(r0, chunk), :], rc_ref[...]) - _dot(s_ref[pl.ds(r0, chunk), :], rs_ref[...])
    eb = e.astype(BF16)
    for p in range(2):
        f = _dot(eb[:, p * FOUR_W:(p + 1) * FOUR_W], wf_ref[...])
        for lt in range(FOUR_W // LANES):
            il_ref[lt, pl.ds(p, chunk, stride=2), :] = f[:, lt * LANES:(lt + 1) * LANES]
    return jnp.concatenate([il_ref[lt] for lt in range(FOUR_W // LANES)], axis=1) + bf_ref[...]


def _out_kernel(x_ref, att_ref, cv_ref, gcv_ref, ab_ref, gf_ref, c_ref, s_ref, cw_ref, sw_ref, wf_ref, bf_ref,
                w_ref, gate_ref, fg_ref, o_ref, rc_ref, rs_ref, il_ref, wbf_ref, *, tm, half, final):
    _cast_weight_once(w_ref, wbf_ref)

    @pl.when(pl.program_id(1) == 0)
    def _():
        _fourier_prepare(ab_ref, cw_ref, sw_ref, rc_ref, rs_ref, half)

    r0 = pl.multiple_of(pl.program_id(1) * (tm // 2), tm // 2)
    four = _fourier_rows(c_ref, s_ref, rc_ref, rs_ref, wf_ref, bf_ref, il_ref, r0, tm // 2)
    fo = (four * gf_ref[0].astype(F32)).astype(BF16)
    cv = cv_ref[0] * gcv_ref[0]
    y = _dot(jnp.concatenate([att_ref[0], cv, fo], axis=1), wbf_ref[...])
    xn = x_ref[0] + gate_ref[0] * y
    if final:
        ms = jnp.mean(xn * xn, axis=-1, keepdims=True)
        xn = xn * lax.rsqrt(ms + EPS) * fg_ref[...]
    o_ref[0] = xn


def _out_proj(x, att, cv, gcv, ab, gf, pos_tabs, w_four_bf, b_four, w_out, mod3, final_g,
              *, tm, mod_row, final, layer):
    bn, s, _ = x.shape
    half = s // 2
    c_half, s_half, cw, sw = pos_tabs
    tile = lambda w: pl.BlockSpec((1, tm, w), lambda b, i: (b, i, 0))
    const = lambda shape: pl.BlockSpec(shape, lambda b, i: (0,) * len(shape))
    return pl.pallas_call(
        functools.partial(_out_kernel, tm=tm, half=half, final=final),
        grid=(bn, s // tm),
        in_specs=[
            tile(D_MODEL), tile(ATT_W), tile(CONV_W), tile(CONV_W),
            pl.BlockSpec((1, s, 2 * FOUR_W), lambda b, i: (b, 0, 0)),
            tile(FOUR_W),
            const((half, half)), const((half, half)), const((half, FOUR_W)), const((half, FOUR_W)),
            const((FOUR_W, FOUR_W)), const((1, FOUR_W)),
            pl.BlockSpec((1, D_MODEL, D_MODEL), lambda b, i: (layer, 0, 0)),
            _mod_spec(mod_row, 2),
            const((1, D_MODEL)),
        ],
        out_specs=tile(D_MODEL),
        out_shape=jax.ShapeDtypeStruct((bn, s, D_MODEL), F32),
        scratch_shapes=[pltpu.VMEM((half, 2 * FOUR_W), BF16), pltpu.VMEM((half, 2 * FOUR_W), BF16),
                        pltpu.VMEM((FOUR_W // LANES, tm, LANES), F32),
                        pltpu.VMEM((D_MODEL, D_MODEL), BF16)],
        compiler_params=SEQUENTIAL,
        name="out_proj",
    )(x, att, cv, gcv, ab, gf, c_half, s_half, cw, sw, w_four_bf, b_four.reshape(1, FOUR_W),
      w_out, mod3, final_g.reshape(1, D_MODEL))


def _rope_tables(s):
    t = np.arange(s)
    half = HEAD_DIM // 4
    freqs = ROPE_BASE ** (-np.arange(half) / half)
    zero = np.zeros((s, half))

    def per_head(fn_lo, fn_hi):
        parts = []
        for pos in (t // GRID_W, t % GRID_W):
            ang = pos[:, None] * freqs[None, :]
            parts += [fn_lo(ang), fn_hi(ang)]
        return np.tile(np.concatenate(parts, axis=-1), (1, LANES // HEAD_DIM))

    cos = per_head(np.cos, np.cos)
    sin_lo = per_head(lambda a: -np.sin(a), lambda a: zero)
    sin_hi = per_head(lambda a: zero, np.sin)
    tab = np.concatenate([cos * Q_SCALE, sin_lo * Q_SCALE, sin_hi * Q_SCALE, cos, sin_lo, sin_hi], axis=-1)
    return jnp.asarray(tab.astype(np.float32))


def _identity_rope_tables(s):
    one = np.ones((s, LANES))
    zero = np.zeros((s, LANES))
    return jnp.asarray(np.concatenate([one * Q_SCALE, zero, zero, one, zero, zero], axis=-1).astype(np.float32))


def _channel_dft(n_pos):
    c = np.arange(FOUR_HEAD_DIM)
    ang = 2.0 * np.pi * ((c[:, None] * c[None, :]) % FOUR_HEAD_DIM) / FOUR_HEAD_DIM
    eye = np.eye(FOUR_HEADS)
    scale = 1.0 / np.sqrt(float(n_pos * FOUR_HEAD_DIM))
    tab = np.concatenate([np.kron(eye, np.cos(ang)), np.kron(eye, np.sin(ang))], axis=1) * scale
    return jnp.asarray(tab.astype(np.float32))


def _position_dft(n_pos):
    half = n_pos // 2
    k = np.arange(half)
    ang = 2.0 * np.pi * ((k[:, None] * k[None, :]) % half) / half
    beta = np.pi * k / half
    ones = np.ones((1, FOUR_W))
    f32 = lambda a: jnp.asarray(a.astype(np.float32))
    return (f32(np.cos(ang)).astype(BF16), f32(np.sin(ang)).astype(BF16),
            f32(np.cos(beta)[:, None] * ones), f32(np.sin(beta)[:, None] * ones))


def kernel(x, c, ctx, c_ctx, w_ada, b_ada, norm_g, w_in, attn_sink, conv_w, conv_b,
           conv_ln_g, conv_ln_b, w_four, b_four, w_out, final_g):
    bn, s, _ = x.shape
    n_ctx = ctx.shape[1]
    assert bn < MOD_ROWS
    ctx_row = bn

    cc = jnp.zeros((MOD_ROWS, D_MODEL), F32).at[:bn].set(c).at[ctx_row].set(c_ctx)
    mod = _modulation(cc, w_ada, b_ada)

    w_four_bf = w_four.astype(BF16)
    rope_x = _rope_tables(s)
    rope_c = _identity_rope_tables(bn * n_ctx)
    dft_x = _channel_dft(s).astype(BF16)
    dft_ctx = _channel_dft(n_ctx).astype(BF16)
    pos_x = _position_dft(s)
    pos_ctx = _position_dft(n_ctx)

    h_ctx = ctx
    for l in range(DEPTH):
        mod3 = mod[l].reshape(MOD_ROWS, 1, 3 * D_MODEL)
        conv_args = _conv_args(conv_w[l], conv_b[l], conv_ln_g[l], conv_ln_b[l])
        four_args = (w_four_bf[l], b_four[l], w_out, mod3, final_g)
        sink = attn_sink[l]
        last = l == DEPTH - 1
        ctx_flat = h_ctx.reshape(1, bn * n_ctx, D_MODEL)
        per_batch = lambda a: a.reshape(bn, n_ctx, a.shape[-1])
        if not last:
            qc, ktc, *rest = _in_proj(ctx_flat, mod3, norm_g[l], w_in, rope_c, dft_ctx,
                                      tm=IN_TM, mod_row=ctx_row, layer=l)
            vc, gatt_c, gcv_c, ab_c, gf_c, uc_c = map(per_batch, rest)
            kvc = (ktc, vc)
            att_c, cv_c = _ctx_attention(sink, per_batch(qc), kvc, gatt_c, uc_c, conv_args)
        else:
            ktc, vc = _kv_proj(ctx_flat, mod3, norm_g[l], w_in, tm=IN_TM, mod_row=ctx_row, layer=l)
            kvc = (ktc, per_batch(vc))
        q, kt, v, gatt, gcv, ab, gf, uc = _in_proj(
            x, mod3, norm_g[l], w_in, rope_x, dft_x, tm=IN_TM, mod_row=None, layer=l)
        att, cv = _local_attention(sink, q, (kt, v), kvc, gatt, uc, conv_args)
        if not last:
            h_ctx = _out_proj(h_ctx, att_c, cv_c, gcv_c, ab_c, gf_c, pos_ctx, *four_args,
                              tm=n_ctx, mod_row=ctx_row, final=False, layer=l)
        x = _out_proj(x, att, cv, gcv, ab, gf, pos_x, *four_args, tm=OUT_TM, mod_row=None, final=last, layer=l)
    return x
```

```python
import functools

import numpy as np
import jax
import jax.numpy as jnp
from jax import lax
from jax.experimental import pallas as pl
from jax.experimental.pallas import tpu as pltpu

F32 = jnp.float32
BF16 = jnp.bfloat16

D_MODEL = 1024
DEPTH = 2
GRID_W = 64
HEAD_DIM = 64
ATT_W = 512
N_HEADS = 8
N_KV_HEADS = 2
GQA_GROUP = N_HEADS // N_KV_HEADS
KV_W = N_KV_HEADS * HEAD_DIM
CONV_W = 256
FOUR_W = 256
FOUR_HEADS = 4
FOUR_HEAD_DIM = FOUR_W // FOUR_HEADS
CONV_K = 31
SUBLANES = 8
CONV_HALO = 16
WINDOW = 128
BLOCK = 128
ROPE_BASE = 10000.0
EPS = 1e-6
NEG_INF = -1e30
LOG2E = 1.4426950408889634
Q_SCALE = HEAD_DIM ** -0.5 * LOG2E
IN_W = 2 * ATT_W + 2 * KV_W + 3 * CONV_W + 2 * FOUR_W
OFF_Q = 0
OFF_KV = ATT_W
OFF_GATT = OFF_KV + 2 * KV_W
OFF_CONV = OFF_GATT + ATT_W
OFF_GCONV = OFF_CONV + 2 * CONV_W
OFF_FOUR = OFF_GCONV + CONV_W
OFF_GFOUR = OFF_FOUR + FOUR_W
LANES = 128
KVX_W = 4 * LANES
MOD_ROWS = 16
SEQUENTIAL = pltpu.CompilerParams(dimension_semantics=("arbitrary", "arbitrary"))
ADA_TN = 1536
IN_TM = 1024
OUT_TM = 1024
ATT_BLOCKS = 8
ATT_AHEAD = 8
CTX_AHEAD = 2


def _dot(a, b):
    return jnp.dot(a, b, preferred_element_type=F32)


def _silu(x):
    return x * jax.nn.sigmoid(x)


def _ada_kernel(cc_ref, w_ref, b_ref, o_ref):
    a = _silu(cc_ref[...])
    w = w_ref[0]
    a_hi = a.astype(BF16)
    a_lo = (a - a_hi.astype(F32)).astype(BF16)
    w_hi = w.astype(BF16)
    o_ref[0] = _dot(a_hi, w_hi) + _dot(a_lo, w_hi) + b_ref[0]


def _modulation(cc, w_ada, b_ada):
    tn = ADA_TN
    return pl.pallas_call(
        _ada_kernel,
        grid=(DEPTH, 3 * D_MODEL // tn),
        in_specs=[
            pl.BlockSpec((MOD_ROWS, D_MODEL), lambda l, j: (0, 0)),
            pl.BlockSpec((1, D_MODEL, tn), lambda l, j: (l, 0, j)),
            pl.BlockSpec((1, 1, tn), lambda l, j: (l, 0, j)),
        ],
        out_specs=pl.BlockSpec((1, MOD_ROWS, tn), lambda l, j: (l, 0, j)),
        out_shape=jax.ShapeDtypeStruct((DEPTH, MOD_ROWS, 3 * D_MODEL), F32),
        name="ada_modulation",
    )(cc, w_ada, b_ada.reshape(DEPTH, 1, 3 * D_MODEL))


def _zero_after(token):
    bits = pltpu.bitcast(jnp.broadcast_to(token[:SUBLANES], (SUBLANES, LANES)), jnp.uint32)
    zero = lax.shift_right_logical(lax.shift_right_logical(bits, jnp.uint32(16)), jnp.uint32(16))
    return pltpu.bitcast(zero, F32)[:1]


def _conv_tile(pad_ref, w_ref, t0, lt, zero=None):
    lead = CONV_HALO - CONV_K // 2
    lanes = slice(lt * LANES, (lt + 1) * LANES)
    acc = None
    for r in range(SUBLANES):
        z = None
        for a in range((CONV_K - r + SUBLANES - 1) // SUBLANES):
            j = SUBLANES * a + r
            w = w_ref[j:j + 1, lanes]
            term = pad_ref[pl.ds(t0 + SUBLANES * a, BLOCK + SUBLANES), lanes] * (w if zero is None else w + zero)
            z = term if z is None else z + term
        shift = r + lead
        zs = z[shift:shift + BLOCK] if shift % SUBLANES == 0 else pltpu.roll(z, BLOCK + SUBLANES - shift, 0)[:BLOCK]
        acc = zs if acc is None else acc + zs
    return acc


def _conv_finish(tiles, b_ref, lg_ref, lb_ref):
    y = jnp.concatenate(tiles, axis=1) + b_ref[...]
    mu = jnp.mean(y, axis=-1, keepdims=True)
    yc = y - mu
    var = jnp.mean(yc * yc, axis=-1, keepdims=True)
    return _silu(yc * lax.rsqrt(var + EPS) * lg_ref[...] + lb_ref[...]).astype(BF16)


def _fill_conv_pad(pad_ref, u_ref, seq):
    zeros = jnp.zeros((CONV_HALO, CONV_W), F32)
    pad_ref[0:CONV_HALO, :] = zeros
    pad_ref[CONV_HALO + seq:2 * CONV_HALO + seq, :] = zeros
    pad_ref[CONV_HALO:CONV_HALO + seq, :] = u_ref[0].astype(F32)


def _conv_fillers(pad_ref, conv_refs, cv_ref, t0, n_rows):
    w_ref, b_ref, lg_ref, lb_ref = conv_refs
    pieces = []
    for j in range(n_rows // BLOCK):
        tiles = []
        for lt in range(CONV_W // LANES):
            pieces.append(lambda token, j=j, lt=lt, tiles=tiles: tiles.append(
                _conv_tile(pad_ref, w_ref, t0 + j * BLOCK, lt, _zero_after(token))))

        def finish(token, j=j, tiles=tiles):
            cv_ref[0, j * BLOCK:(j + 1) * BLOCK, :] = _conv_finish(tiles, b_ref, lg_ref, lb_ref)
        pieces.append(finish)
    return pieces


def _conv_specs(index):
    vec = pl.BlockSpec((1, CONV_W), index)
    return [pl.BlockSpec((CONV_K, CONV_W), index), vec, vec, vec]


def _conv_args(conv_w, conv_b, ln_g, ln_b):
    row = lambda a: a.reshape(1, CONV_W)
    return conv_w, row(conv_b), row(ln_g), row(ln_b)


def _cast_weight_once(w_ref, wbf_ref):
    @pl.when((pl.program_id(0) == 0) & (pl.program_id(1) == 0))
    def _():
        wbf_ref[...] = w_ref[0].astype(BF16)


def _project(x_ref, sh_ref, sc_ref, g_ref, w_ref, wbf_ref):
    _cast_weight_once(w_ref, wbf_ref)
    x = x_ref[0]
    ms = jnp.mean(x * x, axis=-1, keepdims=True)
    gain = g_ref[...] * (1.0 + sc_ref[0])
    h = x * lax.rsqrt(ms + EPS) * gain + sh_ref[0]
    return _dot(h.astype(BF16), wbf_ref[...])


def _rope_block(blk, rope_ref, base):
    cos = rope_ref[:, base:base + LANES]
    sin_lo = rope_ref[:, base + LANES:base + 2 * LANES]
    sin_hi = rope_ref[:, base + 2 * LANES:base + 3 * LANES]
    return (blk * cos + pltpu.roll(blk, LANES - 16, 1) * sin_lo + pltpu.roll(blk, 16, 1) * sin_hi)


def _store_padded_kv(kt_ref, v_ref, kblk, vblk):
    tm = kblk.shape[0]
    kt = kblk.T.astype(BF16)
    zeros = jnp.zeros((HEAD_DIM, tm), BF16)
    k_blocks = ((kt[:HEAD_DIM], zeros), (zeros, kt[:HEAD_DIM]), (kt[HEAD_DIM:], zeros), (zeros, kt[HEAD_DIM:]))
    for i, halves in enumerate(k_blocks):
        kt_ref[0, i * LANES:(i + 1) * LANES, :] = jnp.concatenate(halves, axis=0)
    lo = lax.broadcasted_iota(jnp.int32, vblk.shape, 1) < HEAD_DIM
    swapped = pltpu.roll(vblk, HEAD_DIM, 1)
    v_blocks = (jnp.where(lo, vblk, 0.0), jnp.where(lo, 0.0, swapped),
                jnp.where(lo, swapped, 0.0), jnp.where(lo, 0.0, vblk))
    for i, b in enumerate(v_blocks):
        v_ref[0, :, i * LANES:(i + 1) * LANES] = b.astype(BF16)


def _project_epilogue(p, rope_ref, dft_ref, q_ref, kt_ref, v_ref, gatt_ref, gcv_ref, ab_ref, gf_ref):
    for c in range(ATT_W // LANES):
        blk = _rope_block(p[:, OFF_Q + c * LANES:OFF_Q + (c + 1) * LANES], rope_ref, 0)
        q_ref[0, :, c * LANES:(c + 1) * LANES] = blk.astype(BF16)
    _store_padded_kv(kt_ref, v_ref, _rope_block(p[:, OFF_KV:OFF_KV + KV_W], rope_ref, 3 * LANES),
                     p[:, OFF_KV + KV_W:OFF_KV + 2 * KV_W])
    gatt_ref[0] = _silu(p[:, OFF_GATT:OFF_GATT + ATT_W]).astype(BF16)
    gcv_ref[0] = _silu(p[:, OFF_GCONV:OFF_GCONV + CONV_W]).astype(BF16)
    fu = p[:, OFF_FOUR:OFF_FOUR + FOUR_W].astype(BF16)
    ab_ref[0] = _dot(fu, dft_ref[...]).astype(BF16)
    gf_ref[0] = _silu(p[:, OFF_GFOUR:OFF_GFOUR + FOUR_W]).astype(BF16)
    return (p[:, OFF_CONV:OFF_CONV + CONV_W] * jax.nn.sigmoid(p[:, OFF_CONV + CONV_W:OFF_CONV + 2 * CONV_W])
            ).astype(BF16)


def _in_kernel(x_ref, sh_ref, sc_ref, g_ref, w_ref, rope_ref, dft_ref,
               q_ref, kt_ref, v_ref, gatt_ref, gcv_ref, ab_ref, gf_ref, uc_ref, wbf_ref):
    p = _project(x_ref, sh_ref, sc_ref, g_ref, w_ref, wbf_ref)
    uc_ref[0] = _project_epilogue(p, rope_ref, dft_ref, q_ref, kt_ref, v_ref, gatt_ref, gcv_ref, ab_ref, gf_ref)


def _mod_spec(mod_row, j):
    row = (lambda b: mod_row) if mod_row is not None else (lambda b: b)
    return pl.BlockSpec((1, 1, D_MODEL), lambda b, i: (row(b), 0, j))


def _in_specs(tm, mod_row, layer):
    return [
        pl.BlockSpec((1, tm, D_MODEL), lambda b, i: (b, i, 0)),
        _mod_spec(mod_row, 0), _mod_spec(mod_row, 1),
        pl.BlockSpec((1, D_MODEL), lambda b, i: (0, 0)),
        pl.BlockSpec((1, D_MODEL, IN_W), lambda b, i: (layer, 0, 0)),
        pl.BlockSpec((tm, 6 * LANES), lambda b, i: (i, 0)),
        pl.BlockSpec((FOUR_W, 2 * FOUR_W), lambda b, i: (0, 0)),
    ]


def _in_outs(bn, s, tm):
    widths = (ATT_W, None, KVX_W, ATT_W, CONV_W, 2 * FOUR_W, FOUR_W, CONV_W)
    specs = [pl.BlockSpec((1, tm, w), lambda b, i: (b, i, 0)) if w else
             pl.BlockSpec((1, KVX_W, tm), lambda b, i: (b, 0, i)) for w in widths]
    shapes = [jax.ShapeDtypeStruct((bn, s, w) if w else (bn, KVX_W, s), BF16) for w in widths]
    return specs, shapes


def _in_proj(x, mod3, norm_g, w, rope_tab, dft_c, *, tm, mod_row, layer):
    bn, s, _ = x.shape
    specs, shapes = _in_outs(bn, s, tm)
    return pl.pallas_call(
        _in_kernel,
        grid=(bn, s // tm),
        in_specs=_in_specs(tm, mod_row, layer),
        out_specs=specs,
        out_shape=shapes,
        scratch_shapes=[pltpu.VMEM((D_MODEL, IN_W), BF16)],
        compiler_params=SEQUENTIAL,
        name="in_proj",
    )(x, mod3, mod3, norm_g.reshape(1, D_MODEL), w, rope_tab, dft_c)


def _kv_kernel(x_ref, sh_ref, sc_ref, g_ref, w_ref, kt_ref, v_ref, wbf_ref):
    kvp = _project(x_ref, sh_ref, sc_ref, g_ref, w_ref, wbf_ref)
    _store_padded_kv(kt_ref, v_ref, kvp[:, 0:KV_W], kvp[:, KV_W:2 * KV_W])


def _kv_proj(x, mod3, norm_g, w, *, tm, mod_row, layer):
    bn, s, _ = x.shape
    assert OFF_KV % (2 * KV_W) == 0
    return pl.pallas_call(
        _kv_kernel,
        grid=(bn, s // tm),
        in_specs=[
            pl.BlockSpec((1, tm, D_MODEL), lambda b, i: (b, i, 0)),
            _mod_spec(mod_row, 0), _mod_spec(mod_row, 1),
            pl.BlockSpec((1, D_MODEL), lambda b, i: (0, 0)),
            pl.BlockSpec((1, D_MODEL, 2 * KV_W), lambda b, i: (layer, 0, OFF_KV // (2 * KV_W))),
        ],
        out_specs=[pl.BlockSpec((1, KVX_W, tm), lambda b, i: (b, 0, i)),
                   pl.BlockSpec((1, tm, KVX_W), lambda b, i: (b, i, 0))],
        out_shape=[jax.ShapeDtypeStruct((bn, KVX_W, s), BF16), jax.ShapeDtypeStruct((bn, s, KVX_W), BF16)],
        scratch_shapes=[pltpu.VMEM((D_MODEL, 2 * KV_W), BF16)],
        compiler_params=SEQUENTIAL,
        name="ctx_kv_proj",
    )(x, mod3, mod3, norm_g.reshape(1, D_MODEL), w)


class _Chain:
    def __init__(self, q2, sink_col, kset):
        self.q2, self.sink_col, self.kset = q2, sink_col, kset

    def scores(self):
        self.s = []
        for kt, _, bias in self.kset:
            sc = _dot(self.q2, kt)
            self.s.append(sc if bias is None else sc + bias)

    def softmax(self):
        m = self.sink_col
        for sc in self.s:
            m = jnp.maximum(m, sc.max(axis=-1, keepdims=True))
        self.p = [jnp.exp2(sc - m).astype(BF16) for sc in self.s]
        self.extra = jnp.exp2(self.sink_col - m)
        self.row_max = m
        self.s = None

    def values(self):
        r = None
        for p, (_, v, _) in zip(self.p, self.kset):
            pv = _dot(p, jnp.concatenate([v, jnp.ones_like(v)], axis=1))
            r = pv if r is None else r + pv
        self.p = None
        return r[:, :LANES] / (r[:, LANES:] + self.extra)


def _run_chains(chains, finish, fillers, ahead):
    n = len(chains)
    fillers = list(fillers)
    for i in range(-ahead, n):
        if 0 <= i < n:
            finish(i, chains[i].values())
        if 0 <= i + ahead < n:
            chains[i + ahead].scores()
        if 0 <= i + 1 < n:
            chains[i + 1].softmax()
            if fillers:
                fillers.pop(0)(chains[i + 1].row_max)
    for f in fillers:
        f(chains[-1].row_max)


def _kv_block(kvh, e):
    k0 = (2 * kvh + e) * LANES
    return slice(k0, k0 + LANES)


def _group_chains(sink_ref, q_ref, rows, tq, kvh, ksets):
    c0 = 2 * kvh * LANES
    q2 = jnp.concatenate([q_ref[0, rows, c0:c0 + LANES], q_ref[0, rows, c0 + LANES:c0 + 2 * LANES]], axis=0)
    chains = []
    for e in range(2):
        sink_col = jnp.concatenate([jnp.full((tq, 1), sink_ref[GQA_GROUP * kvh + e] * LOG2E, F32),
                                    jnp.full((tq, 1), sink_ref[GQA_GROUP * kvh + 2 + e] * LOG2E, F32)], axis=0)
        chains.append(_Chain(q2, sink_col, ksets[e]))
    return chains


def _make_finish(groups, g_ref, o_ref):
    partial = {}

    def finish(i, out):
        if i % 2 == 0:
            partial[i // 2] = out
            return
        rows, tq, kvh = groups[i // 2]
        o = partial.pop(i // 2) + out
        c0 = 2 * kvh * LANES
        for hf in range(2):
            cols = slice(c0 + hf * LANES, c0 + (hf + 1) * LANES)
            g = g_ref[0, rows, cols].astype(F32)
            o_ref[0, rows, cols] = (o[hf * tq:(hf + 1) * tq] * g).astype(BF16)
    return finish


def _local_attn_kernel(sink_ref, q_ref, kt_ref, v_ref, ktc_ref, vc_ref, bias_ref, g_ref, uc_ref,
                       cw_ref, cb_ref, lg_ref, lb_ref, o_ref, cv_ref, pad_ref, *, seq, sub, ahead):
    @pl.when(pl.program_id(1) == 0)
    def _():
        _fill_conv_pad(pad_ref, uc_ref, seq)

    nblk = seq // BLOCK
    chains, groups = [], []
    for sb in range(sub):
        n = pl.program_id(1) * sub + sb
        start = pl.multiple_of(jnp.clip((n - 1) * BLOCK, 0, seq - 3 * BLOCK), BLOCK)
        win = pl.ds(start, 3 * BLOCK)
        bias = bias_ref[jnp.where(n == 0, 0, jnp.where(n == nblk - 1, 2, 1))]
        bias2 = jnp.concatenate([bias, bias], axis=0)
        rows = slice(sb * BLOCK, (sb + 1) * BLOCK)
        for kvh in range(N_KV_HEADS):
            ksets = []
            for e in range(2):
                blk = _kv_block(kvh, e)
                ksets.append([(kt_ref[0, blk, win], v_ref[0, win, blk], bias2),
                              (ktc_ref[0, blk, :], vc_ref[0, :, blk], None)])
            chains += _group_chains(sink_ref, q_ref, rows, BLOCK, kvh, ksets)
            groups.append((rows, BLOCK, kvh))
    t0 = pl.multiple_of(pl.program_id(1) * (sub * BLOCK), sub * BLOCK)
    fillers = _conv_fillers(pad_ref, (cw_ref, cb_ref, lg_ref, lb_ref), cv_ref, t0, sub * BLOCK)
    _run_chains(chains, _make_finish(groups, g_ref, o_ref), fillers, ahead)


def _band_bias():
    i = np.arange(BLOCK)[:, None]
    k = np.arange(3 * BLOCK)[None, :]
    tabs = [np.where(np.abs(BLOCK * qb + i - k) <= WINDOW, 0.0, NEG_INF) for qb in range(3)]
    return jnp.asarray(np.stack(tabs).astype(np.float32))


def _local_attention(sink, q, kv, kvc, gatt, uc, conv_args):
    bn, s, _ = q.shape
    n_ctx = kvc[1].shape[1]
    sub = ATT_BLOCKS
    ahead = ATT_AHEAD
    tq = sub * BLOCK
    assert s % tq == 0 and s >= 3 * BLOCK
    tile = lambda w: pl.BlockSpec((1, tq, w), lambda b, n: (b, n, 0))
    whole = lambda rows, w: pl.BlockSpec((1, rows, w), lambda b, n: (b, 0, 0))
    return pl.pallas_call(
        functools.partial(_local_attn_kernel, seq=s, sub=sub, ahead=ahead),
        grid=(bn, s // tq),
        in_specs=[
            pl.BlockSpec(memory_space=pltpu.SMEM),
            tile(ATT_W), whole(KVX_W, s), whole(s, KVX_W),
            pl.BlockSpec((1, KVX_W, n_ctx), lambda b, n: (0, 0, b)), whole(n_ctx, KVX_W),
            pl.BlockSpec((3, BLOCK, 3 * BLOCK), lambda b, n: (0, 0, 0)),
            tile(ATT_W), whole(s, CONV_W),
        ] + _conv_specs(lambda b, n: (0, 0)),
        out_specs=[tile(ATT_W), tile(CONV_W)],
        out_shape=[jax.ShapeDtypeStruct((bn, s, ATT_W), BF16), jax.ShapeDtypeStruct((bn, s, CONV_W), BF16)],
        scratch_shapes=[pltpu.VMEM((s + 2 * CONV_HALO, CONV_W), F32)],
        compiler_params=SEQUENTIAL,
        name="local_attention",
    )(sink, q, *kv, *kvc, _band_bias(), gatt, uc, *conv_args)


def _ctx_attn_kernel(sink_ref, q_ref, ktc_ref, vc_ref, g_ref, uc_ref, cw_ref, cb_ref, lg_ref, lb_ref,
                     o_ref, cv_ref, pad_ref, *, n_ctx, ahead):
    _fill_conv_pad(pad_ref, uc_ref, n_ctx)
    rows = slice(0, n_ctx)
    chains, groups = [], []
    for kvh in range(N_KV_HEADS):
        ksets = []
        for e in range(2):
            blk = _kv_block(kvh, e)
            ksets.append([(ktc_ref[0, blk, :], vc_ref[0, :, blk], None)])
        chains += _group_chains(sink_ref, q_ref, rows, n_ctx, kvh, ksets)
        groups.append((rows, n_ctx, kvh))
    fillers = _conv_fillers(pad_ref, (cw_ref, cb_ref, lg_ref, lb_ref), cv_ref, 0, n_ctx)
    _run_chains(chains, _make_finish(groups, g_ref, o_ref), fillers, ahead)


def _ctx_attention(sink, q, kvc, gatt, uc, conv_args):
    bn, n_ctx, _ = q.shape
    ahead = CTX_AHEAD
    whole = lambda w: pl.BlockSpec((1, n_ctx, w), lambda b: (b, 0, 0))
    return pl.pallas_call(
        functools.partial(_ctx_attn_kernel, n_ctx=n_ctx, ahead=ahead),
        grid=(bn,),
        in_specs=[pl.BlockSpec(memory_space=pltpu.SMEM), whole(ATT_W),
                  pl.BlockSpec((1, KVX_W, n_ctx), lambda b: (0, 0, b)), whole(KVX_W), whole(ATT_W),
                  whole(CONV_W)] + _conv_specs(lambda b: (0, 0)),
        out_specs=[whole(ATT_W), whole(CONV_W)],
        out_shape=[jax.ShapeDtypeStruct((bn, n_ctx, ATT_W), BF16),
                   jax.ShapeDtypeStruct((bn, n_ctx, CONV_W), BF16)],
        scratch_shapes=[pltpu.VMEM((n_ctx + 2 * CONV_HALO, CONV_W), F32)],
        name="ctx_attention",
    )(sink, q, *kvc, gatt, uc, *conv_args)


def _fourier_prepare(ab_ref, cw_ref, sw_ref, r_ref, half):
    rc_ref, rs_ref = r_ref.at[0:half], r_ref.at[half:2 * half]
    top = ab_ref[0, 0:half, :].astype(F32)
    bot = ab_ref[0, half:2 * half, :].astype(F32)
    plus = top + bot
    minus = top - bot
    a_m, b_m = minus[:, :FOUR_W], minus[:, FOUR_W:]
    cw, sw = cw_ref[...], sw_ref[...]
    rc_ref[:, 0:FOUR_W] = plus[:, :FOUR_W].astype(BF16)
    rc_ref[:, FOUR_W:] = (cw * a_m - sw * b_m).astype(BF16)
    rs_ref[:, 0:FOUR_W] = plus[:, FOUR_W:].astype(BF16)
    rs_ref[:, FOUR_W:] = (sw * a_m + cw * b_m).astype(BF16)


def _fourier_rows(cs_ref, r_ref, wf_ref, bf_ref, il_ref, r0, chunk):
    eb = _dot(cs_ref[pl.ds(r0, chunk), :], r_ref[...]).astype(BF16)
    for p in range(2):
        f = _dot(eb[:, p * FOUR_W:(p + 1) * FOUR_W], wf_ref[...])
        for lt in range(FOUR_W // LANES):
            il_ref[lt, pl.ds(p, chunk, stride=2), :] = f[:, lt * LANES:(lt + 1) * LANES]
    return jnp.concatenate([il_ref[lt] for lt in range(FOUR_W // LANES)], axis=1) + bf_ref[...]


def _out_kernel(x_ref, att_ref, cv_ref, gcv_ref, ab_ref, gf_ref, cs_ref, cw_ref, sw_ref, wf_ref, bf_ref,
                w_ref, gate_ref, fg_ref, o_ref, r_ref, il_ref, wbf_ref, *, tm, half, final):
    _cast_weight_once(w_ref, wbf_ref)

    @pl.when(pl.program_id(1) == 0)
    def _():
        _fourier_prepare(ab_ref, cw_ref, sw_ref, r_ref, half)

    r0 = pl.multiple_of(pl.program_id(1) * (tm // 2), tm // 2)
    four = _fourier_rows(cs_ref, r_ref, wf_ref, bf_ref, il_ref, r0, tm // 2)
    fo = (four * gf_ref[0].astype(F32)).astype(BF16)
    cv = cv_ref[0] * gcv_ref[0]
    y = _dot(jnp.concatenate([att_ref[0], cv, fo], axis=1), wbf_ref[...])
    xn = x_ref[0] + gate_ref[0] * y
    if final:
        ms = jnp.mean(xn * xn, axis=-1, keepdims=True)
        xn = xn * lax.rsqrt(ms + EPS) * fg_ref[...]
    o_ref[0] = xn


def _out_proj(x, att, cv, gcv, ab, gf, pos_tabs, w_four_bf, b_four, w_out, mod3, final_g,
              *, tm, mod_row, final, layer):
    bn, s, _ = x.shape
    half = s // 2
    cs_half, cw, sw = pos_tabs
    tile = lambda w: pl.BlockSpec((1, tm, w), lambda b, i: (b, i, 0))
    const = lambda shape: pl.BlockSpec(shape, lambda b, i: (0,) * len(shape))
    return pl.pallas_call(
        functools.partial(_out_kernel, tm=tm, half=half, final=final),
        grid=(bn, s // tm),
        in_specs=[
            tile(D_MODEL), tile(ATT_W), tile(CONV_W), tile(CONV_W),
            pl.BlockSpec((1, s, 2 * FOUR_W), lambda b, i: (b, 0, 0)),
            tile(FOUR_W),
            const((half, 2 * half)), const((half, FOUR_W)), const((half, FOUR_W)),
            const((FOUR_W, FOUR_W)), const((1, FOUR_W)),
            pl.BlockSpec((1, D_MODEL, D_MODEL), lambda b, i: (layer, 0, 0)),
            _mod_spec(mod_row, 2),
            const((1, D_MODEL)),
        ],
        out_specs=tile(D_MODEL),
        out_shape=jax.ShapeDtypeStruct((bn, s, D_MODEL), F32),
        scratch_shapes=[pltpu.VMEM((2 * half, 2 * FOUR_W), BF16),
                        pltpu.VMEM((FOUR_W // LANES, tm, LANES), F32),
                        pltpu.VMEM((D_MODEL, D_MODEL), BF16)],
        compiler_params=SEQUENTIAL,
        name="out_proj",
    )(x, att, cv, gcv, ab, gf, cs_half, cw, sw, w_four_bf, b_four.reshape(1, FOUR_W),
      w_out, mod3, final_g.reshape(1, D_MODEL))


def _rope_tables(s):
    t = np.arange(s)
    half = HEAD_DIM // 4
    freqs = ROPE_BASE ** (-np.arange(half) / half)
    zero = np.zeros((s, half))

    def per_head(fn_lo, fn_hi):
        parts = []
        for pos in (t // GRID_W, t % GRID_W):
            ang = pos[:, None] * freqs[None, :]
            parts += [fn_lo(ang), fn_hi(ang)]
        return np.tile(np.concatenate(parts, axis=-1), (1, LANES // HEAD_DIM))

    cos = per_head(np.cos, np.cos)
    sin_lo = per_head(lambda a: -np.sin(a), lambda a: zero)
    sin_hi = per_head(lambda a: zero, np.sin)
    tab = np.concatenate([cos * Q_SCALE, sin_lo * Q_SCALE, sin_hi * Q_SCALE, cos, sin_lo, sin_hi], axis=-1)
    return jnp.asarray(tab.astype(np.float32))


def _identity_rope_tables(s):
    one = np.ones((s, LANES))
    zero = np.zeros((s, LANES))
    return jnp.asarray(np.concatenate([one * Q_SCALE, zero, zero, one, zero, zero], axis=-1).astype(np.float32))


def _channel_dft(n_pos):
    c = np.arange(FOUR_HEAD_DIM)
    ang = 2.0 * np.pi * ((c[:, None] * c[None, :]) % FOUR_HEAD_DIM) / FOUR_HEAD_DIM
    eye = np.eye(FOUR_HEADS)
    scale = 1.0 / np.sqrt(float(n_pos * FOUR_HEAD_DIM))
    tab = np.concatenate([np.kron(eye, np.cos(ang)), np.kron(eye, np.sin(ang))], axis=1) * scale
    return jnp.asarray(tab.astype(np.float32))


def _position_dft(n_pos):
    half = n_pos // 2
    k = np.arange(half)
    ang = 2.0 * np.pi * ((k[:, None] * k[None, :]) % half) / half
    beta = np.pi * k / half
    ones = np.ones((1, FOUR_W))
    f32 = lambda a: jnp.asarray(a.astype(np.float32))
    return (f32(np.concatenate([np.cos(ang), -np.sin(ang)], axis=1)).astype(BF16),
            f32(np.cos(beta)[:, None] * ones), f32(np.sin(beta)[:, None] * ones))


def kernel(x, c, ctx, c_ctx, w_ada, b_ada, norm_g, w_in, attn_sink, conv_w, conv_b,
           conv_ln_g, conv_ln_b, w_four, b_four, w_out, final_g):
    bn, s, _ = x.shape
    n_ctx = ctx.shape[1]
    assert bn < MOD_ROWS
    ctx_row = bn

    cc = jnp.zeros((MOD_ROWS, D_MODEL), F32).at[:bn].set(c).at[ctx_row].set(c_ctx)
    mod = _modulation(cc, w_ada, b_ada)

    w_four_bf = w_four.astype(BF16)
    rope_x = _rope_tables(s)
    rope_c = _identity_rope_tables(bn * n_ctx)
    dft_x = _channel_dft(s).astype(BF16)
    dft_ctx = _channel_dft(n_ctx).astype(BF16)
    pos_x = _position_dft(s)
    pos_ctx = _position_dft(n_ctx)

    h_ctx = ctx
    for l in range(DEPTH):
        mod3 = mod[l].reshape(MOD_ROWS, 1, 3 * D_MODEL)
        conv_args = _conv_args(conv_w[l], conv_b[l], conv_ln_g[l], conv_ln_b[l])
        four_args = (w_four_bf[l], b_four[l], w_out, mod3, final_g)
        sink = attn_sink[l]
        last = l == DEPTH - 1
        ctx_flat = h_ctx.reshape(1, bn * n_ctx, D_MODEL)
        per_batch = lambda a: a.reshape(bn, n_ctx, a.shape[-1])
        if not last:
            qc, ktc, *rest = _in_proj(ctx_flat, mod3, norm_g[l], w_in, rope_c, dft_ctx,
                                      tm=IN_TM, mod_row=ctx_row, layer=l)
            vc, gatt_c, gcv_c, ab_c, gf_c, uc_c = map(per_batch, rest)
            kvc = (ktc, vc)
            att_c, cv_c = _ctx_attention(sink, per_batch(qc), kvc, gatt_c, uc_c, conv_args)
        else:
            ktc, vc = _kv_proj(ctx_flat, mod3, norm_g[l], w_in, tm=IN_TM, mod_row=ctx_row, layer=l)
            kvc = (ktc, per_batch(vc))
        q, kt, v, gatt, gcv, ab, gf, uc = _in_proj(
            x, mod3, norm_g[l], w_in, rope_x, dft_x, tm=IN_TM, mod_row=None, layer=l)
        att, cv = _local_attention(sink, q, (kt, v), kvc, gatt, uc, conv_args)
        if not last:
            h_ctx = _out_proj(h_ctx, att_c, cv_c, gcv_c, ab_c, gf_c, pos_ctx, *four_args,
                              tm=n_ctx, mod_row=ctx_row, final=False, layer=l)
        x = _out_proj(x, att, cv, gcv, ab, gf, pos_x, *four_args, tm=OUT_TM, mod_row=None, final=last, layer=l)
    return x
```

```python
import functools

import numpy as np
import jax
import jax.numpy as jnp
from jax import lax
from jax.experimental import pallas as pl
from jax.experimental.pallas import tpu as pltpu

F32 = jnp.float32
BF16 = jnp.bfloat16

D_MODEL = 1024
DEPTH = 2
GRID_W = 64
HEAD_DIM = 64
ATT_W = 512
N_HEADS = 8
N_KV_HEADS = 2
GQA_GROUP = N_HEADS // N_KV_HEADS
KV_W = N_KV_HEADS * HEAD_DIM
CONV_W = 256
FOUR_W = 256
FOUR_HEADS = 4
FOUR_HEAD_DIM = FOUR_W // FOUR_HEADS
CONV_K = 31
SUBLANES = 8
CONV_HALO = 16
WINDOW = 128
BLOCK = 128
ROPE_BASE = 10000.0
EPS = 1e-6
NEG_INF = -1e30
LOG2E = 1.4426950408889634
Q_SCALE = HEAD_DIM ** -0.5 * LOG2E
IN_W = 2 * ATT_W + 2 * KV_W + 3 * CONV_W + 2 * FOUR_W
OFF_Q = 0
OFF_KV = ATT_W
OFF_GATT = OFF_KV + 2 * KV_W
OFF_CONV = OFF_GATT + ATT_W
OFF_GCONV = OFF_CONV + 2 * CONV_W
OFF_FOUR = OFF_GCONV + CONV_W
OFF_GFOUR = OFF_FOUR + FOUR_W
LANES = 128
KVX_W = 4 * LANES
MOD_ROWS = 16
SEQUENTIAL = pltpu.CompilerParams(dimension_semantics=("arbitrary", "arbitrary"))
ADA_TN = 1536
IN_TM = 1024
OUT_TM = 1024
ATT_BLOCKS = 8
ATT_AHEAD = 8
CTX_AHEAD = 2
WEIGHT_CHUNKS = 4


def _dot(a, b):
    return jnp.dot(a, b, preferred_element_type=F32)


def _silu(x):
    return x * jax.nn.sigmoid(x)


def _ada_kernel(cc_ref, w_ref, b_ref, o_ref):
    a = _silu(cc_ref[...])
    w = w_ref[0]
    a_hi = a.astype(BF16)
    a_lo = (a - a_hi.astype(F32)).astype(BF16)
    w_hi = w.astype(BF16)
    o_ref[0] = _dot(a_hi, w_hi) + _dot(a_lo, w_hi) + b_ref[0]


def _modulation(cc, w_ada, b_ada):
    tn = ADA_TN
    return pl.pallas_call(
        _ada_kernel,
        grid=(DEPTH, 3 * D_MODEL // tn),
        in_specs=[
            pl.BlockSpec((MOD_ROWS, D_MODEL), lambda l, j: (0, 0)),
            pl.BlockSpec((1, D_MODEL, tn), lambda l, j: (l, 0, j)),
            pl.BlockSpec((1, 1, tn), lambda l, j: (l, 0, j)),
        ],
        out_specs=pl.BlockSpec((1, MOD_ROWS, tn), lambda l, j: (l, 0, j)),
        out_shape=jax.ShapeDtypeStruct((DEPTH, MOD_ROWS, 3 * D_MODEL), F32),
        name="ada_modulation",
    )(cc, w_ada, b_ada.reshape(DEPTH, 1, 3 * D_MODEL))


def _zero_after(token):
    bits = pltpu.bitcast(jnp.broadcast_to(token[:SUBLANES], (SUBLANES, LANES)), jnp.uint32)
    zero = lax.shift_right_logical(lax.shift_right_logical(bits, jnp.uint32(16)), jnp.uint32(16))
    return pltpu.bitcast(zero, F32)[:1]


def _conv_tile(pad_ref, w_ref, t0, lt, zero=None):
    lead = CONV_HALO - CONV_K // 2
    lanes = slice(lt * LANES, (lt + 1) * LANES)
    acc = None
    for r in range(SUBLANES):
        z = None
        for a in range((CONV_K - r + SUBLANES - 1) // SUBLANES):
            j = SUBLANES * a + r
            w = w_ref[j:j + 1, lanes]
            term = pad_ref[pl.ds(t0 + SUBLANES * a, BLOCK + SUBLANES), lanes] * (w if zero is None else w + zero)
            z = term if z is None else z + term
        shift = r + lead
        zs = z[shift:shift + BLOCK] if shift % SUBLANES == 0 else pltpu.roll(z, BLOCK + SUBLANES - shift, 0)[:BLOCK]
        acc = zs if acc is None else acc + zs
    return acc


def _conv_finish(tiles, b_ref, lg_ref, lb_ref):
    y = jnp.concatenate(tiles, axis=1) + b_ref[...]
    mu = jnp.mean(y, axis=-1, keepdims=True)
    yc = y - mu
    var = jnp.mean(yc * yc, axis=-1, keepdims=True)
    return _silu(yc * lax.rsqrt(var + EPS) * lg_ref[...] + lb_ref[...]).astype(BF16)


def _fill_conv_pad(pad_ref, u_ref, seq):
    zeros = jnp.zeros((CONV_HALO, CONV_W), F32)
    pad_ref[0:CONV_HALO, :] = zeros
    pad_ref[CONV_HALO + seq:2 * CONV_HALO + seq, :] = zeros
    pad_ref[CONV_HALO:CONV_HALO + seq, :] = u_ref[0].astype(F32)


def _conv_fillers(pad_ref, conv_refs, cv_ref, t0, n_rows):
    w_ref, b_ref, lg_ref, lb_ref = conv_refs
    pieces = []
    for j in range(n_rows // BLOCK):
        tiles = []
        for lt in range(CONV_W // LANES):
            pieces.append(lambda token, j=j, lt=lt, tiles=tiles: tiles.append(
                _conv_tile(pad_ref, w_ref, t0 + j * BLOCK, lt, _zero_after(token))))

        def finish(token, j=j, tiles=tiles):
            cv_ref[0, j * BLOCK:(j + 1) * BLOCK, :] = _conv_finish(tiles, b_ref, lg_ref, lb_ref)
        pieces.append(finish)
    return pieces


def _conv_specs(index):
    vec = pl.BlockSpec((1, CONV_W), index)
    return [pl.BlockSpec((CONV_K, CONV_W), index), vec, vec, vec]


def _conv_args(conv_w, conv_b, ln_g, ln_b):
    row = lambda a: a.reshape(1, CONV_W)
    return conv_w, row(conv_b), row(ln_g), row(ln_b)


def _cast_weight_once(w_ref, wbf_ref):
    @pl.when((pl.program_id(0) == 0) & (pl.program_id(1) == 0))
    def _():
        wbf_ref[...] = w_ref[0].astype(BF16)


def _weight_chunk_copies(w_hbm_ref, layer, wf_ref, sem_ref):
    rows = wf_ref.shape[0] // WEIGHT_CHUNKS
    return [pltpu.make_async_copy(w_hbm_ref.at[layer, pl.ds(k * rows, rows)],
                                  wf_ref.at[pl.ds(k * rows, rows)], sem_ref.at[k])
            for k in range(WEIGHT_CHUNKS)]


def _project(x_ref, sh_ref, sc_ref, g_ref, w_ref, wbf_ref, staged=None):
    first = (pl.program_id(0) == 0) & (pl.program_id(1) == 0)
    if staged is None:
        _cast_weight_once(w_ref, wbf_ref)
    else:
        wf_ref, sem_ref, layer = staged
        copies = _weight_chunk_copies(w_ref, layer, wf_ref, sem_ref)

        @pl.when(first)
        def _():
            for cp in copies:
                cp.start()

    x = x_ref[0]
    ms = jnp.mean(x * x, axis=-1, keepdims=True)
    gain = g_ref[...] * (1.0 + sc_ref[0])
    h = (x * lax.rsqrt(ms + EPS) * gain + sh_ref[0]).astype(BF16)
    if staged is not None:
        rows = wf_ref.shape[0] // WEIGHT_CHUNKS

        @pl.when(first)
        def _():
            for k, cp in enumerate(copies):
                cp.wait()
                wbf_ref[k * rows:(k + 1) * rows, :] = wf_ref[k * rows:(k + 1) * rows, :].astype(BF16)

    return _dot(h, wbf_ref[...])


def _rope_block(blk, rope_ref, base):
    cos = rope_ref[:, base:base + LANES]
    sin_lo = rope_ref[:, base + LANES:base + 2 * LANES]
    sin_hi = rope_ref[:, base + 2 * LANES:base + 3 * LANES]
    return (blk * cos + pltpu.roll(blk, LANES - 16, 1) * sin_lo + pltpu.roll(blk, 16, 1) * sin_hi)


def _store_padded_kv(kt_ref, v_ref, kblk, vblk):
    tm = kblk.shape[0]
    kt = kblk.T.astype(BF16)
    zeros = jnp.zeros((HEAD_DIM, tm), BF16)
    k_blocks = ((kt[:HEAD_DIM], zeros), (zeros, kt[:HEAD_DIM]), (kt[HEAD_DIM:], zeros), (zeros, kt[HEAD_DIM:]))
    for i, halves in enumerate(k_blocks):
        kt_ref[0, i * LANES:(i + 1) * LANES, :] = jnp.concatenate(halves, axis=0)
    lo = lax.broadcasted_iota(jnp.int32, vblk.shape, 1) < HEAD_DIM
    swapped = pltpu.roll(vblk, HEAD_DIM, 1)
    v_blocks = (jnp.where(lo, vblk, 0.0), jnp.where(lo, 0.0, swapped),
                jnp.where(lo, swapped, 0.0), jnp.where(lo, 0.0, vblk))
    for i, b in enumerate(v_blocks):
        v_ref[0, :, i * LANES:(i + 1) * LANES] = b.astype(BF16)


def _project_epilogue(p, rope_ref, dft_ref, q_ref, kt_ref, v_ref, gatt_ref, gcv_ref, ab_ref, gf_ref):
    for c in range(ATT_W // LANES):
        blk = _rope_block(p[:, OFF_Q + c * LANES:OFF_Q + (c + 1) * LANES], rope_ref, 0)
        q_ref[0, :, c * LANES:(c + 1) * LANES] = blk.astype(BF16)
    _store_padded_kv(kt_ref, v_ref, _rope_block(p[:, OFF_KV:OFF_KV + KV_W], rope_ref, 3 * LANES),
                     p[:, OFF_KV + KV_W:OFF_KV + 2 * KV_W])
    gatt_ref[0] = _silu(p[:, OFF_GATT:OFF_GATT + ATT_W]).astype(BF16)
    gcv_ref[0] = _silu(p[:, OFF_GCONV:OFF_GCONV + CONV_W]).astype(BF16)
    fu = p[:, OFF_FOUR:OFF_FOUR + FOUR_W].astype(BF16)
    ab_ref[0] = _dot(fu, dft_ref[...]).astype(BF16)
    gf_ref[0] = _silu(p[:, OFF_GFOUR:OFF_GFOUR + FOUR_W]).astype(BF16)
    return (p[:, OFF_CONV:OFF_CONV + CONV_W] * jax.nn.sigmoid(p[:, OFF_CONV + CONV_W:OFF_CONV + 2 * CONV_W])
            ).astype(BF16)


def _in_kernel(x_ref, sh_ref, sc_ref, g_ref, w_ref, rope_ref, dft_ref,
               q_ref, kt_ref, v_ref, gatt_ref, gcv_ref, ab_ref, gf_ref, uc_ref, wbf_ref, wf_ref, sem_ref, *, layer):
    p = _project(x_ref, sh_ref, sc_ref, g_ref, w_ref, wbf_ref, staged=(wf_ref, sem_ref, layer))
    uc_ref[0] = _project_epilogue(p, rope_ref, dft_ref, q_ref, kt_ref, v_ref, gatt_ref, gcv_ref, ab_ref, gf_ref)


def _mod_spec(mod_row, j):
    row = (lambda b: mod_row) if mod_row is not None else (lambda b: b)
    return pl.BlockSpec((1, 1, D_MODEL), lambda b, i: (row(b), 0, j))


def _in_specs(tm, mod_row):
    return [
        pl.BlockSpec((1, tm, D_MODEL), lambda b, i: (b, i, 0)),
        _mod_spec(mod_row, 0), _mod_spec(mod_row, 1),
        pl.BlockSpec((1, D_MODEL), lambda b, i: (0, 0)),
        pl.BlockSpec(memory_space=pl.ANY),
        pl.BlockSpec((tm, 6 * LANES), lambda b, i: (i, 0)),
        pl.BlockSpec((FOUR_W, 2 * FOUR_W), lambda b, i: (0, 0)),
    ]


def _in_outs(bn, s, tm):
    widths = (ATT_W, None, KVX_W, ATT_W, CONV_W, 2 * FOUR_W, FOUR_W, CONV_W)
    specs = [pl.BlockSpec((1, tm, w), lambda b, i: (b, i, 0)) if w else
             pl.BlockSpec((1, KVX_W, tm), lambda b, i: (b, 0, i)) for w in widths]
    shapes = [jax.ShapeDtypeStruct((bn, s, w) if w else (bn, KVX_W, s), BF16) for w in widths]
    return specs, shapes


def _in_proj(x, mod3, norm_g, w, rope_tab, dft_c, *, tm, mod_row, layer):
    bn, s, _ = x.shape
    specs, shapes = _in_outs(bn, s, tm)
    return pl.pallas_call(
        functools.partial(_in_kernel, layer=layer),
        grid=(bn, s // tm),
        in_specs=_in_specs(tm, mod_row),
        out_specs=specs,
        out_shape=shapes,
        scratch_shapes=[pltpu.VMEM((D_MODEL, IN_W), BF16), pltpu.VMEM((D_MODEL, IN_W), F32),
                        pltpu.SemaphoreType.DMA((WEIGHT_CHUNKS,))],
        compiler_params=SEQUENTIAL,
        name="in_proj",
    )(x, mod3, mod3, norm_g.reshape(1, D_MODEL), w, rope_tab, dft_c)


def _kv_kernel(x_ref, sh_ref, sc_ref, g_ref, w_ref, kt_ref, v_ref, wbf_ref):
    kvp = _project(x_ref, sh_ref, sc_ref, g_ref, w_ref, wbf_ref)
    _store_padded_kv(kt_ref, v_ref, kvp[:, 0:KV_W], kvp[:, KV_W:2 * KV_W])


def _kv_proj(x, mod3, norm_g, w, *, tm, mod_row, layer):
    bn, s, _ = x.shape
    assert OFF_KV % (2 * KV_W) == 0
    return pl.pallas_call(
        _kv_kernel,
        grid=(bn, s // tm),
        in_specs=[
            pl.BlockSpec((1, tm, D_MODEL), lambda b, i: (b, i, 0)),
            _mod_spec(mod_row, 0), _mod_spec(mod_row, 1),
            pl.BlockSpec((1, D_MODEL), lambda b, i: (0, 0)),
            pl.BlockSpec((1, D_MODEL, 2 * KV_W), lambda b, i: (layer, 0, OFF_KV // (2 * KV_W))),
        ],
        out_specs=[pl.BlockSpec((1, KVX_W, tm), lambda b, i: (b, 0, i)),
                   pl.BlockSpec((1, tm, KVX_W), lambda b, i: (b, i, 0))],
        out_shape=[jax.ShapeDtypeStruct((bn, KVX_W, s), BF16), jax.ShapeDtypeStruct((bn, s, KVX_W), BF16)],
        scratch_shapes=[pltpu.VMEM((D_MODEL, 2 * KV_W), BF16)],
        compiler_params=SEQUENTIAL,
        name="ctx_kv_proj",
    )(x, mod3, mod3, norm_g.reshape(1, D_MODEL), w)


class _Chain:
    def __init__(self, q2, sink_col, kset):
        self.q2, self.sink_col, self.kset = q2, sink_col, kset

    def scores(self):
        self.s = []
        for kt, _, bias in self.kset:
            sc = _dot(self.q2, kt)
            self.s.append(sc if bias is None else sc + bias)

    def softmax(self):
        m = self.sink_col
        for sc in self.s:
            m = jnp.maximum(m, sc.max(axis=-1, keepdims=True))
        self.p = [jnp.exp2(sc - m).astype(BF16) for sc in self.s]
        self.extra = jnp.exp2(self.sink_col - m)
        self.row_max = m
        self.s = None

    def values(self):
        r = None
        for p, (_, v, _) in zip(self.p, self.kset):
            pv = _dot(p, jnp.concatenate([v, jnp.ones_like(v)], axis=1))
            r = pv if r is None else r + pv
        self.p = None
        return r[:, :LANES] / (r[:, LANES:] + self.extra)


def _run_chains(chains, finish, fillers, ahead):
    n = len(chains)
    fillers = list(fillers)
    for i in range(-ahead, n):
        if 0 <= i < n:
            finish(i, chains[i].values())
        if 0 <= i + ahead < n:
            chains[i + ahead].scores()
        if 0 <= i + 1 < n:
            chains[i + 1].softmax()
            if fillers:
                fillers.pop(0)(chains[i + 1].row_max)
    for f in fillers:
        f(chains[-1].row_max)


def _kv_block(kvh, e):
    k0 = (2 * kvh + e) * LANES
    return slice(k0, k0 + LANES)


def _group_chains(sink_ref, q_ref, rows, tq, kvh, ksets):
    c0 = 2 * kvh * LANES
    q2 = jnp.concatenate([q_ref[0, rows, c0:c0 + LANES], q_ref[0, rows, c0 + LANES:c0 + 2 * LANES]], axis=0)
    chains = []
    for e in range(2):
        sink_col = jnp.concatenate([jnp.full((tq, 1), sink_ref[GQA_GROUP * kvh + e] * LOG2E, F32),
                                    jnp.full((tq, 1), sink_ref[GQA_GROUP * kvh + 2 + e] * LOG2E, F32)], axis=0)
        chains.append(_Chain(q2, sink_col, ksets[e]))
    return chains


def _make_finish(groups, g_ref, o_ref):
    partial = {}

    def finish(i, out):
        if i % 2 == 0:
            partial[i // 2] = out
            return
        rows, tq, kvh = groups[i // 2]
        o = partial.pop(i // 2) + out
        c0 = 2 * kvh * LANES
        for hf in range(2):
            cols = slice(c0 + hf * LANES, c0 + (hf + 1) * LANES)
            g = g_ref[0, rows, cols].astype(F32)
            o_ref[0, rows, cols] = (o[hf * tq:(hf + 1) * tq] * g).astype(BF16)
    return finish


def _local_attn_kernel(sink_ref, q_ref, kt_ref, v_ref, ktc_ref, vc_ref, bias_ref, g_ref, uc_ref,
                       cw_ref, cb_ref, lg_ref, lb_ref, o_ref, cv_ref, pad_ref, *, seq, sub, ahead):
    @pl.when(pl.program_id(1) == 0)
    def _():
        _fill_conv_pad(pad_ref, uc_ref, seq)

    nblk = seq // BLOCK
    chains, groups = [], []
    for sb in range(sub):
        n = pl.program_id(1) * sub + sb
        start = pl.multiple_of(jnp.clip((n - 1) * BLOCK, 0, seq - 3 * BLOCK), BLOCK)
        win = pl.ds(start, 3 * BLOCK)
        bias = bias_ref[jnp.where(n == 0, 0, jnp.where(n == nblk - 1, 2, 1))]
        bias2 = jnp.concatenate([bias, bias], axis=0)
        rows = slice(sb * BLOCK, (sb + 1) * BLOCK)
        for kvh in range(N_KV_HEADS):
            ksets = []
            for e in range(2):
                blk = _kv_block(kvh, e)
                ksets.append([(kt_ref[0, blk, win], v_ref[0, win, blk], bias2),
                              (ktc_ref[0, blk, :], vc_ref[0, :, blk], None)])
            chains += _group_chains(sink_ref, q_ref, rows, BLOCK, kvh, ksets)
            groups.append((rows, BLOCK, kvh))
    t0 = pl.multiple_of(pl.program_id(1) * (sub * BLOCK), sub * BLOCK)
    fillers = _conv_fillers(pad_ref, (cw_ref, cb_ref, lg_ref, lb_ref), cv_ref, t0, sub * BLOCK)
    _run_chains(chains, _make_finish(groups, g_ref, o_ref), fillers, ahead)


def _band_bias():
    i = np.arange(BLOCK)[:, None]
    k = np.arange(3 * BLOCK)[None, :]
    tabs = [np.where(np.abs(BLOCK * qb + i - k) <= WINDOW, 0.0, NEG_INF) for qb in range(3)]
    return jnp.asarray(np.stack(tabs).astype(np.float32))


def _local_attention(sink, q, kv, kvc, gatt, uc, conv_args):
    bn, s, _ = q.shape
    n_ctx = kvc[1].shape[1]
    sub = ATT_BLOCKS
    ahead = ATT_AHEAD
    tq = sub * BLOCK
    assert s % tq == 0 and s >= 3 * BLOCK
    tile = lambda w: pl.BlockSpec((1, tq, w), lambda b, n: (b, n, 0))
    whole = lambda rows, w: pl.BlockSpec((1, rows, w), lambda b, n: (b, 0, 0))
    return pl.pallas_call(
        functools.partial(_local_attn_kernel, seq=s, sub=sub, ahead=ahead),
        grid=(bn, s // tq),
        in_specs=[
            pl.BlockSpec(memory_space=pltpu.SMEM),
            tile(ATT_W), whole(KVX_W, s), whole(s, KVX_W),
            pl.BlockSpec((1, KVX_W, n_ctx), lambda b, n: (0, 0, b)), whole(n_ctx, KVX_W),
            pl.BlockSpec((3, BLOCK, 3 * BLOCK), lambda b, n: (0, 0, 0)),
            tile(ATT_W), whole(s, CONV_W),
        ] + _conv_specs(lambda b, n: (0, 0)),
        out_specs=[tile(ATT_W), tile(CONV_W)],
        out_shape=[jax.ShapeDtypeStruct((bn, s, ATT_W), BF16), jax.ShapeDtypeStruct((bn, s, CONV_W), BF16)],
        scratch_shapes=[pltpu.VMEM((s + 2 * CONV_HALO, CONV_W), F32)],
        compiler_params=SEQUENTIAL,
        name="local_attention",
    )(sink, q, *kv, *kvc, _band_bias(), gatt, uc, *conv_args)


def _ctx_attn_kernel(sink_ref, q_ref, ktc_ref, vc_ref, g_ref, uc_ref, cw_ref, cb_ref, lg_ref, lb_ref,
                     o_ref, cv_ref, pad_ref, *, n_ctx, ahead):
    _fill_conv_pad(pad_ref, uc_ref, n_ctx)
    rows = slice(0, n_ctx)
    chains, groups = [], []
    for kvh in range(N_KV_HEADS):
        ksets = []
        for e in range(2):
            blk = _kv_block(kvh, e)
            ksets.append([(ktc_ref[0, blk, :], vc_ref[0, :, blk], None)])
        chains += _group_chains(sink_ref, q_ref, rows, n_ctx, kvh, ksets)
        groups.append((rows, n_ctx, kvh))
    fillers = _conv_fillers(pad_ref, (cw_ref, cb_ref, lg_ref, lb_ref), cv_ref, 0, n_ctx)
    _run_chains(chains, _make_finish(groups, g_ref, o_ref), fillers, ahead)


def _ctx_attention(sink, q, kvc, gatt, uc, conv_args):
    bn, n_ctx, _ = q.shape
    ahead = CTX_AHEAD
    whole = lambda w: pl.BlockSpec((1, n_ctx, w), lambda b: (b, 0, 0))
    return pl.pallas_call(
        functools.partial(_ctx_attn_kernel, n_ctx=n_ctx, ahead=ahead),
        grid=(bn,),
        in_specs=[pl.BlockSpec(memory_space=pltpu.SMEM), whole(ATT_W),
                  pl.BlockSpec((1, KVX_W, n_ctx), lambda b: (0, 0, b)), whole(KVX_W), whole(ATT_W),
                  whole(CONV_W)] + _conv_specs(lambda b: (0, 0)),
        out_specs=[whole(ATT_W), whole(CONV_W)],
        out_shape=[jax.ShapeDtypeStruct((bn, n_ctx, ATT_W), BF16),
                   jax.ShapeDtypeStruct((bn, n_ctx, CONV_W), BF16)],
        scratch_shapes=[pltpu.VMEM((n_ctx + 2 * CONV_HALO, CONV_W), F32)],
        name="ctx_attention",
    )(sink, q, *kvc, gatt, uc, *conv_args)


def _fourier_prepare(ab_ref, cw_ref, sw_ref, rc_ref, rs_ref, half):
    top = ab_ref[0, 0:half, :].astype(F32)
    bot = ab_ref[0, half:2 * half, :].astype(F32)
    plus = top + bot
    minus = top - bot
    a_m, b_m = minus[:, :FOUR_W], minus[:, FOUR_W:]
    cw, sw = cw_ref[...], sw_ref[...]
    rc_ref[:, 0:FOUR_W] = plus[:, :FOUR_W].astype(BF16)
    rc_ref[:, FOUR_W:] = (cw * a_m - sw * b_m).astype(BF16)
    rs_ref[:, 0:FOUR_W] = plus[:, FOUR_W:].astype(BF16)
    rs_ref[:, FOUR_W:] = (sw * a_m + cw * b_m).astype(BF16)


def _fourier_rows(c_ref, s_ref, rc_ref, rs_ref, wf_ref, bf_ref, il_ref, r0, chunk):
    e = _dot(c_ref[pl.ds(r0, chunk), :], rc_ref[...]) - _dot(s_ref[pl.ds(r0, chunk), :], rs_ref[...])
    eb = e.astype(BF16)
    for p in range(2):
        f = _dot(eb[:, p * FOUR_W:(p + 1) * FOUR_W], wf_ref[...])
        for lt in range(FOUR_W // LANES):
            il_ref[lt, pl.ds(p, chunk, stride=2), :] = f[:, lt * LANES:(lt + 1) * LANES]
    return jnp.concatenate([il_ref[lt] for lt in range(FOUR_W // LANES)], axis=1) + bf_ref[...]


def _out_kernel(x_ref, att_ref, cv_ref, gcv_ref, ab_ref, gf_ref, c_ref, s_ref, cw_ref, sw_ref, wf_ref, bf_ref,
                w_ref, gate_ref, fg_ref, o_ref, rc_ref, rs_ref, il_ref, wbf_ref, *, tm, half, final):
    _cast_weight_once(w_ref, wbf_ref)

    @pl.when(pl.program_id(1) == 0)
    def _():
        _fourier_prepare(ab_ref, cw_ref, sw_ref, rc_ref, rs_ref, half)

    r0 = pl.multiple_of(pl.program_id(1) * (tm // 2), tm // 2)
    four = _fourier_rows(c_ref, s_ref, rc_ref, rs_ref, wf_ref, bf_ref, il_ref, r0, tm // 2)
    fo = (four * gf_ref[0].astype(F32)).astype(BF16)
    cv = cv_ref[0] * gcv_ref[0]
    y = _dot(jnp.concatenate([att_ref[0], cv, fo], axis=1), wbf_ref[...])
    xn = x_ref[0] + gate_ref[0] * y
    if final:
        ms = jnp.mean(xn * xn, axis=-1, keepdims=True)
        xn = xn * lax.rsqrt(ms + EPS) * fg_ref[...]
    o_ref[0] = xn


def _out_proj(x, att, cv, gcv, ab, gf, pos_tabs, w_four_bf, b_four, w_out, mod3, final_g,
              *, tm, mod_row, final, layer):
    bn, s, _ = x.shape
    half = s // 2
    c_half, s_half, cw, sw = pos_tabs
    tile = lambda w: pl.BlockSpec((1, tm, w), lambda b, i: (b, i, 0))
    const = lambda shape: pl.BlockSpec(shape, lambda b, i: (0,) * len(shape))
    return pl.pallas_call(
        functools.partial(_out_kernel, tm=tm, half=half, final=final),
        grid=(bn, s // tm),
        in_specs=[
            tile(D_MODEL), tile(ATT_W), tile(CONV_W), tile(CONV_W),
            pl.BlockSpec((1, s, 2 * FOUR_W), lambda b, i: (b, 0, 0)),
            tile(FOUR_W),
            const((half, half)), const((half, half)), const((half, FOUR_W)), const((half, FOUR_W)),
            const((FOUR_W, FOUR_W)), const((1, FOUR_W)),
            pl.BlockSpec((1, D_MODEL, D_MODEL), lambda b, i: (layer, 0, 0)),
            _mod_spec(mod_row, 2),
            const((1, D_MODEL)),
        ],
        out_specs=tile(D_MODEL),
        out_shape=jax.ShapeDtypeStruct((bn, s, D_MODEL), F32),
        scratch_shapes=[pltpu.VMEM((half, 2 * FOUR_W), BF16), pltpu.VMEM((half, 2 * FOUR_W), BF16),
                        pltpu.VMEM((FOUR_W // LANES, tm, LANES), F32),
                        pltpu.VMEM((D_MODEL, D_MODEL), BF16)],
        compiler_params=SEQUENTIAL,
        name="out_proj",
    )(x, att, cv, gcv, ab, gf, c_half, s_half, cw, sw, w_four_bf, b_four.reshape(1, FOUR_W),
      w_out, mod3, final_g.reshape(1, D_MODEL))


def _rope_tables(s):
    t = np.arange(s)
    half = HEAD_DIM // 4
    freqs = ROPE_BASE ** (-np.arange(half) / half)
    zero = np.zeros((s, half))

    def per_head(fn_lo, fn_hi):
        parts = []
        for pos in (t // GRID_W, t % GRID_W):
            ang = pos[:, None] * freqs[None, :]
            parts += [fn_lo(ang), fn_hi(ang)]
        return np.tile(np.concatenate(parts, axis=-1), (1, LANES // HEAD_DIM))

    cos = per_head(np.cos, np.cos)
    sin_lo = per_head(lambda a: -np.sin(a), lambda a: zero)
    sin_hi = per_head(lambda a: zero, np.sin)
    tab = np.concatenate([cos * Q_SCALE, sin_lo * Q_SCALE, sin_hi * Q_SCALE, cos, sin_lo, sin_hi], axis=-1)
    return jnp.asarray(tab.astype(np.float32))


def _identity_rope_tables(s):
    one = np.ones((s, LANES))
    zero = np.zeros((s, LANES))
    return jnp.asarray(np.concatenate([one * Q_SCALE, zero, zero, one, zero, zero], axis=-1).astype(np.float32))


def _channel_dft(n_pos):
    c = np.arange(FOUR_HEAD_DIM)
    ang = 2.0 * np.pi * ((c[:, None] * c[None, :]) % FOUR_HEAD_DIM) / FOUR_HEAD_DIM
    eye = np.eye(FOUR_HEADS)
    scale = 1.0 / np.sqrt(float(n_pos * FOUR_HEAD_DIM))
    tab = np.concatenate([np.kron(eye, np.cos(ang)), np.kron(eye, np.sin(ang))], axis=1) * scale
    return jnp.asarray(tab.astype(np.float32))


def _position_dft(n_pos):
    half = n_pos // 2
    k = np.arange(half)
    ang = 2.0 * np.pi * ((k[:, None] * k[None, :]) % half) / half
    beta = np.pi * k / half
    ones = np.ones((1, FOUR_W))
    f32 = lambda a: jnp.asarray(a.astype(np.float32))
    return (f32(np.cos(ang)).astype(BF16), f32(np.sin(ang)).astype(BF16),
            f32(np.cos(beta)[:, None] * ones), f32(np.sin(beta)[:, None] * ones))


def kernel(x, c, ctx, c_ctx, w_ada, b_ada, norm_g, w_in, attn_sink, conv_w, conv_b,
           conv_ln_g, conv_ln_b, w_four, b_four, w_out, final_g):
    bn, s, _ = x.shape
    n_ctx = ctx.shape[1]
    assert bn < MOD_ROWS
    ctx_row = bn

    cc = jnp.zeros((MOD_ROWS, D_MODEL), F32).at[:bn].set(c).at[ctx_row].set(c_ctx)
    mod = _modulation(cc, w_ada, b_ada)

    w_four_bf = w_four.astype(BF16)
    rope_x = _rope_tables(s)
    rope_c = _identity_rope_tables(bn * n_ctx)
    dft_x = _channel_dft(s).astype(BF16)
    dft_ctx = _channel_dft(n_ctx).astype(BF16)
    pos_x = _position_dft(s)
    pos_ctx = _position_dft(n_ctx)

    h_ctx = ctx
    for l in range(DEPTH):
        mod3 = mod[l].reshape(MOD_ROWS, 1, 3 * D_MODEL)
        conv_args = _conv_args(conv_w[l], conv_b[l], conv_ln_g[l], conv_ln_b[l])
        four_args = (w_four_bf[l], b_four[l], w_out, mod3, final_g)
        sink = attn_sink[l]
        last = l == DEPTH - 1
        ctx_flat = h_ctx.reshape(1, bn * n_ctx, D_MODEL)
        per_batch = lambda a: a.reshape(bn, n_ctx, a.shape[-1])
        if not last:
            qc, ktc, *rest = _in_proj(ctx_flat, mod3, norm_g[l], w_in, rope_c, dft_ctx,
                                      tm=IN_TM, mod_row=ctx_row, layer=l)
            vc, gatt_c, gcv_c, ab_c, gf_c, uc_c = map(per_batch, rest)
            kvc = (ktc, vc)
            att_c, cv_c = _ctx_attention(sink, per_batch(qc), kvc, gatt_c, uc_c, conv_args)
        else:
            ktc, vc = _kv_proj(ctx_flat, mod3, norm_g[l], w_in, tm=IN_TM, mod_row=ctx_row, layer=l)
            kvc = (ktc, per_batch(vc))
        q, kt, v, gatt, gcv, ab, gf, uc = _in_proj(
            x, mod3, norm_g[l], w_in, rope_x, dft_x, tm=IN_TM, mod_row=None, layer=l)
        att, cv = _local_attention(sink, q, (kt, v), kvc, gatt, uc, conv_args)
        if not last:
            h_ctx = _out_proj(h_ctx, att_c, cv_c, gcv_c, ab_c, gf_c, pos_ctx, *four_args,
                              tm=n_ctx, mod_row=ctx_row, final=False, layer=l)
        x = _out_proj(x, att, cv, gcv, ab, gf, pos_x, *four_args, tm=OUT_TM, mod_row=None, final=last, layer=l)
    return x
```

```python
import functools

import numpy as np
import jax
import jax.numpy as jnp
from jax import lax
from jax.experimental import pallas as pl
from jax.experimental.pallas import tpu as pltpu

F32 = jnp.float32
BF16 = jnp.bfloat16

D_MODEL = 1024
DEPTH = 2
GRID_W = 64
HEAD_DIM = 64
ATT_W = 512
N_HEADS = 8
N_KV_HEADS = 2
GQA_GROUP = N_HEADS // N_KV_HEADS
KV_W = N_KV_HEADS * HEAD_DIM
CONV_W = 256
FOUR_W = 256
FOUR_HEADS = 4
FOUR_HEAD_DIM = FOUR_W // FOUR_HEADS
CONV_K = 31
SUBLANES = 8
CONV_HALO = 16
WINDOW = 128
BLOCK = 128
ROPE_BASE = 10000.0
EPS = 1e-6
NEG_INF = -1e30
LOG2E = 1.4426950408889634
Q_SCALE = HEAD_DIM ** -0.5 * LOG2E
IN_W = 2 * ATT_W + 2 * KV_W + 3 * CONV_W + 2 * FOUR_W
OFF_Q = 0
OFF_KV = ATT_W
OFF_GATT = OFF_KV + 2 * KV_W
OFF_CONV = OFF_GATT + ATT_W
OFF_GCONV = OFF_CONV + 2 * CONV_W
OFF_FOUR = OFF_GCONV + CONV_W
OFF_GFOUR = OFF_FOUR + FOUR_W
LANES = 128
KVX_W = 4 * LANES
MOD_ROWS = 16
SEQUENTIAL = pltpu.CompilerParams(dimension_semantics=("arbitrary", "arbitrary"))
ADA_TN = 1536
IN_TM = 1024
OUT_TM = 1024
ATT_BLOCKS = 8
ATT_AHEAD = 8
CTX_AHEAD = 2
WEIGHT_CHUNKS = 4


def _dot(a, b):
    return jnp.dot(a, b, preferred_element_type=F32)


def _silu(x):
    return x * jax.nn.sigmoid(x)


def _ada_kernel(cc_ref, w_ref, b_ref, o_ref):
    a = _silu(cc_ref[...])
    w = w_ref[0]
    a_hi = a.astype(BF16)
    a_lo = (a - a_hi.astype(F32)).astype(BF16)
    w_hi = w.astype(BF16)
    o_ref[0] = _dot(a_hi, w_hi) + _dot(a_lo, w_hi) + b_ref[0]


def _modulation(cc, w_ada, b_ada):
    tn = ADA_TN
    return pl.pallas_call(
        _ada_kernel,
        grid=(DEPTH, 3 * D_MODEL // tn),
        in_specs=[
            pl.BlockSpec((MOD_ROWS, D_MODEL), lambda l, j: (0, 0)),
            pl.BlockSpec((1, D_MODEL, tn), lambda l, j: (l, 0, j)),
            pl.BlockSpec((1, 1, tn), lambda l, j: (l, 0, j)),
        ],
        out_specs=pl.BlockSpec((1, MOD_ROWS, tn), lambda l, j: (l, 0, j)),
        out_shape=jax.ShapeDtypeStruct((DEPTH, MOD_ROWS, 3 * D_MODEL), F32),
        name="ada_modulation",
    )(cc, w_ada, b_ada.reshape(DEPTH, 1, 3 * D_MODEL))


def _zero_after(token):
    bits = pltpu.bitcast(jnp.broadcast_to(token[:SUBLANES], (SUBLANES, LANES)), jnp.uint32)
    zero = lax.shift_right_logical(lax.shift_right_logical(bits, jnp.uint32(16)), jnp.uint32(16))
    return pltpu.bitcast(zero, F32)[:1]


def _conv_tile(pad_ref, w_ref, t0, lt, zero=None):
    lead = CONV_HALO - CONV_K // 2
    lanes = slice(lt * LANES, (lt + 1) * LANES)
    acc = None
    for r in range(SUBLANES):
        z = None
        for a in range((CONV_K - r + SUBLANES - 1) // SUBLANES):
            j = SUBLANES * a + r
            w = w_ref[j:j + 1, lanes]
            term = pad_ref[pl.ds(t0 + SUBLANES * a, BLOCK + SUBLANES), lanes] * (w if zero is None else w + zero)
            z = term if z is None else z + term
        shift = r + lead
        zs = z[shift:shift + BLOCK] if shift % SUBLANES == 0 else pltpu.roll(z, BLOCK + SUBLANES - shift, 0)[:BLOCK]
        acc = zs if acc is None else acc + zs
    return acc


def _conv_finish(tiles, b_ref, lg_ref, lb_ref):
    y = jnp.concatenate(tiles, axis=1) + b_ref[...]
    mu = jnp.mean(y, axis=-1, keepdims=True)
    yc = y - mu
    var = jnp.mean(yc * yc, axis=-1, keepdims=True)
    return _silu(yc * lax.rsqrt(var + EPS) * lg_ref[...] + lb_ref[...]).astype(BF16)


def _fill_conv_pad(pad_ref, u_ref, seq):
    zeros = jnp.zeros((CONV_HALO, CONV_W), F32)
    pad_ref[0:CONV_HALO, :] = zeros
    pad_ref[CONV_HALO + seq:2 * CONV_HALO + seq, :] = zeros
    pad_ref[CONV_HALO:CONV_HALO + seq, :] = u_ref[0].astype(F32)


def _conv_fillers(pad_ref, conv_refs, cv_ref, t0, n_rows):
    w_ref, b_ref, lg_ref, lb_ref = conv_refs
    pieces = []
    for j in range(n_rows // BLOCK):
        tiles = []
        for lt in range(CONV_W // LANES):
            pieces.append(lambda token, j=j, lt=lt, tiles=tiles: tiles.append(
                _conv_tile(pad_ref, w_ref, t0 + j * BLOCK, lt, _zero_after(token))))

        def finish(token, j=j, tiles=tiles):
            cv_ref[0, j * BLOCK:(j + 1) * BLOCK, :] = _conv_finish(tiles, b_ref, lg_ref, lb_ref)
        pieces.append(finish)
    return pieces


def _conv_specs(index):
    vec = pl.BlockSpec((1, CONV_W), index)
    return [pl.BlockSpec((CONV_K, CONV_W), index), vec, vec, vec]


def _conv_args(conv_w, conv_b, ln_g, ln_b):
    row = lambda a: a.reshape(1, CONV_W)
    return conv_w, row(conv_b), row(ln_g), row(ln_b)


def _cast_weight_once(w_ref, wbf_ref):
    @pl.when((pl.program_id(0) == 0) & (pl.program_id(1) == 0))
    def _():
        wbf_ref[...] = w_ref[0].astype(BF16)


def _weight_chunk_copies(w_hbm_ref, layer, wf_ref, sem_ref):
    rows = wf_ref.shape[0] // WEIGHT_CHUNKS
    return [pltpu.make_async_copy(w_hbm_ref.at[layer, pl.ds(k * rows, rows)],
                                  wf_ref.at[pl.ds(k * rows, rows)], sem_ref.at[k])
            for k in range(WEIGHT_CHUNKS)]


def _project(x_ref, sh_ref, sc_ref, g_ref, w_ref, wbf_ref, staged=None):
    if staged is None:
        _cast_weight_once(w_ref, wbf_ref)
    else:
        wf_ref, sem_ref, layer = staged
        copies = _weight_chunk_copies(w_ref, layer, wf_ref, sem_ref)
        rows = wf_ref.shape[0] // WEIGHT_CHUNKS

        @pl.when((pl.program_id(0) == 0) & (pl.program_id(1) == 0))
        def _():
            for cp in copies:
                cp.start()
            for k, cp in enumerate(copies):
                cp.wait()
                wbf_ref[k * rows:(k + 1) * rows, :] = wf_ref[k * rows:(k + 1) * rows, :].astype(BF16)

    x = x_ref[0]
    ms = jnp.mean(x * x, axis=-1, keepdims=True)
    gain = g_ref[...] * (1.0 + sc_ref[0])
    h = x * lax.rsqrt(ms + EPS) * gain + sh_ref[0]
    return _dot(h.astype(BF16), wbf_ref[...])


def _rope_block(blk, rope_ref, base):
    cos = rope_ref[:, base:base + LANES]
    sin_lo = rope_ref[:, base + LANES:base + 2 * LANES]
    sin_hi = rope_ref[:, base + 2 * LANES:base + 3 * LANES]
    return (blk * cos + pltpu.roll(blk, LANES - 16, 1) * sin_lo + pltpu.roll(blk, 16, 1) * sin_hi)


def _store_padded_kv(kt_ref, v_ref, kblk, vblk):
    tm = kblk.shape[0]
    kt = kblk.T.astype(BF16)
    zeros = jnp.zeros((HEAD_DIM, tm), BF16)
    k_blocks = ((kt[:HEAD_DIM], zeros), (zeros, kt[:HEAD_DIM]), (kt[HEAD_DIM:], zeros), (zeros, kt[HEAD_DIM:]))
    for i, halves in enumerate(k_blocks):
        kt_ref[0, i * LANES:(i + 1) * LANES, :] = jnp.concatenate(halves, axis=0)
    lo = lax.broadcasted_iota(jnp.int32, vblk.shape, 1) < HEAD_DIM
    swapped = pltpu.roll(vblk, HEAD_DIM, 1)
    v_blocks = (jnp.where(lo, vblk, 0.0), jnp.where(lo, 0.0, swapped),
                jnp.where(lo, swapped, 0.0), jnp.where(lo, 0.0, vblk))
    for i, b in enumerate(v_blocks):
        v_ref[0, :, i * LANES:(i + 1) * LANES] = b.astype(BF16)


def _project_epilogue(p, rope_ref, dft_ref, q_ref, kt_ref, v_ref, gatt_ref, gcv_ref, ab_ref, gf_ref):
    for c in range(ATT_W // LANES):
        blk = _rope_block(p[:, OFF_Q + c * LANES:OFF_Q + (c + 1) * LANES], rope_ref, 0)
        q_ref[0, :, c * LANES:(c + 1) * LANES] = blk.astype(BF16)
    _store_padded_kv(kt_ref, v_ref, _rope_block(p[:, OFF_KV:OFF_KV + KV_W], rope_ref, 3 * LANES),
                     p[:, OFF_KV + KV_W:OFF_KV + 2 * KV_W])
    gatt_ref[0] = _silu(p[:, OFF_GATT:OFF_GATT + ATT_W]).astype(BF16)
    gcv_ref[0] = _silu(p[:, OFF_GCONV:OFF_GCONV + CONV_W]).astype(BF16)
    fu = p[:, OFF_FOUR:OFF_FOUR + FOUR_W].astype(BF16)
    ab_ref[0] = _dot(fu, dft_ref[...]).astype(BF16)
    gf_ref[0] = _silu(p[:, OFF_GFOUR:OFF_GFOUR + FOUR_W]).astype(BF16)
    return (p[:, OFF_CONV:OFF_CONV + CONV_W] * jax.nn.sigmoid(p[:, OFF_CONV + CONV_W:OFF_CONV + 2 * CONV_W])
            ).astype(BF16)


def _in_kernel(x_ref, sh_ref, sc_ref, g_ref, w_ref, rope_ref, dft_ref,
               q_ref, kt_ref, v_ref, gatt_ref, gcv_ref, ab_ref, gf_ref, uc_ref, wbf_ref, wf_ref, sem_ref, *, layer):
    p = _project(x_ref, sh_ref, sc_ref, g_ref, w_ref, wbf_ref, staged=(wf_ref, sem_ref, layer))
    uc_ref[0] = _project_epilogue(p, rope_ref, dft_ref, q_ref, kt_ref, v_ref, gatt_ref, gcv_ref, ab_ref, gf_ref)


def _mod_spec(mod_row, j):
    row = (lambda b: mod_row) if mod_row is not None else (lambda b: b)
    return pl.BlockSpec((1, 1, D_MODEL), lambda b, i: (row(b), 0, j))


def _in_specs(tm, mod_row):
    return [
        pl.BlockSpec((1, tm, D_MODEL), lambda b, i: (b, i, 0)),
        _mod_spec(mod_row, 0), _mod_spec(mod_row, 1),
        pl.BlockSpec((1, D_MODEL), lambda b, i: (0, 0)),
        pl.BlockSpec(memory_space=pl.ANY),
        pl.BlockSpec((tm, 6 * LANES), lambda b, i: (i, 0)),
        pl.BlockSpec((FOUR_W, 2 * FOUR_W), lambda b, i: (0, 0)),
    ]


def _in_outs(bn, s, tm):
    widths = (ATT_W, None, KVX_W, ATT_W, CONV_W, 2 * FOUR_W, FOUR_W, CONV_W)
    specs = [pl.BlockSpec((1, tm, w), lambda b, i: (b, i, 0)) if w else
             pl.BlockSpec((1, KVX_W, tm), lambda b, i: (b, 0, i)) for w in widths]
    shapes = [jax.ShapeDtypeStruct((bn, s, w) if w else (bn, KVX_W, s), BF16) for w in widths]
    return specs, shapes


def _in_proj(x, mod3, norm_g, w, rope_tab, dft_c, *, tm, mod_row, layer):
    bn, s, _ = x.shape
    specs, shapes = _in_outs(bn, s, tm)
    return pl.pallas_call(
        functools.partial(_in_kernel, layer=layer),
        grid=(bn, s // tm),
        in_specs=_in_specs(tm, mod_row),
        out_specs=specs,
        out_shape=shapes,
        scratch_shapes=[pltpu.VMEM((D_MODEL, IN_W), BF16), pltpu.VMEM((D_MODEL, IN_W), F32),
                        pltpu.SemaphoreType.DMA((WEIGHT_CHUNKS,))],
        compiler_params=SEQUENTIAL,
        name="in_proj",
    )(x, mod3, mod3, norm_g.reshape(1, D_MODEL), w, rope_tab, dft_c)


def _kv_kernel(x_ref, sh_ref, sc_ref, g_ref, w_ref, kt_ref, v_ref, wbf_ref):
    kvp = _project(x_ref, sh_ref, sc_ref, g_ref, w_ref, wbf_ref)
    _store_padded_kv(kt_ref, v_ref, kvp[:, 0:KV_W], kvp[:, KV_W:2 * KV_W])


def _kv_proj(x, mod3, norm_g, w, *, tm, mod_row, layer):
    bn, s, _ = x.shape
    assert OFF_KV % (2 * KV_W) == 0
    return pl.pallas_call(
        _kv_kernel,
        grid=(bn, s // tm),
        in_specs=[
            pl.BlockSpec((1, tm, D_MODEL), lambda b, i: (b, i, 0)),
            _mod_spec(mod_row, 0), _mod_spec(mod_row, 1),
            pl.BlockSpec((1, D_MODEL), lambda b, i: (0, 0)),
            pl.BlockSpec((1, D_MODEL, 2 * KV_W), lambda b, i: (layer, 0, OFF_KV // (2 * KV_W))),
        ],
        out_specs=[pl.BlockSpec((1, KVX_W, tm), lambda b, i: (b, 0, i)),
                   pl.BlockSpec((1, tm, KVX_W), lambda b, i: (b, i, 0))],
        out_shape=[jax.ShapeDtypeStruct((bn, KVX_W, s), BF16), jax.ShapeDtypeStruct((bn, s, KVX_W), BF16)],
        scratch_shapes=[pltpu.VMEM((D_MODEL, 2 * KV_W), BF16)],
        compiler_params=SEQUENTIAL,
        name="ctx_kv_proj",
    )(x, mod3, mod3, norm_g.reshape(1, D_MODEL), w)


class _Chain:
    def __init__(self, q2, sink_col, kset):
        self.q2, self.sink_col, self.kset = q2, sink_col, kset

    def scores(self):
        self.s = []
        for kt, _, bias in self.kset:
            sc = _dot(self.q2, kt)
            self.s.append(sc if bias is None else sc + bias)

    def softmax(self):
        m = self.sink_col
        for sc in self.s:
            m = jnp.maximum(m, sc.max(axis=-1, keepdims=True))
        self.p = [jnp.exp2(sc - m).astype(BF16) for sc in self.s]
        self.extra = jnp.exp2(self.sink_col - m)
        self.row_max = m
        self.s = None

    def values(self):
        r = None
        for p, (_, v, _) in zip(self.p, self.kset):
            pv = _dot(p, jnp.concatenate([v, jnp.ones_like(v)], axis=1))
            r = pv if r is None else r + pv
        self.p = None
        return r[:, :LANES] / (r[:, LANES:] + self.extra)


def _run_chains(chains, finish, fillers, ahead):
    n = len(chains)
    fillers = list(fillers)
    for i in range(-ahead, n):
        if 0 <= i < n:
            finish(i, chains[i].values())
        if 0 <= i + ahead < n:
            chains[i + ahead].scores()
        if 0 <= i + 1 < n:
            chains[i + 1].softmax()
            if fillers:
                fillers.pop(0)(chains[i + 1].row_max)
    for f in fillers:
        f(chains[-1].row_max)


def _kv_block(kvh, e):
    k0 = (2 * kvh + e) * LANES
    return slice(k0, k0 + LANES)


def _group_chains(sink_ref, q_ref, rows, tq, kvh, ksets):
    c0 = 2 * kvh * LANES
    q2 = jnp.concatenate([q_ref[0, rows, c0:c0 + LANES], q_ref[0, rows, c0 + LANES:c0 + 2 * LANES]], axis=0)
    chains = []
    for e in range(2):
        sink_col = jnp.concatenate([jnp.full((tq, 1), sink_ref[GQA_GROUP * kvh + e] * LOG2E, F32),
                                    jnp.full((tq, 1), sink_ref[GQA_GROUP * kvh + 2 + e] * LOG2E, F32)], axis=0)
        chains.append(_Chain(q2, sink_col, ksets[e]))
    return chains


def _make_finish(groups, g_ref, o_ref):
    partial = {}

    def finish(i, out):
        if i % 2 == 0:
            partial[i // 2] = out
            return
        rows, tq, kvh = groups[i // 2]
        o = partial.pop(i // 2) + out
        c0 = 2 * kvh * LANES
        for hf in range(2):
            cols = slice(c0 + hf * LANES, c0 + (hf + 1) * LANES)
            g = g_ref[0, rows, cols].astype(F32)
            o_ref[0, rows, cols] = (o[hf * tq:(hf + 1) * tq] * g).astype(BF16)
    return finish


def _local_attn_kernel(sink_ref, q_ref, kt_ref, v_ref, ktc_ref, vc_ref, bias_ref, g_ref, uc_ref,
                       cw_ref, cb_ref, lg_ref, lb_ref, o_ref, cv_ref, pad_ref, *, seq, sub, ahead):
    @pl.when(pl.program_id(1) == 0)
    def _():
        _fill_conv_pad(pad_ref, uc_ref, seq)

    nblk = seq // BLOCK
    chains, groups = [], []
    for sb in range(sub):
        n = pl.program_id(1) * sub + sb
        start = pl.multiple_of(jnp.clip((n - 1) * BLOCK, 0, seq - 3 * BLOCK), BLOCK)
        win = pl.ds(start, 3 * BLOCK)
        bias = bias_ref[jnp.where(n == 0, 0, jnp.where(n == nblk - 1, 2, 1))]
        bias2 = jnp.concatenate([bias, bias], axis=0)
        rows = slice(sb * BLOCK, (sb + 1) * BLOCK)
        for kvh in range(N_KV_HEADS):
            ksets = []
            for e in range(2):
                blk = _kv_block(kvh, e)
                ksets.append([(kt_ref[0, blk, win], v_ref[0, win, blk], bias2),
                              (ktc_ref[0, blk, :], vc_ref[0, :, blk], None)])
            chains += _group_chains(sink_ref, q_ref, rows, BLOCK, kvh, ksets)
            groups.append((rows, BLOCK, kvh))
    t0 = pl.multiple_of(pl.program_id(1) * (sub * BLOCK), sub * BLOCK)
    fillers = _conv_fillers(pad_ref, (cw_ref, cb_ref, lg_ref, lb_ref), cv_ref, t0, sub * BLOCK)
    _run_chains(chains, _make_finish(groups, g_ref, o_ref), fillers, ahead)


def _band_bias():
    i = np.arange(BLOCK)[:, None]
    k = np.arange(3 * BLOCK)[None, :]
    tabs = [np.where(np.abs(BLOCK * qb + i - k) <= WINDOW, 0.0, NEG_INF) for qb in range(3)]
    return jnp.asarray(np.stack(tabs).astype(np.float32))


def _local_attention(sink, q, kv, kvc, gatt, uc, conv_args):
    bn, s, _ = q.shape
    n_ctx = kvc[1].shape[1]
    sub = ATT_BLOCKS
    ahead = ATT_AHEAD
    tq = sub * BLOCK
    assert s % tq == 0 and s >= 3 * BLOCK
    tile = lambda w: pl.BlockSpec((1, tq, w), lambda b, n: (b, n, 0))
    whole = lambda rows, w: pl.BlockSpec((1, rows, w), lambda b, n: (b, 0, 0))
    return pl.pallas_call(
        functools.partial(_local_attn_kernel, seq=s, sub=sub, ahead=ahead),
        grid=(bn, s // tq),
        in_specs=[
            pl.BlockSpec(memory_space=pltpu.SMEM),
            tile(ATT_W), whole(KVX_W, s), whole(s, KVX_W),
            pl.BlockSpec((1, KVX_W, n_ctx), lambda b, n: (0, 0, b)), whole(n_ctx, KVX_W),
            pl.BlockSpec((3, BLOCK, 3 * BLOCK), lambda b, n: (0, 0, 0)),
            tile(ATT_W), whole(s, CONV_W),
        ] + _conv_specs(lambda b, n: (0, 0)),
        out_specs=[tile(ATT_W), tile(CONV_W)],
        out_shape=[jax.ShapeDtypeStruct((bn, s, ATT_W), BF16), jax.ShapeDtypeStruct((bn, s, CONV_W), BF16)],
        scratch_shapes=[pltpu.VMEM((s + 2 * CONV_HALO, CONV_W), F32)],
        compiler_params=SEQUENTIAL,
        name="local_attention",
    )(sink, q, *kv, *kvc, _band_bias(), gatt, uc, *conv_args)


def _ctx_attn_kernel(sink_ref, q_ref, ktc_ref, vc_ref, g_ref, uc_ref, cw_ref, cb_ref, lg_ref, lb_ref,
                     o_ref, cv_ref, pad_ref, *, n_ctx, ahead):
    _fill_conv_pad(pad_ref, uc_ref, n_ctx)
    rows = slice(0, n_ctx)
    chains, groups = [], []
    for kvh in range(N_KV_HEADS):
        ksets = []
        for e in range(2):
            blk = _kv_block(kvh, e)
            ksets.append([(ktc_ref[0, blk, :], vc_ref[0, :, blk], None)])
        chains += _group_chains(sink_ref, q_ref, rows, n_ctx, kvh, ksets)
        groups.append((rows, n_ctx, kvh))
    fillers = _conv_fillers(pad_ref, (cw_ref, cb_ref, lg_ref, lb_ref), cv_ref, 0, n_ctx)
    _run_chains(chains, _make_finish(groups, g_ref, o_ref), fillers, ahead)


def _ctx_attention(sink, q, kvc, gatt, uc, conv_args):
    bn, n_ctx, _ = q.shape
    ahead = CTX_AHEAD
    whole = lambda w: pl.BlockSpec((1, n_ctx, w), lambda b: (b, 0, 0))
    return pl.pallas_call(
        functools.partial(_ctx_attn_kernel, n_ctx=n_ctx, ahead=ahead),
        grid=(bn,),
        in_specs=[pl.BlockSpec(memory_space=pltpu.SMEM), whole(ATT_W),
                  pl.BlockSpec((1, KVX_W, n_ctx), lambda b: (0, 0, b)), whole(KVX_W), whole(ATT_W),
                  whole(CONV_W)] + _conv_specs(lambda b: (0, 0)),
        out_specs=[whole(ATT_W), whole(CONV_W)],
        out_shape=[jax.ShapeDtypeStruct((bn, n_ctx, ATT_W), BF16),
                   jax.ShapeDtypeStruct((bn, n_ctx, CONV_W), BF16)],
        scratch_shapes=[pltpu.VMEM((n_ctx + 2 * CONV_HALO, CONV_W), F32)],
        name="ctx_attention",
    )(sink, q, *kvc, gatt, uc, *conv_args)


def _fourier_prepare(ab_ref, cw_ref, sw_ref, rc_ref, rs_ref, half):
    top = ab_ref[0, 0:half, :].astype(F32)
    bot = ab_ref[0, half:2 * half, :].astype(F32)
    plus = top + bot
    minus = top - bot
    a_m, b_m = minus[:, :FOUR_W], minus[:, FOUR_W:]
    cw, sw = cw_ref[...], sw_ref[...]
    rc_ref[:, 0:FOUR_W] = plus[:, :FOUR_W].astype(BF16)
    rc_ref[:, FOUR_W:] = (cw * a_m - sw * b_m).astype(BF16)
    rs_ref[:, 0:FOUR_W] = plus[:, FOUR_W:].astype(BF16)
    rs_ref[:, FOUR_W:] = (sw * a_m + cw * b_m).astype(BF16)


def _fourier_rows(c_ref, s_ref, rc_ref, rs_ref, wf_ref, bf_ref, il_ref, r0, chunk):
    e = _dot(c_ref[pl.ds(r0, chunk), :], rc_ref[...]) - _dot(s_ref[pl.ds(r0, chunk), :], rs_ref[...])
    eb = e.astype(BF16)
    for p in range(2):
        f = _dot(eb[:, p * FOUR_W:(p + 1) * FOUR_W], wf_ref[...])
        for lt in range(FOUR_W // LANES):
            il_ref[lt, pl.ds(p, chunk, stride=2), :] = f[:, lt * LANES:(lt + 1) * LANES]
    return jnp.concatenate([il_ref[lt] for lt in range(FOUR_W // LANES)], axis=1) + bf_ref[...]


def _out_kernel(x_ref, att_ref, cv_ref, gcv_ref, ab_ref, gf_ref, c_ref, s_ref, cw_ref, sw_ref, wf_ref, bf_ref,
                w_ref, gate_ref, fg_ref, o_ref, rc_ref, rs_ref, il_ref, wbf_ref, *, tm, half, final):
    _cast_weight_once(w_ref, wbf_ref)

    @pl.when(pl.program_id(1) == 0)
    def _():
        _fourier_prepare(ab_ref, cw_ref, sw_ref, rc_ref, rs_ref, half)

    r0 = pl.multiple_of(pl.program_id(1) * (tm // 2), tm // 2)
    four = _fourier_rows(c_ref, s_ref, rc_ref, rs_ref, wf_ref, bf_ref, il_ref, r0, tm // 2)
    fo = (four * gf_ref[0].astype(F32)).astype(BF16)
    cv = cv_ref[0] * gcv_ref[0]
    y = _dot(jnp.concatenate([att_ref[0], cv, fo], axis=1), wbf_ref[...])
    xn = x_ref[0] + gate_ref[0] * y
    if final:
        ms = jnp.mean(xn * xn, axis=-1, keepdims=True)
        xn = xn * lax.rsqrt(ms + EPS) * fg_ref[...]
    o_ref[0] = xn


def _out_proj(x, att, cv, gcv, ab, gf, pos_tabs, w_four_bf, b_four, w_out, mod3, final_g,
              *, tm, mod_row, final, layer):
    bn, s, _ = x.shape
    half = s // 2
    c_half, s_half, cw, sw = pos_tabs
    tile = lambda w: pl.BlockSpec((1, tm, w), lambda b, i: (b, i, 0))
    const = lambda shape: pl.BlockSpec(shape, lambda b, i: (0,) * len(shape))
    return pl.pallas_call(
        functools.partial(_out_kernel, tm=tm, half=half, final=final),
        grid=(bn, s // tm),
        in_specs=[
            tile(D_MODEL), tile(ATT_W), tile(CONV_W), tile(CONV_W),
            pl.BlockSpec((1, s, 2 * FOUR_W), lambda b, i: (b, 0, 0)),
            tile(FOUR_W),
            const((half, half)), const((half, half)), const((half, FOUR_W)), const((half, FOUR_W)),
            const((FOUR_W, FOUR_W)), const((1, FOUR_W)),
            pl.BlockSpec((1, D_MODEL, D_MODEL), lambda b, i: (layer, 0, 0)),
            _mod_spec(mod_row, 2),
            const((1, D_MODEL)),
        ],
        out_specs=tile(D_MODEL),
        out_shape=jax.ShapeDtypeStruct((bn, s, D_MODEL), F32),
        scratch_shapes=[pltpu.VMEM((half, 2 * FOUR_W), BF16), pltpu.VMEM((half, 2 * FOUR_W), BF16),
                        pltpu.VMEM((FOUR_W // LANES, tm, LANES), F32),
                        pltpu.VMEM((D_MODEL, D_MODEL), BF16)],
        compiler_params=SEQUENTIAL,
        name="out_proj",
    )(x, att, cv, gcv, ab, gf, c_half, s_half, cw, sw, w_four_bf, b_four.reshape(1, FOUR_W),
      w_out, mod3, final_g.reshape(1, D_MODEL))


def _rope_tables(s):
    t = np.arange(s)
    half = HEAD_DIM // 4
    freqs = ROPE_BASE ** (-np.arange(half) / half)
    zero = np.zeros((s, half))

    def per_head(fn_lo, fn_hi):
        parts = []
        for pos in (t // GRID_W, t % GRID_W):
            ang = pos[:, None] * freqs[None, :]
            parts += [fn_lo(ang), fn_hi(ang)]
        return np.tile(np.concatenate(parts, axis=-1), (1, LANES // HEAD_DIM))

    cos = per_head(np.cos, np.cos)
    sin_lo = per_head(lambda a: -np.sin(a), lambda a: zero)
    sin_hi = per_head(lambda a: zero, np.sin)
    tab = np.concatenate([cos * Q_SCALE, sin_lo * Q_SCALE, sin_hi * Q_SCALE, cos, sin_lo, sin_hi], axis=-1)
    return jnp.asarray(tab.astype(np.float32))


def _identity_rope_tables(s):
    one = np.ones((s, LANES))
    zero = np.zeros((s, LANES))
    return jnp.asarray(np.concatenate([one * Q_SCALE, zero, zero, one, zero, zero], axis=-1).astype(np.float32))


def _channel_dft(n_pos):
    c = np.arange(FOUR_HEAD_DIM)
    ang = 2.0 * np.pi * ((c[:, None] * c[None, :]) % FOUR_HEAD_DIM) / FOUR_HEAD_DIM
    eye = np.eye(FOUR_HEADS)
    scale = 1.0 / np.sqrt(float(n_pos * FOUR_HEAD_DIM))
    tab = np.concatenate([np.kron(eye, np.cos(ang)), np.kron(eye, np.sin(ang))], axis=1) * scale
    return jnp.asarray(tab.astype(np.float32))


def _position_dft(n_pos):
    half = n_pos // 2
    k = np.arange(half)
    ang = 2.0 * np.pi * ((k[:, None] * k[None, :]) % half) / half
    beta = np.pi * k / half
    ones = np.ones((1, FOUR_W))
    f32 = lambda a: jnp.asarray(a.astype(np.float32))
    return (f32(np.cos(ang)).astype(BF16), f32(np.sin(ang)).astype(BF16),
            f32(np.cos(beta)[:, None] * ones), f32(np.sin(beta)[:, None] * ones))


def kernel(x, c, ctx, c_ctx, w_ada, b_ada, norm_g, w_in, attn_sink, conv_w, conv_b,
           conv_ln_g, conv_ln_b, w_four, b_four, w_out, final_g):
    bn, s, _ = x.shape
    n_ctx = ctx.shape[1]
    assert bn < MOD_ROWS
    ctx_row = bn

    cc = jnp.zeros((MOD_ROWS, D_MODEL), F32).at[:bn].set(c).at[ctx_row].set(c_ctx)
    mod = _modulation(cc, w_ada, b_ada)

    w_four_bf = w_four.astype(BF16)
    rope_x = _rope_tables(s)
    rope_c = _identity_rope_tables(bn * n_ctx)
    dft_x = _channel_dft(s).astype(BF16)
    dft_ctx = _channel_dft(n_ctx).astype(BF16)
    pos_x = _position_dft(s)
    pos_ctx = _position_dft(n_ctx)

    h_ctx = ctx
    for l in range(DEPTH):
        mod3 = mod[l].reshape(MOD_ROWS, 1, 3 * D_MODEL)
        conv_args = _conv_args(conv_w[l], conv_b[l], conv_ln_g[l], conv_ln_b[l])
        four_args = (w_four_bf[l], b_four[l], w_out, mod3, final_g)
        sink = attn_sink[l]
        last = l == DEPTH - 1
        ctx_flat = h_ctx.reshape(1, bn * n_ctx, D_MODEL)
        per_batch = lambda a: a.reshape(bn, n_ctx, a.shape[-1])
        if not last:
            qc, ktc, *rest = _in_proj(ctx_flat, mod3, norm_g[l], w_in, rope_c, dft_ctx,
                                      tm=IN_TM, mod_row=ctx_row, layer=l)
            vc, gatt_c, gcv_c, ab_c, gf_c, uc_c = map(per_batch, rest)
            kvc = (ktc, vc)
            att_c, cv_c = _ctx_attention(sink, per_batch(qc), kvc, gatt_c, uc_c, conv_args)
        else:
            ktc, vc = _kv_proj(ctx_flat, mod3, norm_g[l], w_in, tm=IN_TM, mod_row=ctx_row, layer=l)
            kvc = (ktc, per_batch(vc))
        q, kt, v, gatt, gcv, ab, gf, uc = _in_proj(
            x, mod3, norm_g[l], w_in, rope_x, dft_x, tm=IN_TM, mod_row=None, layer=l)
        att, cv = _local_attention(sink, q, (kt, v), kvc, gatt, uc, conv_args)
        if not last:
            h_ctx = _out_proj(h_ctx, att_c, cv_c, gcv_c, ab_c, gf_c, pos_ctx, *four_args,
                              tm=n_ctx, mod_row=ctx_row, final=False, layer=l)
        x = _out_proj(x, att, cv, gcv, ab, gf, pos_x, *four_args, tm=OUT_TM, mod_row=None, final=last, layer=l)
    return x
```

```python
import functools

import numpy as np
import jax
import jax.numpy as jnp
from jax import lax
from jax.experimental import pallas as pl
from jax.experimental.pallas import tpu as pltpu

F32 = jnp.float32
BF16 = jnp.bfloat16

D_MODEL = 1024
DEPTH = 2
GRID_W = 64
HEAD_DIM = 64
ATT_W = 512
N_HEADS = 8
N_KV_HEADS = 2
GQA_GROUP = N_HEADS // N_KV_HEADS
KV_W = N_KV_HEADS * HEAD_DIM
CONV_W = 256
FOUR_W = 256
FOUR_HEADS = 4
FOUR_HEAD_DIM = FOUR_W // FOUR_HEADS
CONV_K = 31
SUBLANES = 8
CONV_HALO = 16
WINDOW = 128
BLOCK = 128
ROPE_BASE = 10000.0
EPS = 1e-6
NEG_INF = -1e30
LOG2E = 1.4426950408889634
Q_SCALE = HEAD_DIM ** -0.5 * LOG2E
IN_W = 2 * ATT_W + 2 * KV_W + 3 * CONV_W + 2 * FOUR_W
OFF_Q = 0
OFF_KV = ATT_W
OFF_GATT = OFF_KV + 2 * KV_W
OFF_CONV = OFF_GATT + ATT_W
OFF_GCONV = OFF_CONV + 2 * CONV_W
OFF_FOUR = OFF_GCONV + CONV_W
OFF_GFOUR = OFF_FOUR + FOUR_W
LANES = 128
KVX_W = 4 * LANES
MOD_ROWS = 16
SEQUENTIAL = pltpu.CompilerParams(dimension_semantics=("arbitrary", "arbitrary"))
ADA_TN = 1536
IN_TM = 1024
OUT_TM = 1024
ATT_BLOCKS = 8
ATT_AHEAD = 8
CTX_AHEAD = 2


def _dot(a, b):
    return jnp.dot(a, b, preferred_element_type=F32)


def _silu(x):
    return x * jax.nn.sigmoid(x)


def _ada_kernel(cc_ref, w_ref, b_ref, o_ref):
    a = _silu(cc_ref[...])
    w = w_ref[0]
    a_hi = a.astype(BF16)
    a_lo = (a - a_hi.astype(F32)).astype(BF16)
    w_hi = w.astype(BF16)
    o_ref[0] = _dot(a_hi, w_hi) + _dot(a_lo, w_hi) + b_ref[0]


def _modulation(cc, w_ada, b_ada):
    tn = ADA_TN
    return pl.pallas_call(
        _ada_kernel,
        grid=(DEPTH, 3 * D_MODEL // tn),
        in_specs=[
            pl.BlockSpec((MOD_ROWS, D_MODEL), lambda l, j: (0, 0)),
            pl.BlockSpec((1, D_MODEL, tn), lambda l, j: (l, 0, j)),
            pl.BlockSpec((1, 1, tn), lambda l, j: (l, 0, j)),
        ],
        out_specs=pl.BlockSpec((1, MOD_ROWS, tn), lambda l, j: (l, 0, j)),
        out_shape=jax.ShapeDtypeStruct((DEPTH, MOD_ROWS, 3 * D_MODEL), F32),
        name="ada_modulation",
    )(cc, w_ada, b_ada.reshape(DEPTH, 1, 3 * D_MODEL))


def _zero_after(token):
    bits = pltpu.bitcast(jnp.broadcast_to(token[:SUBLANES], (SUBLANES, LANES)), jnp.uint32)
    zero = lax.shift_right_logical(lax.shift_right_logical(bits, jnp.uint32(16)), jnp.uint32(16))
    return pltpu.bitcast(zero, F32)[:1]


def _conv_tile(pad_ref, w_ref, t0, lt, zero=None):
    lead = CONV_HALO - CONV_K // 2
    lanes = slice(lt * LANES, (lt + 1) * LANES)
    acc = None
    for r in range(SUBLANES):
        z = None
        for a in range((CONV_K - r + SUBLANES - 1) // SUBLANES):
            j = SUBLANES * a + r
            w = w_ref[j:j + 1, lanes]
            term = pad_ref[pl.ds(t0 + SUBLANES * a, BLOCK + SUBLANES), lanes] * (w if zero is None else w + zero)
            z = term if z is None else z + term
        shift = r + lead
        zs = z[shift:shift + BLOCK] if shift % SUBLANES == 0 else pltpu.roll(z, BLOCK + SUBLANES - shift, 0)[:BLOCK]
        acc = zs if acc is None else acc + zs
    return acc


def _conv_finish(tiles, b_ref, lg_ref, lb_ref):
    y = jnp.concatenate(tiles, axis=1) + b_ref[...]
    mu = jnp.mean(y, axis=-1, keepdims=True)
    yc = y - mu
    var = jnp.mean(yc * yc, axis=-1, keepdims=True)
    return _silu(yc * lax.rsqrt(var + EPS) * lg_ref[...] + lb_ref[...]).astype(BF16)


def _fill_conv_pad(pad_ref, u_ref, seq):
    zeros = jnp.zeros((CONV_HALO, CONV_W), F32)
    pad_ref[0:CONV_HALO, :] = zeros
    pad_ref[CONV_HALO + seq:2 * CONV_HALO + seq, :] = zeros
    pad_ref[CONV_HALO:CONV_HALO + seq, :] = u_ref[0].astype(F32)


def _conv_fillers(pad_ref, conv_refs, cv_ref, t0, n_rows):
    w_ref, b_ref, lg_ref, lb_ref = conv_refs
    pieces = []
    for j in range(n_rows // BLOCK):
        tiles = []
        for lt in range(CONV_W // LANES):
            pieces.append(lambda token, j=j, lt=lt, tiles=tiles: tiles.append(
                _conv_tile(pad_ref, w_ref, t0 + j * BLOCK, lt, _zero_after(token))))

        def finish(token, j=j, tiles=tiles):
            cv_ref[0, j * BLOCK:(j + 1) * BLOCK, :] = _conv_finish(tiles, b_ref, lg_ref, lb_ref)
        pieces.append(finish)
    return pieces


def _conv_specs(index):
    vec = pl.BlockSpec((1, CONV_W), index)
    return [pl.BlockSpec((CONV_K, CONV_W), index), vec, vec, vec]


def _conv_args(conv_w, conv_b, ln_g, ln_b):
    row = lambda a: a.reshape(1, CONV_W)
    return conv_w, row(conv_b), row(ln_g), row(ln_b)


def _cast_weight_once(w_ref, wbf_ref):
    @pl.when((pl.program_id(0) == 0) & (pl.program_id(1) == 0))
    def _():
        wbf_ref[...] = w_ref[0].astype(BF16)


def _project(x_ref, sh_ref, sc_ref, g_ref, w_ref, wbf_ref):
    _cast_weight_once(w_ref, wbf_ref)
    x = x_ref[0]
    ms = jnp.mean(x * x, axis=-1, keepdims=True)
    gain = g_ref[...] * (1.0 + sc_ref[0])
    h = x * lax.rsqrt(ms + EPS) * gain + sh_ref[0]
    return _dot(h.astype(BF16), wbf_ref[...])


def _rope_block(blk, rope_ref, base):
    cos = rope_ref[:, base:base + LANES]
    sin_lo = rope_ref[:, base + LANES:base + 2 * LANES]
    sin_hi = rope_ref[:, base + 2 * LANES:base + 3 * LANES]
    return (blk * cos + pltpu.roll(blk, LANES - 16, 1) * sin_lo + pltpu.roll(blk, 16, 1) * sin_hi)


def _store_padded_kv(kt_ref, v_ref, kblk, vblk):
    tm = kblk.shape[0]
    kt = kblk.T.astype(BF16)
    zeros = jnp.zeros((HEAD_DIM, tm), BF16)
    k_blocks = ((kt[:HEAD_DIM], zeros), (zeros, kt[:HEAD_DIM]), (kt[HEAD_DIM:], zeros), (zeros, kt[HEAD_DIM:]))
    for i, halves in enumerate(k_blocks):
        kt_ref[0, i * LANES:(i + 1) * LANES, :] = jnp.concatenate(halves, axis=0)
    lo = lax.broadcasted_iota(jnp.int32, vblk.shape, 1) < HEAD_DIM
    swapped = pltpu.roll(vblk, HEAD_DIM, 1)
    v_blocks = (jnp.where(lo, vblk, 0.0), jnp.where(lo, 0.0, swapped),
                jnp.where(lo, swapped, 0.0), jnp.where(lo, 0.0, vblk))
    for i, b in enumerate(v_blocks):
        v_ref[0, :, i * LANES:(i + 1) * LANES] = b.astype(BF16)


def _project_epilogue(p, rope_ref, dft_ref, q_ref, kt_ref, v_ref, gatt_ref, gcv_ref, ab_ref, gf_ref):
    for c in range(ATT_W // LANES):
        blk = _rope_block(p[:, OFF_Q + c * LANES:OFF_Q + (c + 1) * LANES], rope_ref, 0)
        q_ref[0, :, c * LANES:(c + 1) * LANES] = blk.astype(BF16)
    _store_padded_kv(kt_ref, v_ref, _rope_block(p[:, OFF_KV:OFF_KV + KV_W], rope_ref, 3 * LANES),
                     p[:, OFF_KV + KV_W:OFF_KV + 2 * KV_W])
    gatt_ref[0] = _silu(p[:, OFF_GATT:OFF_GATT + ATT_W]).astype(BF16)
    gcv_ref[0] = _silu(p[:, OFF_GCONV:OFF_GCONV + CONV_W]).astype(BF16)
    fu = p[:, OFF_FOUR:OFF_FOUR + FOUR_W].astype(BF16)
    ab_ref[0] = _dot(fu, dft_ref[...]).astype(BF16)
    gf_ref[0] = _silu(p[:, OFF_GFOUR:OFF_GFOUR + FOUR_W]).astype(BF16)
    return (p[:, OFF_CONV:OFF_CONV + CONV_W] * jax.nn.sigmoid(p[:, OFF_CONV + CONV_W:OFF_CONV + 2 * CONV_W])
            ).astype(BF16)


def _in_kernel(x_ref, sh_ref, sc_ref, g_ref, w_ref, rope_ref, dft_ref,
               q_ref, kt_ref, v_ref, gatt_ref, gcv_ref, ab_ref, gf_ref, uc_ref, wbf_ref):
    p = _project(x_ref, sh_ref, sc_ref, g_ref, w_ref, wbf_ref)
    uc_ref[0] = _project_epilogue(p, rope_ref, dft_ref, q_ref, kt_ref, v_ref, gatt_ref, gcv_ref, ab_ref, gf_ref)


def _mod_spec(mod_row, j):
    row = (lambda b: mod_row) if mod_row is not None else (lambda b: b)
    return pl.BlockSpec((1, 1, D_MODEL), lambda b, i: (row(b), 0, j))


def _in_specs(tm, mod_row, layer):
    return [
        pl.BlockSpec((1, tm, D_MODEL), lambda b, i: (b, i, 0)),
        _mod_spec(mod_row, 0), _mod_spec(mod_row, 1),
        pl.BlockSpec((1, D_MODEL), lambda b, i: (0, 0)),
        pl.BlockSpec((1, D_MODEL, IN_W), lambda b, i: (layer, 0, 0)),
        pl.BlockSpec((tm, 6 * LANES), lambda b, i: (i, 0)),
        pl.BlockSpec((FOUR_W, 2 * FOUR_W), lambda b, i: (0, 0)),
    ]


def _in_outs(bn, s, tm):
    widths = (ATT_W, None, KVX_W, ATT_W, CONV_W, 2 * FOUR_W, FOUR_W, CONV_W)
    specs = [pl.BlockSpec((1, tm, w), lambda b, i: (b, i, 0)) if w else
             pl.BlockSpec((1, KVX_W, tm), lambda b, i: (b, 0, i)) for w in widths]
    shapes = [jax.ShapeDtypeStruct((bn, s, w) if w else (bn, KVX_W, s), BF16) for w in widths]
    return specs, shapes


def _in_proj(x, mod3, norm_g, w, rope_tab, dft_c, *, tm, mod_row, layer):
    bn, s, _ = x.shape
    specs, shapes = _in_outs(bn, s, tm)
    return pl.pallas_call(
        _in_kernel,
        grid=(bn, s // tm),
        in_specs=_in_specs(tm, mod_row, layer),
        out_specs=specs,
        out_shape=shapes,
        scratch_shapes=[pltpu.VMEM((D_MODEL, IN_W), BF16)],
        compiler_params=SEQUENTIAL,
        name="in_proj",
    )(x, mod3, mod3, norm_g.reshape(1, D_MODEL), w, rope_tab, dft_c)


def _kv_kernel(x_ref, sh_ref, sc_ref, g_ref, w_ref, kt_ref, v_ref, wbf_ref):
    kvp = _project(x_ref, sh_ref, sc_ref, g_ref, w_ref, wbf_ref)
    _store_padded_kv(kt_ref, v_ref, kvp[:, 0:KV_W], kvp[:, KV_W:2 * KV_W])


def _kv_proj(x, mod3, norm_g, w, *, tm, mod_row, layer):
    bn, s, _ = x.shape
    assert OFF_KV % (2 * KV_W) == 0
    return pl.pallas_call(
        _kv_kernel,
        grid=(bn, s // tm),
        in_specs=[
            pl.BlockSpec((1, tm, D_MODEL), lambda b, i: (b, i, 0)),
            _mod_spec(mod_row, 0), _mod_spec(mod_row, 1),
            pl.BlockSpec((1, D_MODEL), lambda b, i: (0, 0)),
            pl.BlockSpec((1, D_MODEL, 2 * KV_W), lambda b, i: (layer, 0, OFF_KV // (2 * KV_W))),
        ],
        out_specs=[pl.BlockSpec((1, KVX_W, tm), lambda b, i: (b, 0, i)),
                   pl.BlockSpec((1, tm, KVX_W), lambda b, i: (b, i, 0))],
        out_shape=[jax.ShapeDtypeStruct((bn, KVX_W, s), BF16), jax.ShapeDtypeStruct((bn, s, KVX_W), BF16)],
        scratch_shapes=[pltpu.VMEM((D_MODEL, 2 * KV_W), BF16)],
        compiler_params=SEQUENTIAL,
        name="ctx_kv_proj",
    )(x, mod3, mod3, norm_g.reshape(1, D_MODEL), w)


class _Chain:
    def __init__(self, q2, sink_col, kset):
        self.q2, self.sink_col, self.kset = q2, sink_col, kset

    def scores(self):
        self.s = []
        for kt, _, bias in self.kset:
            sc = _dot(self.q2, kt)
            self.s.append(sc if bias is None else sc + bias)

    def softmax(self):
        m = self.sink_col
        for sc in self.s:
            m = jnp.maximum(m, sc.max(axis=-1, keepdims=True))
        self.p = [jnp.exp2(sc - m).astype(BF16) for sc in self.s]
        self.extra = jnp.exp2(self.sink_col - m)
        self.row_max = m
        self.s = None

    def values(self):
        r = None
        for p, (_, v, _) in zip(self.p, self.kset):
            pv = _dot(p, jnp.concatenate([v, jnp.ones_like(v)], axis=1))
            r = pv if r is None else r + pv
        self.p = None
        return r[:, :LANES] / (r[:, LANES:] + self.extra)


def _run_chains(chains, finish, fillers, ahead):
    n = len(chains)
    fillers = list(fillers)
    for i in range(-ahead, n):
        if 0 <= i < n:
            finish(i, chains[i].values())
        if 0 <= i + ahead < n:
            chains[i + ahead].scores()
        if 0 <= i + 1 < n:
            chains[i + 1].softmax()
            if fillers:
                fillers.pop(0)(chains[i + 1].row_max)
    for f in fillers:
        f(chains[-1].row_max)


def _kv_block(kvh, e):
    k0 = (2 * kvh + e) * LANES
    return slice(k0, k0 + LANES)


def _group_chains(sink_ref, q_ref, rows, tq, kvh, ksets):
    c0 = 2 * kvh * LANES
    q2 = jnp.concatenate([q_ref[0, rows, c0:c0 + LANES], q_ref[0, rows, c0 + LANES:c0 + 2 * LANES]], axis=0)
    chains = []
    for e in range(2):
        sink_col = jnp.concatenate([jnp.full((tq, 1), sink_ref[GQA_GROUP * kvh + e] * LOG2E, F32),
                                    jnp.full((tq, 1), sink_ref[GQA_GROUP * kvh + 2 + e] * LOG2E, F32)], axis=0)
        chains.append(_Chain(q2, sink_col, ksets[e]))
    return chains


def _make_finish(groups, g_ref, o_ref):
    partial = {}

    def finish(i, out):
        if i % 2 == 0:
            partial[i // 2] = out
            return
        rows, tq, kvh = groups[i // 2]
        o = partial.pop(i // 2) + out
        c0 = 2 * kvh * LANES
        for hf in range(2):
            cols = slice(c0 + hf * LANES, c0 + (hf + 1) * LANES)
            g = g_ref[0, rows, cols].astype(F32)
            o_ref[0, rows, cols] = (o[hf * tq:(hf + 1) * tq] * g).astype(BF16)
    return finish


def _local_attn_kernel(sink_ref, q_ref, kt_ref, v_ref, ktc_ref, vc_ref, bias_ref, g_ref, uc_ref,
                       cw_ref, cb_ref, lg_ref, lb_ref, o_ref, cv_ref, pad_ref, *, seq, sub, ahead):
    @pl.when(pl.program_id(1) == 0)
    def _():
        _fill_conv_pad(pad_ref, uc_ref, seq)

    nblk = seq // BLOCK
    chains, groups = [], []
    for sb in range(sub):
        n = pl.program_id(1) * sub + sb
        start = pl.multiple_of(jnp.clip((n - 1) * BLOCK, 0, seq - 3 * BLOCK), BLOCK)
        win = pl.ds(start, 3 * BLOCK)
        bias = bias_ref[jnp.where(n == 0, 0, jnp.where(n == nblk - 1, 2, 1))]
        bias2 = jnp.concatenate([bias, bias], axis=0)
        rows = slice(sb * BLOCK, (sb + 1) * BLOCK)
        for kvh in range(N_KV_HEADS):
            ksets = []
            for e in range(2):
                blk = _kv_block(kvh, e)
                ksets.append([(kt_ref[0, blk, win], v_ref[0, win, blk], bias2),
                              (ktc_ref[0, blk, :], vc_ref[0, :, blk], None)])
            chains += _group_chains(sink_ref, q_ref, rows, BLOCK, kvh, ksets)
            groups.append((rows, BLOCK, kvh))
    t0 = pl.multiple_of(pl.program_id(1) * (sub * BLOCK), sub * BLOCK)
    fillers = _conv_fillers(pad_ref, (cw_ref, cb_ref, lg_ref, lb_ref), cv_ref, t0, sub * BLOCK)
    _run_chains(chains, _make_finish(groups, g_ref, o_ref), fillers, ahead)


def _band_bias():
    i = np.arange(BLOCK)[:, None]
    k = np.arange(3 * BLOCK)[None, :]
    tabs = [np.where(np.abs(BLOCK * qb + i - k) <= WINDOW, 0.0, NEG_INF) for qb in range(3)]
    return jnp.asarray(np.stack(tabs).astype(np.float32))


def _local_attention(sink, q, kv, kvc, gatt, uc, conv_args):
    bn, s, _ = q.shape
    n_ctx = kvc[1].shape[1]
    sub = ATT_BLOCKS
    ahead = ATT_AHEAD
    tq = sub * BLOCK
    assert s % tq == 0 and s >= 3 * BLOCK
    tile = lambda w: pl.BlockSpec((1, tq, w), lambda b, n: (b, n, 0))
    whole = lambda rows, w: pl.BlockSpec((1, rows, w), lambda b, n: (b, 0, 0))
    return pl.pallas_call(
        functools.partial(_local_attn_kernel, seq=s, sub=sub, ahead=ahead),
        grid=(bn, s // tq),
        in_specs=[
            pl.BlockSpec(memory_space=pltpu.SMEM),
            tile(ATT_W), whole(KVX_W, s), whole(s, KVX_W),
            pl.BlockSpec((1, KVX_W, n_ctx), lambda b, n: (0, 0, b)), whole(n_ctx, KVX_W),
            pl.BlockSpec((3, BLOCK, 3 * BLOCK), lambda b, n: (0, 0, 0)),
            tile(ATT_W), whole(s, CONV_W),
        ] + _conv_specs(lambda b, n: (0, 0)),
        out_specs=[tile(ATT_W), tile(CONV_W)],
        out_shape=[jax.ShapeDtypeStruct((bn, s, ATT_W), BF16), jax.ShapeDtypeStruct((bn, s, CONV_W), BF16)],
        scratch_shapes=[pltpu.VMEM((s + 2 * CONV_HALO, CONV_W), F32)],
        compiler_params=SEQUENTIAL,
        name="local_attention",
    )(sink, q, *kv, *kvc, _band_bias(), gatt, uc, *conv_args)


def _ctx_attn_kernel(sink_ref, q_ref, ktc_ref, vc_ref, g_ref, uc_ref, cw_ref, cb_ref, lg_ref, lb_ref,
                     o_ref, cv_ref, pad_ref, *, n_ctx, ahead):
    _fill_conv_pad(pad_ref, uc_ref, n_ctx)
    rows = slice(0, n_ctx)
    chains, groups = [], []
    for kvh in range(N_KV_HEADS):
        ksets = []
        for e in range(2):
            blk = _kv_block(kvh, e)
            ksets.append([(ktc_ref[0, blk, :], vc_ref[0, :, blk], None)])
        chains += _group_chains(sink_ref, q_ref, rows, n_ctx, kvh, ksets)
        groups.append((rows, n_ctx, kvh))
    fillers = _conv_fillers(pad_ref, (cw_ref, cb_ref, lg_ref, lb_ref), cv_ref, 0, n_ctx)
    _run_chains(chains, _make_finish(groups, g_ref, o_ref), fillers, ahead)


def _ctx_attention(sink, q, kvc, gatt, uc, conv_args):
    bn, n_ctx, _ = q.shape
    ahead = CTX_AHEAD
    whole = lambda w: pl.BlockSpec((1, n_ctx, w), lambda b: (b, 0, 0))
    return pl.pallas_call(
        functools.partial(_ctx_attn_kernel, n_ctx=n_ctx, ahead=ahead),
        grid=(bn,),
        in_specs=[pl.BlockSpec(memory_space=pltpu.SMEM), whole(ATT_W),
                  pl.BlockSpec((1, KVX_W, n_ctx), lambda b: (0, 0, b)), whole(KVX_W), whole(ATT_W),
                  whole(CONV_W)] + _conv_specs(lambda b: (0, 0)),
        out_specs=[whole(ATT_W), whole(CONV_W)],
        out_shape=[jax.ShapeDtypeStruct((bn, n_ctx, ATT_W), BF16),
                   jax.ShapeDtypeStruct((bn, n_ctx, CONV_W), BF16)],
        scratch_shapes=[pltpu.VMEM((n_ctx + 2 * CONV_HALO, CONV_W), F32)],
        name="ctx_attention",
    )(sink, q, *kvc, gatt, uc, *conv_args)


def _fourier_prepare(ab_ref, cw_ref, sw_ref, rc_ref, rs_ref, half):
    top = ab_ref[0, 0:half, :].astype(F32)
    bot = ab_ref[0, half:2 * half, :].astype(F32)
    plus = top + bot
    minus = top - bot
    a_m, b_m = minus[:, :FOUR_W], minus[:, FOUR_W:]
    cw, sw = cw_ref[...], sw_ref[...]
    rc_ref[:, 0:FOUR_W] = plus[:, :FOUR_W].astype(BF16)
    rc_ref[:, FOUR_W:] = (cw * a_m - sw * b_m).astype(BF16)
    rs_ref[:, 0:FOUR_W] = plus[:, FOUR_W:].astype(BF16)
    rs_ref[:, FOUR_W:] = (sw * a_m + cw * b_m).astype(BF16)


def _fourier_rows(c_ref, s_ref, rc_ref, rs_ref, wf_ref, bf_ref, il_ref, r0, chunk):
    e = _dot(c_ref[pl.ds(r0, chunk), :], rc_ref[...]) - _dot(s_ref[pl.ds(r0, chunk), :], rs_ref[...])
    eb = e.astype(BF16)
    for p in range(2):
        f = _dot(eb[:, p * FOUR_W:(p + 1) * FOUR_W], wf_ref[...])
        for lt in range(FOUR_W // LANES):
            il_ref[lt, pl.ds(p, chunk, stride=2), :] = f[:, lt * LANES:(lt + 1) * LANES]
    return jnp.concatenate([il_ref[lt] for lt in range(FOUR_W // LANES)], axis=1) + bf_ref[...]


def _out_kernel(x_ref, att_ref, cv_ref, gcv_ref, ab_ref, gf_ref, c_ref, s_ref, cw_ref, sw_ref, wf_ref, bf_ref,
                w_ref, gate_ref, fg_ref, o_ref, rc_ref, rs_ref, il_ref, wbf_ref, *, tm, half, final):
    _cast_weight_once(w_ref, wbf_ref)

    @pl.when(pl.program_id(1) == 0)
    def _():
        _fourier_prepare(ab_ref, cw_ref, sw_ref, rc_ref, rs_ref, half)

    r0 = pl.multiple_of(pl.program_id(1) * (tm // 2), tm // 2)
    four = _fourier_rows(c_ref, s_ref, rc_ref, rs_ref, wf_ref, bf_ref, il_ref, r0, tm // 2)
    fo = (four * gf_ref[0].astype(F32)).astype(BF16)
    cv = cv_ref[0] * gcv_ref[0]
    y = (_dot(jnp.concatenate([att_ref[0], cv], axis=1), wbf_ref[0:ATT_W + CONV_W, :])
         + _dot(fo, wbf_ref[ATT_W + CONV_W:, :]))
    xn = x_ref[0] + gate_ref[0] * y
    if final:
        ms = jnp.mean(xn * xn, axis=-1, keepdims=True)
        xn = xn * lax.rsqrt(ms + EPS) * fg_ref[...]
    o_ref[0] = xn


def _out_proj(x, att, cv, gcv, ab, gf, pos_tabs, w_four_bf, b_four, w_out, mod3, final_g,
              *, tm, mod_row, final, layer):
    bn, s, _ = x.shape
    half = s // 2
    c_half, s_half, cw, sw = pos_tabs
    tile = lambda w: pl.BlockSpec((1, tm, w), lambda b, i: (b, i, 0))
    const = lambda shape: pl.BlockSpec(shape, lambda b, i: (0,) * len(shape))
    return pl.pallas_call(
        functools.partial(_out_kernel, tm=tm, half=half, final=final),
        grid=(bn, s // tm),
        in_specs=[
            tile(D_MODEL), tile(ATT_W), tile(CONV_W), tile(CONV_W),
            pl.BlockSpec((1, s, 2 * FOUR_W), lambda b, i: (b, 0, 0)),
            tile(FOUR_W),
            const((half, half)), const((half, half)), const((half, FOUR_W)), const((half, FOUR_W)),
            const((FOUR_W, FOUR_W)), const((1, FOUR_W)),
            pl.BlockSpec((1, D_MODEL, D_MODEL), lambda b, i: (layer, 0, 0)),
            _mod_spec(mod_row, 2),
            const((1, D_MODEL)),
        ],
        out_specs=tile(D_MODEL),
        out_shape=jax.ShapeDtypeStruct((bn, s, D_MODEL), F32),
        scratch_shapes=[pltpu.VMEM((half, 2 * FOUR_W), BF16), pltpu.VMEM((half, 2 * FOUR_W), BF16),
                        pltpu.VMEM((FOUR_W // LANES, tm, LANES), F32),
                        pltpu.VMEM((D_MODEL, D_MODEL), BF16)],
        compiler_params=SEQUENTIAL,
        name="out_proj",
    )(x, att, cv, gcv, ab, gf, c_half, s_half, cw, sw, w_four_bf, b_four.reshape(1, FOUR_W),
      w_out, mod3, final_g.reshape(1, D_MODEL))


def _rope_tables(s):
    t = np.arange(s)
    half = HEAD_DIM // 4
    freqs = ROPE_BASE ** (-np.arange(half) / half)
    zero = np.zeros((s, half))

    def per_head(fn_lo, fn_hi):
        parts = []
        for pos in (t // GRID_W, t % GRID_W):
            ang = pos[:, None] * freqs[None, :]
            parts += [fn_lo(ang), fn_hi(ang)]
        return np.tile(np.concatenate(parts, axis=-1), (1, LANES // HEAD_DIM))

    cos = per_head(np.cos, np.cos)
    sin_lo = per_head(lambda a: -np.sin(a), lambda a: zero)
    sin_hi = per_head(lambda a: zero, np.sin)
    tab = np.concatenate([cos * Q_SCALE, sin_lo * Q_SCALE, sin_hi * Q_SCALE, cos, sin_lo, sin_hi], axis=-1)
    return jnp.asarray(tab.astype(np.float32))


def _identity_rope_tables(s):
    one = np.ones((s, LANES))
    zero = np.zeros((s, LANES))
    return jnp.asarray(np.concatenate([one * Q_SCALE, zero, zero, one, zero, zero], axis=-1).astype(np.float32))


def _channel_dft(n_pos):
    c = np.arange(FOUR_HEAD_DIM)
    ang = 2.0 * np.pi * ((c[:, None] * c[None, :]) % FOUR_HEAD_DIM) / FOUR_HEAD_DIM
    eye = np.eye(FOUR_HEADS)
    scale = 1.0 / np.sqrt(float(n_pos * FOUR_HEAD_DIM))
    tab = np.concatenate([np.kron(eye, np.cos(ang)), np.kron(eye, np.sin(ang))], axis=1) * scale
    return jnp.asarray(tab.astype(np.float32))


def _position_dft(n_pos):
    half = n_pos // 2
    k = np.arange(half)
    ang = 2.0 * np.pi * ((k[:, None] * k[None, :]) % half) / half
    beta = np.pi * k / half
    ones = np.ones((1, FOUR_W))
    f32 = lambda a: jnp.asarray(a.astype(np.float32))
    return (f32(np.cos(ang)).astype(BF16), f32(np.sin(ang)).astype(BF16),
            f32(np.cos(beta)[:, None] * ones), f32(np.sin(beta)[:, None] * ones))


def kernel(x, c, ctx, c_ctx, w_ada, b_ada, norm_g, w_in, attn_sink, conv_w, conv_b,
           conv_ln_g, conv_ln_b, w_four, b_four, w_out, final_g):
    bn, s, _ = x.shape
    n_ctx = ctx.shape[1]
    assert bn < MOD_ROWS
    ctx_row = bn

    cc = jnp.zeros((MOD_ROWS, D_MODEL), F32).at[:bn].set(c).at[ctx_row].set(c_ctx)
    mod = _modulation(cc, w_ada, b_ada)

    w_four_bf = w_four.astype(BF16)
    rope_x = _rope_tables(s)
    rope_c = _identity_rope_tables(bn * n_ctx)
    dft_x = _channel_dft(s).astype(BF16)
    dft_ctx = _channel_dft(n_ctx).astype(BF16)
    pos_x = _position_dft(s)
    pos_ctx = _position_dft(n_ctx)

    h_ctx = ctx
    for l in range(DEPTH):
        mod3 = mod[l].reshape(MOD_ROWS, 1, 3 * D_MODEL)
        conv_args = _conv_args(conv_w[l], conv_b[l], conv_ln_g[l], conv_ln_b[l])
        four_args = (w_four_bf[l], b_four[l], w_out, mod3, final_g)
        sink = attn_sink[l]
        last = l == DEPTH - 1
        ctx_flat = h_ctx.reshape(1, bn * n_ctx, D_MODEL)
        per_batch = lambda a: a.reshape(bn, n_ctx, a.shape[-1])
        if not last:
            qc, ktc, *rest = _in_proj(ctx_flat, mod3, norm_g[l], w_in, rope_c, dft_ctx,
                                      tm=IN_TM, mod_row=ctx_row, layer=l)
            vc, gatt_c, gcv_c, ab_c, gf_c, uc_c = map(per_batch, rest)
            kvc = (ktc, vc)
            att_c, cv_c = _ctx_attention(sink, per_batch(qc), kvc, gatt_c, uc_c, conv_args)
        else:
            ktc, vc = _kv_proj(ctx_flat, mod3, norm_g[l], w_in, tm=IN_TM, mod_row=ctx_row, layer=l)
            kvc = (ktc, per_batch(vc))
        q, kt, v, gatt, gcv, ab, gf, uc = _in_proj(
            x, mod3, norm_g[l], w_in, rope_x, dft_x, tm=IN_TM, mod_row=None, layer=l)
        att, cv = _local_attention(sink, q, (kt, v), kvc, gatt, uc, conv_args)
        if not last:
            h_ctx = _out_proj(h_ctx, att_c, cv_c, gcv_c, ab_c, gf_c, pos_ctx, *four_args,
                              tm=n_ctx, mod_row=ctx_row, final=False, layer=l)
        x = _out_proj(x, att, cv, gcv, ab, gf, pos_x, *four_args, tm=OUT_TM, mod_row=None, final=last, layer=l)
    return x
```
